```python
import jax, jax.numpy as jnp
from jax import lax
import numpy as np

D_MODEL = 1024
BATCH = 2
SEQ = 16384
DEPTH = 4

N_MIXERS = 2
N_HEADS = 16
HEAD_DIM = D_MODEL // N_HEADS
Q_BLOCK = 128
CONV_WIDTH = 3
D_FF = 4 * D_MODEL
PLE_DIM = 256
N_ATTN = (DEPTH + 1) // 2
N_CONV = DEPTH // 2
RMS_EPS = 1e-6
NEG_INF = -1e30

kernel_name = "fox_shortconv_hybrid_trunk"


def rmsnorm(x, g):
    xf = x.astype(jnp.float32)
    y = xf * lax.rsqrt(jnp.mean(xf * xf, axis=-1, keepdims=True) + RMS_EPS)
    return (y * g.astype(jnp.float32)).astype(x.dtype)


def fox_attention(h, w_in, b_f, w_out):
    B, S, D = h.shape
    proj = h @ w_in
    q = proj[..., :D].reshape(B, S, N_HEADS, HEAD_DIM).transpose(0, 2, 1, 3)
    k = proj[..., D:2 * D].reshape(B, S, N_HEADS, HEAD_DIM).transpose(0, 2, 1, 3)
    v = proj[..., 2 * D:3 * D].reshape(B, S, N_HEADS, HEAD_DIM).transpose(0, 2, 1, 3)
    q = q * jnp.asarray(HEAD_DIM ** -0.5, q.dtype)
    f_logit = (proj[..., 3 * D:] + b_f).astype(jnp.float32)
    log_f = jax.nn.log_sigmoid(f_logit)
    c = lax.cumsum(log_f, axis=1).transpose(0, 2, 1)
    k_pos = jnp.arange(S)

    def q_block(i):
        start = i * Q_BLOCK
        qb = lax.dynamic_slice_in_dim(q, start, Q_BLOCK, axis=2)
        cb = lax.dynamic_slice_in_dim(c, start, Q_BLOCK, axis=2)
        s = jnp.einsum('bhqd,bhkd->bhqk', qb, k, preferred_element_type=jnp.float32)
        s = s + cb[..., :, None] - c[..., None, :]
        q_pos = start + jnp.arange(Q_BLOCK)
        s = jnp.where(k_pos[None, :] <= q_pos[:, None], s, NEG_INF)
        pr = jax.nn.softmax(s, axis=-1)
        return jnp.einsum('bhqk,bhkd->bhqd', pr.astype(v.dtype), v)

    o = lax.map(q_block, jnp.arange(S // Q_BLOCK))
    o = o.transpose(1, 0, 3, 2, 4).reshape(B, S, D)
    return o @ w_out


def short_conv(h, w_in, conv_w, w_out):
    D = h.shape[-1]
    proj = h @ w_in
    b_gate = proj[..., :D]
    c_gate = proj[..., D:2 * D]
    u = proj[..., 2 * D:]
    z = c_gate * u
    zc = lax.conv_general_dilated(
        z, conv_w[:, None, :].astype(z.dtype), window_strides=(1,),
        padding=[(CONV_WIDTH - 1, 0)],
        dimension_numbers=('NWC', 'WIO', 'NWC'),
        feature_group_count=D)
    return (b_gate * zc) @ w_out


def sq_relu_mlp(h, w_up, w_down):
    return jnp.square(jax.nn.relu(h @ w_up)) @ w_down


def setup_inputs(seed: int = 0) -> dict:
    key = jax.random.key(seed)
    ks = jax.random.split(key, 14)
    f32 = jnp.float32
    nrm = lambda k, shape, scale: jax.random.normal(k, shape, f32) * scale
    x = jax.random.normal(ks[0], (BATCH, SEQ, D_MODEL), f32)
    p = jax.random.normal(ks[1], (DEPTH, BATCH, SEQ, PLE_DIM), f32)
    norm_g = 1.0 + nrm(ks[2], (DEPTH, 6, D_MODEL), 0.05)
    w_attn_in = nrm(ks[3], (N_ATTN, D_MODEL, 3 * D_MODEL + N_HEADS), D_MODEL ** -0.5)
    b_forget = 2.0 + nrm(ks[4], (N_ATTN, N_HEADS), 0.5)
    w_attn_out = nrm(ks[5], (N_ATTN, D_MODEL, D_MODEL), D_MODEL ** -0.5)
    w_conv_in = nrm(ks[6], (N_CONV, D_MODEL, 3 * D_MODEL), D_MODEL ** -0.5)
    conv_w = nrm(ks[7], (N_CONV, CONV_WIDTH, D_MODEL), CONV_WIDTH ** -0.5)
    w_conv_out = nrm(ks[8], (N_CONV, D_MODEL, D_MODEL), D_MODEL ** -0.5)
    w_mlp_up = nrm(ks[9], (DEPTH, D_MODEL, D_FF), D_MODEL ** -0.5)
    w_mlp_down = nrm(ks[10], (DEPTH, D_FF, D_MODEL), D_FF ** -0.5)
    w_ple_proj = nrm(ks[11], (DEPTH, PLE_DIM, D_MODEL), PLE_DIM ** -0.5)
    w_ple_gate = nrm(ks[12], (DEPTH, D_MODEL, D_MODEL), D_MODEL ** -0.5)
    return {"x": x, "p": p, "norm_g": norm_g, "w_attn_in": w_attn_in,
            "b_forget": b_forget, "w_attn_out": w_attn_out, "w_conv_in": w_conv_in,
            "conv_w": conv_w, "w_conv_out": w_conv_out, "w_mlp_up": w_mlp_up,
            "w_mlp_down": w_mlp_down, "w_ple_proj": w_ple_proj, "w_ple_gate": w_ple_gate}


def reference(x, p, norm_g, w_attn_in, b_forget, w_attn_out, w_conv_in, conv_w,
              w_conv_out, w_mlp_up, w_mlp_down, w_ple_proj, w_ple_gate):
    for i in range(DEPTH):
        g = norm_g[i]
        hn = rmsnorm(x, g[0])
        j = i // N_MIXERS
        if i % N_MIXERS == 0:
            m = fox_attention(hn, w_attn_in[j], b_forget[j], w_attn_out[j])
        else:
            m = short_conv(hn, w_conv_in[j], conv_w[j], w_conv_out[j])
        x = x + rmsnorm(m, g[1])
        f = sq_relu_mlp(rmsnorm(x, g[2]), w_mlp_up[i], w_mlp_down[i])
        x = x + rmsnorm(f, g[3])
        gate = jax.nn.sigmoid(rmsnorm(x, g[4]) @ w_ple_gate[i])
        e = (p[i] @ w_ple_proj[i]) * gate
        x = x + rmsnorm(e, g[5])
    return x
```

```python
import functools

import numpy as np
import jax
import jax.numpy as jnp
from jax import lax
from jax.experimental import pallas as pl
from jax.experimental.pallas import tpu as pltpu

F32 = jnp.float32
BF16 = jnp.bfloat16

D_MODEL = 1024
N_HEADS = 16
HEAD_DIM = D_MODEL // N_HEADS
HEAD_PAD = 128
QKV_PAD = N_HEADS * HEAD_PAD
CONV_WIDTH = 3
RMS_EPS = 1e-6
NEG_INF = -1e30
LOG2E = 1.4426950408889634
Q_SCALE = (HEAD_DIM ** -0.5) * LOG2E

VMEM_LIMIT_BYTES = 56 * 1024 * 1024

ROW_TILE = 512
ATTN_BLOCK = 1024

LANE_CQ = (64, 65, 66)
LANE_CK = (67, 68, 69)
LANE_ONE_V = 64
ROW_ONES = 48


def _rms(xf, g):
    ms = jnp.mean(xf * xf, axis=-1, keepdims=True)
    return xf * lax.rsqrt(ms + RMS_EPS) * g


def _split3(x):
    hi = x.astype(BF16).astype(F32)
    r = x - hi
    mid = r.astype(BF16).astype(F32)
    lo = (r - mid).astype(BF16).astype(F32)
    return hi, mid, lo


def _placement_matrix():
    p = np.zeros((HEAD_PAD, 3 * QKV_PAD), np.float32)
    for h in range(N_HEADS):
        qb = h * HEAD_PAD
        kb = QKV_PAD + h * HEAD_PAD
        vb = 2 * QKV_PAD + h * HEAD_PAD
        for part in range(3):
            p[part * N_HEADS + h, qb + LANE_CQ[part]] = 1.0
            p[part * N_HEADS + h, kb + LANE_CK[part]] = -1.0
            p[ROW_ONES, qb + LANE_CK[part]] = 1.0
            p[ROW_ONES, kb + LANE_CQ[part]] = 1.0
        p[ROW_ONES, vb + LANE_ONE_V] = 1.0
    return p


def _attn_in_kernel(x_ref, g_ref, w_ref, wf_ref, bf_ref, p_ref,
                    q_ref, k_ref, v_ref, carry_ref, *, tm):
    @pl.when(pl.program_id(1) == 0)
    def _():
        carry_ref[...] = jnp.zeros_like(carry_ref)

    h = _rms(x_ref[...], g_ref[...]).astype(BF16)

    f = jnp.dot(h, wf_ref[...], preferred_element_type=F32) + bf_ref[...]
    logf = (jnp.minimum(f, 0.0) - jnp.log1p(jnp.exp(-jnp.abs(f)))) * LOG2E

    row = lax.broadcasted_iota(jnp.int32, (tm, tm), 0)
    col = lax.broadcasted_iota(jnp.int32, (tm, tm), 1)
    tri = (row >= col).astype(BF16)
    c = carry_ref[...]
    for part in _split3(logf):
        c = c + jnp.dot(tri, part.astype(BF16), preferred_element_type=F32)
    carry_ref[...] = c[tm - 1:tm, :]

    chi, cmid, clo = _split3(c)
    lane = lax.broadcasted_iota(jnp.int32, (tm, HEAD_PAD), 1)
    a = jnp.where(lane < N_HEADS, chi,
                  jnp.where(lane < 2 * N_HEADS, cmid,
                            jnp.where(lane < 3 * N_HEADS, clo,
                                      jnp.where(lane == ROW_ONES, 1.0, 0.0))))
    a = a.astype(BF16)

    for sec, (o_ref, scale) in enumerate(((q_ref, Q_SCALE), (k_ref, None), (v_ref, None))):
        cols = slice(sec * QKV_PAD, (sec + 1) * QKV_PAD)
        proj = jnp.dot(h, w_ref[:, cols], preferred_element_type=F32)
        if scale is not None:
            proj = proj * scale
        proj = proj + jnp.dot(a, p_ref[:, cols], preferred_element_type=F32)
        for hh in range(N_HEADS):
            o_ref[hh] = proj[:, hh * HEAD_PAD:(hh + 1) * HEAD_PAD].astype(BF16)


def _attn_in(x, g, w_pad, wf3, bf3, place, *, tm=ROW_TILE):
    b, s, d = x.shape
    grid = (b, s // tm)
    head_shape = jax.ShapeDtypeStruct((b, N_HEADS, s, HEAD_PAD), BF16)
    head_spec = pl.BlockSpec((None, N_HEADS, tm, HEAD_PAD), lambda bi, i: (bi, 0, i, 0))
    const = lambda bi, i: (0, 0)
    return pl.pallas_call(
        functools.partial(_attn_in_kernel, tm=tm),
        grid=grid,
        in_specs=[
            pl.BlockSpec((None, tm, d), lambda bi, i: (bi, i, 0)),
            pl.BlockSpec((1, d), const),
            pl.BlockSpec(w_pad.shape, const),
            pl.BlockSpec(wf3.shape, const),
            pl.BlockSpec(bf3.shape, const),
            pl.BlockSpec(place.shape, const),
        ],
        out_specs=[head_spec, head_spec, head_spec],
        out_shape=[head_shape, head_shape, head_shape],
        scratch_shapes=[pltpu.VMEM((1, HEAD_PAD), F32)],
        compiler_params=pltpu.CompilerParams(
            dimension_semantics=("arbitrary", "arbitrary"),
            vmem_limit_bytes=VMEM_LIMIT_BYTES),
        name="attn_in",
    )(x, g, w_pad, wf3, bf3, place)


def _flash_kernel(q_ref, k_ref, v_ref, o_ref, m_ref, acc_ref, *, blk):
    i = pl.program_id(2)
    m_ref[...] = jnp.full_like(m_ref, -jnp.inf)
    acc_ref[...] = jnp.zeros_like(acc_ref)
    q = q_ref[...]

    def block(j, masked):
        start = pl.multiple_of(j * blk, blk)
        k = k_ref[pl.ds(start, blk), :]
        v = v_ref[pl.ds(start, blk), :]
        s = lax.dot_general(q, k, (((1,), (1,)), ((), ())), preferred_element_type=F32)
        if masked:
            row = lax.broadcasted_iota(jnp.int32, (blk, blk), 0)
            col = lax.broadcasted_iota(jnp.int32, (blk, blk), 1)
            s = jnp.where(col <= row, s, NEG_INF)
        m_prev = m_ref[...]
        m_new = jnp.maximum(m_prev, jnp.max(s, axis=-1, keepdims=True))
        alpha = jnp.exp2(m_prev - m_new)
        p = jnp.exp2(s - m_new).astype(BF16)
        acc_ref[...] = alpha * acc_ref[...] + jnp.dot(p, v, preferred_element_type=F32)
        m_ref[...] = m_new

    def body(j, carry):
        block(j, masked=False)
        return carry

    lax.fori_loop(0, i, body, 0)
    block(i, masked=True)

    acc = acc_ref[...]
    o_ref[...] = (acc / acc[:, LANE_ONE_V:LANE_ONE_V + 1]).astype(o_ref.dtype)


def _flash(q, k, v, *, blk=ATTN_BLOCK):
    b, nh, s, hp = q.shape
    grid = (b, nh, s // blk)
    return pl.pallas_call(
        functools.partial(_flash_kernel, blk=blk),
        grid=grid,
        in_specs=[
            pl.BlockSpec((None, None, blk, hp), lambda bi, h, i: (bi, h, i, 0)),
            pl.BlockSpec((None, None, s, hp), lambda bi, h, i: (bi, h, 0, 0)),
            pl.BlockSpec((None, None, s, hp), lambda bi, h, i: (bi, h, 0, 0)),
        ],
        out_specs=pl.BlockSpec((None, None, blk, hp), lambda bi, h, i: (bi, h, i, 0)),
        out_shape=jax.ShapeDtypeStruct((b, nh, s, hp), BF16),
        scratch_shapes=[pltpu.VMEM((blk, 1), F32), pltpu.VMEM((blk, hp), F32)],
        compiler_params=pltpu.CompilerParams(
            dimension_semantics=("arbitrary", "arbitrary", "arbitrary"),
            vmem_limit_bytes=VMEM_LIMIT_BYTES),
        name="fox_flash",
    )(q, k, v)


def _attn_out_kernel(o_ref, x_ref, w_ref, g_ref, y_ref):
    o = jnp.concatenate([o_ref[hh] for hh in range(N_HEADS)], axis=-1)
    m = jnp.dot(o, w_ref[...], preferred_element_type=F32)
    y_ref[...] = x_ref[...] + _rms(m, g_ref[...])


def _attn_out(o, x, w_pad, g, *, tm=ROW_TILE):
    b, s, d = x.shape
    const = lambda bi, i: (0, 0)
    return pl.pallas_call(
        _attn_out_kernel,
        grid=(b, s // tm),
        in_specs=[
            pl.BlockSpec((None, N_HEADS, tm, HEAD_PAD), lambda bi, i: (bi, 0, i, 0)),
            pl.BlockSpec((None, tm, d), lambda bi, i: (bi, i, 0)),
            pl.BlockSpec(w_pad.shape, const),
            pl.BlockSpec((1, d), const),
        ],
        out_specs=pl.BlockSpec((None, tm, d), lambda bi, i: (bi, i, 0)),
        out_shape=jax.ShapeDtypeStruct(x.shape, F32),
        compiler_params=pltpu.CompilerParams(
            dimension_semantics=("arbitrary", "arbitrary"),
            vmem_limit_bytes=VMEM_LIMIT_BYTES),
        name="attn_out",
    )(o, x, w_pad, g)


HALO = 8


def _conv_kernel(x_ref, g0_ref, win_ref, cw_ref, wout_ref, g1_ref, y_ref, z_ref, *, tm):
    d = D_MODEL

    @pl.when(pl.program_id(1) == 0)
    def _():
        z_ref[0:HALO, :] = jnp.zeros((HALO, d), F32)

    @pl.when(pl.program_id(1) > 0)
    def _():
        z_ref[0:HALO, :] = z_ref[tm:tm + HALO, :]

    x = x_ref[...]
    h = _rms(x, g0_ref[...]).astype(BF16)
    c_gate = jnp.dot(h, win_ref[:, d:2 * d], preferred_element_type=F32)
    u = jnp.dot(h, win_ref[:, 2 * d:3 * d], preferred_element_type=F32)
    z = c_gate * u
    z_ref[HALO:HALO + tm, :] = z
    cw = cw_ref[...]
    zc = (cw[2:3, :] * z
          + cw[1:2, :] * z_ref[HALO - 1:HALO - 1 + tm, :]
          + cw[0:1, :] * z_ref[HALO - 2:HALO - 2 + tm, :])
    b_gate = jnp.dot(h, win_ref[:, 0:d], preferred_element_type=F32)
    y = (b_gate * zc).astype(BF16)
    m = jnp.dot(y, wout_ref[...], preferred_element_type=F32)
    y_ref[...] = x + _rms(m, g1_ref[...])


def _conv_layer(x, g0, w_in, conv_w, w_out, g1, *, tm=ROW_TILE):
    b, s, d = x.shape
    const = lambda bi, i: (0, 0)
    return pl.pallas_call(
        functools.partial(_conv_kernel, tm=tm),
        grid=(b, s // tm),
        in_specs=[
            pl.BlockSpec((None, tm, d), lambda bi, i: (bi, i, 0)),
            pl.BlockSpec((1, d), const),
            pl.BlockSpec(w_in.shape, const),
            pl.BlockSpec(conv_w.shape, const),
            pl.BlockSpec(w_out.shape, const),
            pl.BlockSpec((1, d), const),
        ],
        out_specs=pl.BlockSpec((None, tm, d), lambda bi, i: (bi, i, 0)),
        out_shape=jax.ShapeDtypeStruct(x.shape, F32),
        scratch_shapes=[pltpu.VMEM((tm + HALO, d), F32)],
        compiler_params=pltpu.CompilerParams(
            dimension_semantics=("arbitrary", "arbitrary"),
            vmem_limit_bytes=VMEM_LIMIT_BYTES),
        name="conv_mixer",
    )(x, g0, w_in, conv_w, w_out, g1)


FF_CHUNK = 1024


def _mlp_kernel(x_ref, g2_ref, wup_ref, wdn_ref, g3_ref, y_ref):
    x = x_ref[...]
    h = _rms(x, g2_ref[...]).astype(BF16)
    d_ff = wup_ref.shape[1]
    f = None
    for c in range(d_ff // FF_CHUNK):
        cols = slice(c * FF_CHUNK, (c + 1) * FF_CHUNK)
        u = jnp.dot(h, wup_ref[:, cols], preferred_element_type=F32)
        a = jnp.square(jnp.maximum(u, 0.0)).astype(BF16)
        part = jnp.dot(a, wdn_ref[cols, :], preferred_element_type=F32)
        f = part if f is None else f + part
    y_ref[...] = x + _rms(f, g3_ref[...])


def _mlp(x, g2, w_up, w_down, g3, *, tm=ROW_TILE):
    b, s, d = x.shape
    const = lambda bi, i: (0, 0)
    return pl.pallas_call(
        _mlp_kernel,
        grid=(b, s // tm),
        in_specs=[
            pl.BlockSpec((None, tm, d), lambda bi, i: (bi, i, 0)),
            pl.BlockSpec((1, d), const),
            pl.BlockSpec(w_up.shape, const),
            pl.BlockSpec(w_down.shape, const),
            pl.BlockSpec((1, d), const),
        ],
        out_specs=pl.BlockSpec((None, tm, d), lambda bi, i: (bi, i, 0)),
        out_shape=jax.ShapeDtypeStruct(x.shape, F32),
        compiler_params=pltpu.CompilerParams(
            dimension_semantics=("arbitrary", "arbitrary"),
            vmem_limit_bytes=VMEM_LIMIT_BYTES),
        name="sq_relu_mlp",
    )(x, g2, w_up, w_down, g3)


def _ple_kernel(x_ref, p_ref, g4_ref, wg_ref, wp_ref, g5_ref, y_ref):
    x = x_ref[...]
    h = _rms(x, g4_ref[...]).astype(BF16)
    gate = jax.nn.sigmoid(jnp.dot(h, wg_ref[...], preferred_element_type=F32))
    e = jnp.dot(p_ref[...].astype(BF16), wp_ref[...], preferred_element_type=F32) * gate
    y_ref[...] = x + _rms(e, g5_ref[...])


def _ple(x, p, layer, g4, w_gate, w_proj, g5, *, tm=ROW_TILE):
    b, s, d = x.shape
    const = lambda bi, i: (0, 0)
    return pl.pallas_call(
        _ple_kernel,
        grid=(b, s // tm),
        in_specs=[
            pl.BlockSpec((None, tm, d), lambda bi, i: (bi, i, 0)),
            pl.BlockSpec((None, None, tm, p.shape[-1]), lambda bi, i: (layer, bi, i, 0)),
            pl.BlockSpec((1, d), const),
            pl.BlockSpec(w_gate.shape, const),
            pl.BlockSpec(w_proj.shape, const),
            pl.BlockSpec((1, d), const),
        ],
        out_specs=pl.BlockSpec((None, tm, d), lambda bi, i: (bi, i, 0)),
        out_shape=jax.ShapeDtypeStruct(x.shape, F32),
        compiler_params=pltpu.CompilerParams(
            dimension_semantics=("arbitrary", "arbitrary"),
            vmem_limit_bytes=VMEM_LIMIT_BYTES),
        name="ple_gate",
    )(x, p, g4, w_gate, w_proj, g5)


def _pad_head_cols(w):
    d = w.shape[0]
    w = w.reshape(d, N_HEADS, HEAD_DIM)
    w = jnp.pad(w, ((0, 0), (0, 0), (0, HEAD_PAD - HEAD_DIM)))
    return w.reshape(d, QKV_PAD)


def _prep_attn_weights(w_in, b_f, w_out):
    d = D_MODEL
    w_pad = jnp.concatenate(
        [_pad_head_cols(w_in[:, s * d:(s + 1) * d]) for s in range(3)], axis=1).astype(BF16)
    wf = w_in[:, 3 * d:]
    wf3 = jnp.pad(jnp.concatenate([wf, wf, wf], axis=1),
                  ((0, 0), (0, HEAD_PAD - 3 * N_HEADS))).astype(BF16)
    bf3 = jnp.pad(jnp.concatenate([b_f, b_f, b_f]), (0, HEAD_PAD - 3 * N_HEADS))
    bf3 = bf3.reshape(1, HEAD_PAD).astype(F32)
    w_out_pad = jnp.pad(w_out.reshape(N_HEADS, HEAD_DIM, d),
                        ((0, 0), (0, HEAD_PAD - HEAD_DIM), (0, 0)))
    w_out_pad = w_out_pad.reshape(QKV_PAD, d).astype(BF16)
    return w_pad, wf3, bf3, w_out_pad


def kernel(x, p, norm_g, w_attn_in, b_forget, w_attn_out, w_conv_in, conv_w, w_conv_out,
           w_mlp_up, w_mlp_down, w_ple_proj, w_ple_gate):
    depth = norm_g.shape[0]
    place = jnp.asarray(_placement_matrix(), dtype=BF16)
    for i in range(depth):
        g = norm_g[i].astype(F32)
        gi = lambda n: g[n:n + 1]
        j = i // 2
        if i % 2 == 0:
            w_pad, wf3, bf3, w_out_pad = _prep_attn_weights(
                w_attn_in[j], b_forget[j], w_attn_out[j])
            q, k, v = _attn_in(x, gi(0), w_pad, wf3, bf3, place)
            o = _flash(q, k, v)
            x = _attn_out(o, x, w_out_pad, gi(1))
        else:
            x = _conv_layer(x, gi(0), w_conv_in[j].astype(BF16), conv_w[j].astype(F32),
                            w_conv_out[j].astype(BF16), gi(1))
        x = _mlp(x, gi(2), w_mlp_up[i].astype(BF16), w_mlp_down[i].astype(BF16), gi(3))
        x = _ple(x, p, i, gi(4), w_ple_gate[i].astype(BF16), w_ple_proj[i].astype(BF16),
                 gi(5))
    return x
```

```python
import functools

import numpy as np
import jax
import jax.numpy as jnp
from jax import lax
from jax.experimental import pallas as pl
from jax.experimental.pallas import tpu as pltpu

F32 = jnp.float32
BF16 = jnp.bfloat16

D_MODEL = 1024
N_HEADS = 16
HEAD_DIM = D_MODEL // N_HEADS
HEAD_PAD = 128
QK_PAD = N_HEADS * HEAD_PAD
V_ROWS = 80
V_ONES_ROW = HEAD_DIM
CONV_WIDTH = 3
RMS_EPS = 1e-6
NEG_INF = -1e30
LOG2E = 1.4426950408889634
Q_SCALE = (HEAD_DIM ** -0.5) * LOG2E

VMEM_LIMIT_BYTES = 56 * 1024 * 1024

ROW_TILE = 512
ATTN_BLOCK = ROW_TILE
HEADS_PER_STEP = 1

SLOT_CQ = (64, 65, 66)
SLOT_CK = (67, 68, 69)
ROW_ONES = 48


def _rms(xf, g):
    ms = jnp.mean(xf * xf, axis=-1, keepdims=True)
    return xf * lax.rsqrt(ms + RMS_EPS) * g


def _split3(x):
    hi = x.astype(BF16).astype(F32)
    r = x - hi
    mid = r.astype(BF16).astype(F32)
    lo = (r - mid).astype(BF16).astype(F32)
    return hi, mid, lo


def _placement_matrices():
    pq = np.zeros((HEAD_PAD, QK_PAD), np.float32)
    pk = np.zeros((HEAD_PAD, QK_PAD), np.float32)
    for h in range(N_HEADS):
        base = h * HEAD_PAD
        for part in range(3):
            pq[part * N_HEADS + h, base + SLOT_CQ[part]] = 1.0
            pk[part * N_HEADS + h, base + SLOT_CK[part]] = -1.0
            pq[ROW_ONES, base + SLOT_CK[part]] = 1.0
            pk[ROW_ONES, base + SLOT_CQ[part]] = 1.0
    return pq.T.copy(), pk


def _attn_in_kernel(x_ref, g_ref, wq_ref, wk_ref, wv_ref, wf_ref, bf_ref, pq_ref, pk_ref,
                    q_ref, k_ref, v_ref, carry_ref, *, tm):
    @pl.when(pl.program_id(1) == 0)
    def _():
        carry_ref[...] = jnp.zeros_like(carry_ref)

    hn = _rms(x_ref[...], g_ref[...])
    h = hn.astype(BF16)
    ht = hn.T.astype(BF16)

    f = jnp.dot(wf_ref[...], ht, preferred_element_type=F32) + bf_ref[...]
    logf = (jnp.minimum(f, 0.0) - jnp.log1p(jnp.exp(-jnp.abs(f)))) * LOG2E

    row = lax.broadcasted_iota(jnp.int32, (tm, tm), 0)
    col = lax.broadcasted_iota(jnp.int32, (tm, tm), 1)
    triu = (row <= col).astype(BF16)
    c = carry_ref[...]
    for part in _split3(logf):
        c = c + jnp.dot(part.astype(BF16), triu, preferred_element_type=F32)
    carry_ref[...] = c[:, tm - 1:tm]

    chi, cmid, clo = _split3(c)
    sub = lax.broadcasted_iota(jnp.int32, (HEAD_PAD, tm), 0)
    bias_t = jnp.where(sub < N_HEADS, chi,
                       jnp.where(sub < 2 * N_HEADS, cmid,
                                 jnp.where(sub < 3 * N_HEADS, clo,
                                           jnp.where(sub == ROW_ONES, 1.0, 0.0))))
    bias = bias_t.T.astype(BF16)
    bias_t = bias_t.astype(BF16)

    kk = (jnp.dot(h, wk_ref[...], preferred_element_type=F32)
          + jnp.dot(bias, pk_ref[...], preferred_element_type=F32))
    for hh in range(N_HEADS):
        k_ref[hh] = kk[:, hh * HEAD_PAD:(hh + 1) * HEAD_PAD].astype(BF16)

    qt = (jnp.dot(wq_ref[...], ht, preferred_element_type=F32) * Q_SCALE
          + jnp.dot(pq_ref[...], bias_t, preferred_element_type=F32))
    for hh in range(N_HEADS):
        q_ref[hh] = qt[hh * HEAD_PAD:(hh + 1) * HEAD_PAD, :].astype(BF16)

    vt = jnp.dot(wv_ref[...], ht, preferred_element_type=F32)
    ones_row = lax.broadcasted_iota(jnp.int32, (V_ROWS, tm), 0) == V_ONES_ROW
    for hh in range(N_HEADS):
        v_ref[hh] = jnp.where(ones_row, 1.0, vt[hh * V_ROWS:(hh + 1) * V_ROWS, :]).astype(BF16)


def _attn_in(x, g, wq_t, wk, wv_t, wf_t, bf_col, pq_t, pk, *, tm=ROW_TILE):
    b, s, d = x.shape
    const = lambda bi, i: (0, 0)
    return pl.pallas_call(
        functools.partial(_attn_in_kernel, tm=tm),
        grid=(b, s // tm),
        in_specs=[
            pl.BlockSpec((None, tm, d), lambda bi, i: (bi, i, 0)),
            pl.BlockSpec((1, d), const),
            pl.BlockSpec(wq_t.shape, const),
            pl.BlockSpec(wk.shape, const),
            pl.BlockSpec(wv_t.shape, const),
            pl.BlockSpec(wf_t.shape, const),
            pl.BlockSpec(bf_col.shape, const),
            pl.BlockSpec(pq_t.shape, const),
            pl.BlockSpec(pk.shape, const),
        ],
        out_specs=[
            pl.BlockSpec((None, N_HEADS, HEAD_PAD, tm), lambda bi, i: (bi, 0, 0, i)),
            pl.BlockSpec((None, N_HEADS, tm, HEAD_PAD), lambda bi, i: (bi, 0, i, 0)),
            pl.BlockSpec((None, N_HEADS, None, V_ROWS, tm), lambda bi, i: (bi, 0, i, 0, 0)),
        ],
        out_shape=[
            jax.ShapeDtypeStruct((b, N_HEADS, HEAD_PAD, s), BF16),
            jax.ShapeDtypeStruct((b, N_HEADS, s, HEAD_PAD), BF16),
            jax.ShapeDtypeStruct((b, N_HEADS, s // tm, V_ROWS, tm), BF16),
        ],
        scratch_shapes=[pltpu.VMEM((HEAD_PAD, 1), F32)],
        compiler_params=pltpu.CompilerParams(
            dimension_semantics=("arbitrary", "arbitrary"),
            vmem_limit_bytes=VMEM_LIMIT_BYTES),
        name="attn_in",
    )(x, g, wq_t, wk, wv_t, wf_t, bf_col, pq_t, pk)


def _flash_kernel(q_ref, k_ref, v_ref, o_ref, sa_ref, sb_ref, m_ref, acc_ref, *, bk, hps):
    i = pl.program_id(2)
    bq = 2 * bk
    m_ref[...] = jnp.full_like(m_ref, -jnp.inf)
    acc_ref[...] = jnp.zeros_like(acc_ref)

    def scores(j, s_ref):
        start = pl.multiple_of(j * bk, bk)
        for hh in range(hps):
            s_ref[hh] = jnp.dot(k_ref[hh, pl.ds(start, bk), :], q_ref[hh],
                                preferred_element_type=F32)

    def consume(j, s_ref, key_offset):
        for hh in range(hps):
            s = s_ref[hh]
            if key_offset is not None:
                key = lax.broadcasted_iota(jnp.int32, (bk, bq), 0) + key_offset
                qry = lax.broadcasted_iota(jnp.int32, (bk, bq), 1)
                s = jnp.where(key <= qry, s, NEG_INF)
            m_prev = m_ref[hh]
            m_new = jnp.maximum(m_prev, jnp.max(s, axis=0, keepdims=True))
            alpha = jnp.exp2(m_prev - m_new)
            p = jnp.exp2(s - m_new).astype(BF16)
            acc_ref[hh] = alpha * acc_ref[hh] + jnp.dot(
                v_ref[hh, j], p, preferred_element_type=F32)
            m_ref[hh] = m_new

    scores(0, sa_ref)

    def body(t, carry):
        j = 2 * t
        scores(j + 1, sb_ref)
        consume(j, sa_ref, None)
        scores(j + 2, sa_ref)
        consume(j + 1, sb_ref, None)
        return carry

    lax.fori_loop(0, i, body, 0)
    j = 2 * i
    scores(j + 1, sb_ref)
    consume(j, sa_ref, 0)
    consume(j + 1, sb_ref, bk)

    for hh in range(hps):
        acc = acc_ref[hh]
        o_ref[hh] = (acc[0:HEAD_DIM] / acc[V_ONES_ROW:V_ONES_ROW + 1]).astype(o_ref.dtype)


def _flash(q_t, k, v_t, *, bk=ATTN_BLOCK, hps=HEADS_PER_STEP):
    b, nh, s, hp = k.shape
    bq = 2 * bk
    assert v_t.shape == (b, nh, s // bk, V_ROWS, bk)
    score_buf = pltpu.VMEM((hps, bk, bq), F32)
    return pl.pallas_call(
        functools.partial(_flash_kernel, bk=bk, hps=hps),
        grid=(b, nh // hps, s // bq),
        in_specs=[
            pl.BlockSpec((None, hps, hp, bq), lambda bi, h, i: (bi, h, 0, i)),
            pl.BlockSpec((None, hps, s, hp), lambda bi, h, i: (bi, h, 0, 0)),
            pl.BlockSpec((None, hps, s // bk, V_ROWS, bk), lambda bi, h, i: (bi, h, 0, 0, 0)),
        ],
        out_specs=pl.BlockSpec((None, hps, HEAD_DIM, bq), lambda bi, h, i: (bi, h, 0, i)),
        out_shape=jax.ShapeDtypeStruct((b, nh, HEAD_DIM, s), BF16),
        scratch_shapes=[score_buf, score_buf,
                        pltpu.VMEM((hps, 1, bq), F32), pltpu.VMEM((hps, V_ROWS, bq), F32)],
        compiler_params=pltpu.CompilerParams(
            dimension_semantics=("arbitrary", "arbitrary", "arbitrary"),
            vmem_limit_bytes=VMEM_LIMIT_BYTES),
        name="fox_flash",
    )(q_t, k, v_t)


def _attn_out_kernel(o_ref, x_ref, w_ref, g_ref, y_ref):
    o_t = jnp.concatenate([o_ref[hh] for hh in range(N_HEADS)], axis=0)
    m_t = jnp.dot(w_ref[...], o_t, preferred_element_type=F32)
    y_ref[...] = x_ref[...] + _rms(m_t.T, g_ref[...])


def _attn_out(o_t, x, w_out_t, g, *, tm=ROW_TILE):
    b, s, d = x.shape
    const = lambda bi, i: (0, 0)
    return pl.pallas_call(
        _attn_out_kernel,
        grid=(b, s // tm),
        in_specs=[
            pl.BlockSpec((None, N_HEADS, HEAD_DIM, tm), lambda bi, i: (bi, 0, 0, i)),
            pl.BlockSpec((None, tm, d), lambda bi, i: (bi, i, 0)),
            pl.BlockSpec(w_out_t.shape, const),
            pl.BlockSpec((1, d), const),
        ],
        out_specs=pl.BlockSpec((None, tm, d), lambda bi, i: (bi, i, 0)),
        out_shape=jax.ShapeDtypeStruct(x.shape, F32),
        compiler_params=pltpu.CompilerParams(
            dimension_semantics=("arbitrary", "arbitrary"),
            vmem_limit_bytes=VMEM_LIMIT_BYTES),
        name="attn_out",
    )(o_t, x, w_out_t, g)


HALO = 8


def _conv_kernel(x_ref, g0_ref, win_ref, cw_ref, wout_ref, g1_ref, y_ref, z_ref, *, tm):
    d = D_MODEL

    @pl.when(pl.program_id(1) == 0)
    def _():
        z_ref[0:HALO, :] = jnp.zeros((HALO, d), F32)

    @pl.when(pl.program_id(1) > 0)
    def _():
        z_ref[0:HALO, :] = z_ref[tm:tm + HALO, :]

    x = x_ref[...]
    h = _rms(x, g0_ref[...]).astype(BF16)
    c_gate = jnp.dot(h, win_ref[:, d:2 * d], preferred_element_type=F32)
    u = jnp.dot(h, win_ref[:, 2 * d:3 * d], preferred_element_type=F32)
    z = c_gate * u
    z_ref[HALO:HALO + tm, :] = z
    cw = cw_ref[...]
    zc = (cw[2:3, :] * z
          + cw[1:2, :] * z_ref[HALO - 1:HALO - 1 + tm, :]
          + cw[0:1, :] * z_ref[HALO - 2:HALO - 2 + tm, :])
    b_gate = jnp.dot(h, win_ref[:, 0:d], preferred_element_type=F32)
    y = (b_gate * zc).astype(BF16)
    m = jnp.dot(y, wout_ref[...], preferred_element_type=F32)
    y_ref[...] = x + _rms(m, g1_ref[...])


def _conv_layer(x, g0, w_in, conv_w, w_out, g1, *, tm=ROW_TILE):
    b, s, d = x.shape
    const = lambda bi, i: (0, 0)
    return pl.pallas_call(
        functools.partial(_conv_kernel, tm=tm),
        grid=(b, s // tm),
        in_specs=[
            pl.BlockSpec((None, tm, d), lambda bi, i: (bi, i, 0)),
            pl.BlockSpec((1, d), const),
            pl.BlockSpec(w_in.shape, const),
            pl.BlockSpec(conv_w.shape, const),
            pl.BlockSpec(w_out.shape, const),
            pl.BlockSpec((1, d), const),
        ],
        out_specs=pl.BlockSpec((None, tm, d), lambda bi, i: (bi, i, 0)),
        out_shape=jax.ShapeDtypeStruct(x.shape, F32),
        scratch_shapes=[pltpu.VMEM((tm + HALO, d), F32)],
        compiler_params=pltpu.CompilerParams(
            dimension_semantics=("arbitrary", "arbitrary"),
            vmem_limit_bytes=VMEM_LIMIT_BYTES),
        name="conv_mixer",
    )(x, g0, w_in, conv_w, w_out, g1)


FF_CHUNK = 1024


def _mlp_kernel(x_ref, g2_ref, wup_ref, wdn_ref, g3_ref, y_ref):
    x = x_ref[...]
    h = _rms(x, g2_ref[...]).astype(BF16)
    d_ff = wup_ref.shape[1]
    f = None
    for c in range(d_ff // FF_CHUNK):
        cols = slice(c * FF_CHUNK, (c + 1) * FF_CHUNK)
        u = jnp.dot(h, wup_ref[:, cols], preferred_element_type=F32)
        a = jnp.square(jnp.maximum(u, 0.0)).astype(BF16)
        part = jnp.dot(a, wdn_ref[cols, :], preferred_element_type=F32)
        f = part if f is None else f + part
    y_ref[...] = x + _rms(f, g3_ref[...])


def _mlp(x, g2, w_up, w_down, g3, *, tm=ROW_TILE):
    b, s, d = x.shape
    const = lambda bi, i: (0, 0)
    return pl.pallas_call(
        _mlp_kernel,
        grid=(b, s // tm),
        in_specs=[
            pl.BlockSpec((None, tm, d), lambda bi, i: (bi, i, 0)),
            pl.BlockSpec((1, d), const),
            pl.BlockSpec(w_up.shape, const),
            pl.BlockSpec(w_down.shape, const),
            pl.BlockSpec((1, d), const),
        ],
        out_specs=pl.BlockSpec((None, tm, d), lambda bi, i: (bi, i, 0)),
        out_shape=jax.ShapeDtypeStruct(x.shape, F32),
        compiler_params=pltpu.CompilerParams(
            dimension_semantics=("arbitrary", "arbitrary"),
            vmem_limit_bytes=VMEM_LIMIT_BYTES),
        name="sq_relu_mlp",
    )(x, g2, w_up, w_down, g3)


def _ple_kernel(x_ref, p_ref, g4_ref, wg_ref, wp_ref, g5_ref, y_ref):
    x = x_ref[...]
    h = _rms(x, g4_ref[...]).astype(BF16)
    gate = jax.nn.sigmoid(jnp.dot(h, wg_ref[...], preferred_element_type=F32))
    e = jnp.dot(p_ref[...].astype(BF16), wp_ref[...], preferred_element_type=F32) * gate
    y_ref[...] = x + _rms(e, g5_ref[...])


def _ple(x, p, layer, g4, w_gate, w_proj, g5, *, tm=ROW_TILE):
    b, s, d = x.shape
    const = lambda bi, i: (0, 0)
    return pl.pallas_call(
        _ple_kernel,
        grid=(b, s // tm),
        in_specs=[
            pl.BlockSpec((None, tm, d), lambda bi, i: (bi, i, 0)),
            pl.BlockSpec((None, None, tm, p.shape[-1]), lambda bi, i: (layer, bi, i, 0)),
            pl.BlockSpec((1, d), const),
            pl.BlockSpec(w_gate.shape, const),
            pl.BlockSpec(w_proj.shape, const),
            pl.BlockSpec((1, d), const),
        ],
        out_specs=pl.BlockSpec((None, tm, d), lambda bi, i: (bi, i, 0)),
        out_shape=jax.ShapeDtypeStruct(x.shape, F32),
        compiler_params=pltpu.CompilerParams(
            dimension_semantics=("arbitrary", "arbitrary"),
            vmem_limit_bytes=VMEM_LIMIT_BYTES),
        name="ple_gate",
    )(x, p, g4, w_gate, w_proj, g5)


def _pad_heads(w, width):
    d = w.shape[0]
    w = w.reshape(d, N_HEADS, HEAD_DIM)
    w = jnp.pad(w, ((0, 0), (0, 0), (0, width - HEAD_DIM)))
    return w.reshape(d, N_HEADS * width)


def _prep_attn_weights(w_in, b_f, w_out):
    d = D_MODEL
    wq_t = _pad_heads(w_in[:, 0:d], HEAD_PAD).T.astype(BF16)
    wk = _pad_heads(w_in[:, d:2 * d], HEAD_PAD).astype(BF16)
    wv_t = _pad_heads(w_in[:, 2 * d:3 * d], V_ROWS).T.astype(BF16)
    wf = w_in[:, 3 * d:]
    wf_t = jnp.pad(jnp.concatenate([wf, wf, wf], axis=1),
                   ((0, 0), (0, HEAD_PAD - 3 * N_HEADS))).T.astype(BF16)
    bf_col = jnp.pad(jnp.concatenate([b_f, b_f, b_f]), (0, HEAD_PAD - 3 * N_HEADS))
    bf_col = bf_col.reshape(HEAD_PAD, 1).astype(F32)
    return wq_t, wk, wv_t, wf_t, bf_col, w_out.T.astype(BF16)


def kernel(x, p, norm_g, w_attn_in, b_forget, w_attn_out, w_conv_in, conv_w, w_conv_out,
           w_mlp_up, w_mlp_down, w_ple_proj, w_ple_gate):
    depth = norm_g.shape[0]
    pq_np, pk_np = _placement_matrices()
    pq_t = jnp.asarray(pq_np, dtype=BF16)
    pk = jnp.asarray(pk_np, dtype=BF16)
    for i in range(depth):
        g = norm_g[i].astype(F32)
        gi = lambda n: g[n:n + 1]
        j = i // 2
        if i % 2 == 0:
            wq_t, wk, wv_t, wf_t, bf_col, w_out_t = _prep_attn_weights(
                w_attn_in[j], b_forget[j], w_attn_out[j])
            q_t, k, v_t = _attn_in(x, gi(0), wq_t, wk, wv_t, wf_t, bf_col, pq_t, pk)
            o_t = _flash(q_t, k, v_t)
            x = _attn_out(o_t, x, w_out_t, gi(1))
        else:
            x = _conv_layer(x, gi(0), w_conv_in[j].astype(BF16), conv_w[j].astype(F32),
                            w_conv_out[j].astype(BF16), gi(1))
        x = _mlp(x, gi(2), w_mlp_up[i].astype(BF16), w_mlp_down[i].astype(BF16), gi(3))
        x = _ple(x, p, i, gi(4), w_ple_gate[i].astype(BF16), w_ple_proj[i].astype(BF16),
                 gi(5))
    return x
```

```python
import functools

import numpy as np
import jax
import jax.numpy as jnp
from jax import lax
from jax.experimental import pallas as pl
from jax.experimental.pallas import tpu as pltpu

F32 = jnp.float32
BF16 = jnp.bfloat16

D_MODEL = 1024
N_HEADS = 16
HEAD_DIM = D_MODEL // N_HEADS
HEAD_PAD = 128
QK_PAD = N_HEADS * HEAD_PAD
V_ROWS = 128
V_ONES_ROW = HEAD_DIM
CONV_WIDTH = 3
RMS_EPS = 1e-6
NEG_INF = -1e30
LOG2E = 1.4426950408889634
Q_SCALE = (HEAD_DIM ** -0.5) * LOG2E

VMEM_LIMIT_BYTES = 56 * 1024 * 1024

ROW_TILE = 512
ATTN_BLOCK = ROW_TILE
GROUP = 4

SLOT_CQ = (64, 65, 66)
SLOT_CK = (67, 68, 69)
ROW_ONES = 48


def _rms(xf, g):
    ms = jnp.mean(xf * xf, axis=-1, keepdims=True)
    return xf * lax.rsqrt(ms + RMS_EPS) * g


def _split3(x):
    hi = x.astype(BF16).astype(F32)
    r = x - hi
    mid = r.astype(BF16).astype(F32)
    lo = (r - mid).astype(BF16).astype(F32)
    return hi, mid, lo


def _placement_matrices():
    pq = np.zeros((HEAD_PAD, QK_PAD), np.float32)
    pk = np.zeros((HEAD_PAD, QK_PAD), np.float32)
    for h in range(N_HEADS):
        base = h * HEAD_PAD
        for part in range(3):
            pq[part * N_HEADS + h, base + SLOT_CQ[part]] = 1.0
            pk[part * N_HEADS + h, base + SLOT_CK[part]] = -1.0
            pq[ROW_ONES, base + SLOT_CK[part]] = 1.0
            pk[ROW_ONES, base + SLOT_CQ[part]] = 1.0
    return pq.T.copy(), pk


def _attn_in_kernel(x_ref, g_ref, wq_ref, wk_ref, wv_ref, wf_ref, bf_ref, pq_ref, pk_ref,
                    q_ref, k_ref, v_ref, carry_ref, *, tm):
    @pl.when(pl.program_id(1) == 0)
    def _():
        carry_ref[...] = jnp.zeros_like(carry_ref)

    hn = _rms(x_ref[...], g_ref[...])
    h = hn.astype(BF16)
    ht = hn.T.astype(BF16)

    f = jnp.dot(wf_ref[...], ht, preferred_element_type=F32) + bf_ref[...]
    logf = (jnp.minimum(f, 0.0) - jnp.log1p(jnp.exp(-jnp.abs(f)))) * LOG2E

    row = lax.broadcasted_iota(jnp.int32, (tm, tm), 0)
    col = lax.broadcasted_iota(jnp.int32, (tm, tm), 1)
    triu = (row <= col).astype(BF16)
    c = carry_ref[...]
    for part in _split3(logf):
        c = c + jnp.dot(part.astype(BF16), triu, preferred_element_type=F32)
    carry_ref[...] = c[:, tm - 1:tm]

    chi, cmid, clo = _split3(c)
    sub = lax.broadcasted_iota(jnp.int32, (HEAD_PAD, tm), 0)
    bias_t = jnp.where(sub < N_HEADS, chi,
                       jnp.where(sub < 2 * N_HEADS, cmid,
                                 jnp.where(sub < 3 * N_HEADS, clo,
                                           jnp.where(sub == ROW_ONES, 1.0, 0.0))))
    bias = bias_t.T.astype(BF16)
    bias_t = bias_t.astype(BF16)

    kk = (jnp.dot(h, wk_ref[...], preferred_element_type=F32)
          + jnp.dot(bias, pk_ref[...], preferred_element_type=F32))
    for hh in range(N_HEADS):
        k_ref[hh] = kk[:, hh * HEAD_PAD:(hh + 1) * HEAD_PAD].astype(BF16)

    qt = (jnp.dot(wq_ref[...], ht, preferred_element_type=F32) * Q_SCALE
          + jnp.dot(pq_ref[...], bias_t, preferred_element_type=F32))
    for hh in range(N_HEADS):
        q_ref[hh] = qt[hh * HEAD_PAD:(hh + 1) * HEAD_PAD, :].astype(BF16)

    vt = jnp.dot(wv_ref[...], ht, preferred_element_type=F32)
    ones_row = lax.broadcasted_iota(jnp.int32, (V_ROWS, tm), 0) == V_ONES_ROW
    for hh in range(N_HEADS):
        v_ref[hh] = jnp.where(ones_row, 1.0, vt[hh * V_ROWS:(hh + 1) * V_ROWS, :]).astype(BF16)


def _attn_in(x, g, wq_t, wk, wv_t, wf_t, bf_col, pq_t, pk, *, tm=ROW_TILE):
    b, s, d = x.shape
    const = lambda bi, i: (0, 0)
    return pl.pallas_call(
        functools.partial(_attn_in_kernel, tm=tm),
        grid=(b, s // tm),
        in_specs=[
            pl.BlockSpec((None, tm, d), lambda bi, i: (bi, i, 0)),
            pl.BlockSpec((1, d), const),
            pl.BlockSpec(wq_t.shape, const),
            pl.BlockSpec(wk.shape, const),
            pl.BlockSpec(wv_t.shape, const),
            pl.BlockSpec(wf_t.shape, const),
            pl.BlockSpec(bf_col.shape, const),
            pl.BlockSpec(pq_t.shape, const),
            pl.BlockSpec(pk.shape, const),
        ],
        out_specs=[
            pl.BlockSpec((None, N_HEADS, HEAD_PAD, tm), lambda bi, i: (bi, 0, 0, i)),
            pl.BlockSpec((None, N_HEADS, tm, HEAD_PAD), lambda bi, i: (bi, 0, i, 0)),
            pl.BlockSpec((None, N_HEADS, None, V_ROWS, tm), lambda bi, i: (bi, 0, i, 0, 0)),
        ],
        out_shape=[
            jax.ShapeDtypeStruct((b, N_HEADS, HEAD_PAD, s), BF16),
            jax.ShapeDtypeStruct((b, N_HEADS, s, HEAD_PAD), BF16),
            jax.ShapeDtypeStruct((b, N_HEADS, s // tm, V_ROWS, tm), BF16),
        ],
        scratch_shapes=[pltpu.VMEM((HEAD_PAD, 1), F32)],
        compiler_params=pltpu.CompilerParams(
            dimension_semantics=("arbitrary", "arbitrary"),
            vmem_limit_bytes=VMEM_LIMIT_BYTES),
        name="attn_in",
    )(x, g, wq_t, wk, wv_t, wf_t, bf_col, pq_t, pk)


def _flash_kernel(q_ref, k_ref, v_ref, o_ref, s_refs, smax_refs, m_ref, acc_ref, *, bk):
    i = pl.program_id(2)
    bq = GROUP * bk
    m_ref[...] = jnp.full_like(m_ref, -jnp.inf)
    acc_ref[...] = jnp.zeros_like(acc_ref)

    def scores(j, slot, c0=0):
        start = pl.multiple_of(j * bk, bk)
        s = jnp.dot(k_ref[pl.ds(start, bk), :], q_ref[:, c0:],
                    preferred_element_type=F32)
        s_refs[slot][:, c0:] = s
        smax_refs[slot][:, c0:] = jnp.max(s, axis=0, keepdims=True)

    def consume(j, slot, diag_block=None):
        c0 = 0 if diag_block is None else diag_block * bk
        s = s_refs[slot][:, c0:]
        if diag_block is None:
            s_max = smax_refs[slot][...]
        else:
            key = lax.broadcasted_iota(jnp.int32, (bk, bq - c0), 0)
            qry = lax.broadcasted_iota(jnp.int32, (bk, bq - c0), 1)
            s = jnp.where(key <= qry, s, NEG_INF)
            s_max = jnp.max(s, axis=0, keepdims=True)
        m_prev = m_ref[:, c0:]
        m_new = jnp.maximum(m_prev, s_max)
        alpha = jnp.exp2(m_prev - m_new)
        p = jnp.exp2(s - m_new).astype(BF16)
        acc_ref[:, c0:] = alpha * acc_ref[:, c0:] + jnp.dot(
            v_ref[j], p, preferred_element_type=F32)
        m_ref[:, c0:] = m_new

    half = GROUP // 2
    for r in range(half):
        scores(r, r)

    def body(g, carry):
        j = GROUP * g
        for r in range(half):
            scores(j + half + r, half + r)
        for r in range(half):
            consume(j + r, r)
        for r in range(half):
            scores(j + GROUP + r, r)
        for r in range(half):
            consume(j + half + r, half + r)
        return carry

    lax.fori_loop(0, i, body, 0)
    j = GROUP * i
    for r in range(half, GROUP):
        scores(j + r, r, c0=r * bk)
    for r in range(GROUP):
        consume(j + r, r, diag_block=r)

    acc = acc_ref[...]
    o_ref[...] = (acc[0:HEAD_DIM] / acc[V_ONES_ROW:V_ONES_ROW + 1]).astype(o_ref.dtype)


def _flash(q_t, k, v_t, *, bk=ATTN_BLOCK):
    b, nh, s, hp = k.shape
    bq = GROUP * bk
    assert v_t.shape == (b, nh, s // bk, V_ROWS, bk)
    score_bufs = [pltpu.VMEM((bk, bq), F32) for _ in range(GROUP)]
    smax_bufs = [pltpu.VMEM((1, bq), F32) for _ in range(GROUP)]
    return pl.pallas_call(
        functools.partial(_flash_kernel, bk=bk),
        grid=(b, nh, s // bq),
        in_specs=[
            pl.BlockSpec((None, None, hp, bq), lambda bi, h, i: (bi, h, 0, i)),
            pl.BlockSpec((None, None, s, hp), lambda bi, h, i: (bi, h, 0, 0)),
            pl.BlockSpec((None, None, s // bk, V_ROWS, bk), lambda bi, h, i: (bi, h, 0, 0, 0)),
        ],
        out_specs=pl.BlockSpec((None, None, HEAD_DIM, bq), lambda bi, h, i: (bi, h, 0, i)),
        out_shape=jax.ShapeDtypeStruct((b, nh, HEAD_DIM, s), BF16),
        scratch_shapes=[score_bufs, smax_bufs,
                        pltpu.VMEM((1, bq), F32), pltpu.VMEM((V_ROWS, bq), F32)],
        compiler_params=pltpu.CompilerParams(
            dimension_semantics=("arbitrary", "arbitrary", "arbitrary"),
            vmem_limit_bytes=VMEM_LIMIT_BYTES),
        name="fox_flash",
    )(q_t, k, v_t)


def _attn_out_kernel(o_ref, x_ref, w_ref, g_ref, y_ref):
    o_t = jnp.concatenate([o_ref[hh] for hh in range(N_HEADS)], axis=0)
    m_t = jnp.dot(w_ref[...], o_t, preferred_element_type=F32)
    y_ref[...] = x_ref[...] + _rms(m_t.T, g_ref[...])


def _attn_out(o_t, x, w_out_t, g, *, tm=ROW_TILE):
    b, s, d = x.shape
    const = lambda bi, i: (0, 0)
    return pl.pallas_call(
        _attn_out_kernel,
        grid=(b, s // tm),
        in_specs=[
            pl.BlockSpec((None, N_HEADS, HEAD_DIM, tm), lambda bi, i: (bi, 0, 0, i)),
            pl.BlockSpec((None, tm, d), lambda bi, i: (bi, i, 0)),
            pl.BlockSpec(w_out_t.shape, const),
            pl.BlockSpec((1, d), const),
        ],
        out_specs=pl.BlockSpec((None, tm, d), lambda bi, i: (bi, i, 0)),
        out_shape=jax.ShapeDtypeStruct(x.shape, F32),
        compiler_params=pltpu.CompilerParams(
            dimension_semantics=("arbitrary", "arbitrary"),
            vmem_limit_bytes=VMEM_LIMIT_BYTES),
        name="attn_out",
    )(o_t, x, w_out_t, g)


HALO = 8


def _conv_kernel(x_ref, g0_ref, win_ref, cw_ref, wout_ref, g1_ref, y_ref, z_ref, *, tm):
    d = D_MODEL

    @pl.when(pl.program_id(1) == 0)
    def _():
        z_ref[0:HALO, :] = jnp.zeros((HALO, d), F32)

    @pl.when(pl.program_id(1) > 0)
    def _():
        z_ref[0:HALO, :] = z_ref[tm:tm + HALO, :]

    x = x_ref[...]
    h = _rms(x, g0_ref[...]).astype(BF16)
    c_gate = jnp.dot(h, win_ref[:, d:2 * d], preferred_element_type=F32)
    u = jnp.dot(h, win_ref[:, 2 * d:3 * d], preferred_element_type=F32)
    z = c_gate * u
    z_ref[HALO:HALO + tm, :] = z
    cw = cw_ref[...]
    zc = (cw[2:3, :] * z
          + cw[1:2, :] * z_ref[HALO - 1:HALO - 1 + tm, :]
          + cw[0:1, :] * z_ref[HALO - 2:HALO - 2 + tm, :])
    b_gate = jnp.dot(h, win_ref[:, 0:d], preferred_element_type=F32)
    y = (b_gate * zc).astype(BF16)
    m = jnp.dot(y, wout_ref[...], preferred_element_type=F32)
    y_ref[...] = x + _rms(m, g1_ref[...])


def _conv_layer(x, g0, w_in, conv_w, w_out, g1, *, tm=ROW_TILE):
    b, s, d = x.shape
    const = lambda bi, i: (0, 0)
    return pl.pallas_call(
        functools.partial(_conv_kernel, tm=tm),
        grid=(b, s // tm),
        in_specs=[
            pl.BlockSpec((None, tm, d), lambda bi, i: (bi, i, 0)),
            pl.BlockSpec((1, d), const),
            pl.BlockSpec(w_in.shape, const),
            pl.BlockSpec(conv_w.shape, const),
            pl.BlockSpec(w_out.shape, const),
            pl.BlockSpec((1, d), const),
        ],
        out_specs=pl.BlockSpec((None, tm, d), lambda bi, i: (bi, i, 0)),
        out_shape=jax.ShapeDtypeStruct(x.shape, F32),
        scratch_shapes=[pltpu.VMEM((tm + HALO, d), F32)],
        compiler_params=pltpu.CompilerParams(
            dimension_semantics=("arbitrary", "arbitrary"),
            vmem_limit_bytes=VMEM_LIMIT_BYTES),
        name="conv_mixer",
    )(x, g0, w_in, conv_w, w_out, g1)


FF_CHUNK = 1024


def _mlp_kernel(x_ref, g2_ref, wup_ref, wdn_ref, g3_ref, y_ref):
    x = x_ref[...]
    h = _rms(x, g2_ref[...]).astype(BF16)
    d_ff = wup_ref.shape[1]
    f = None
    for c in range(d_ff // FF_CHUNK):
        cols = slice(c * FF_CHUNK, (c + 1) * FF_CHUNK)
        u = jnp.dot(h, wup_ref[:, cols], preferred_element_type=F32)
        a = jnp.square(jnp.maximum(u, 0.0)).astype(BF16)
        part = jnp.dot(a, wdn_ref[cols, :], preferred_element_type=F32)
        f = part if f is None else f + part
    y_ref[...] = x + _rms(f, g3_ref[...])


def _mlp(x, g2, w_up, w_down, g3, *, tm=ROW_TILE):
    b, s, d = x.shape
    const = lambda bi, i: (0, 0)
    return pl.pallas_call(
        _mlp_kernel,
        grid=(b, s // tm),
        in_specs=[
            pl.BlockSpec((None, tm, d), lambda bi, i: (bi, i, 0)),
            pl.BlockSpec((1, d), const),
            pl.BlockSpec(w_up.shape, const),
            pl.BlockSpec(w_down.shape, const),
            pl.BlockSpec((1, d), const),
        ],
        out_specs=pl.BlockSpec((None, tm, d), lambda bi, i: (bi, i, 0)),
        out_shape=jax.ShapeDtypeStruct(x.shape, F32),
        compiler_params=pltpu.CompilerParams(
            dimension_semantics=("arbitrary", "arbitrary"),
            vmem_limit_bytes=VMEM_LIMIT_BYTES),
        name="sq_relu_mlp",
    )(x, g2, w_up, w_down, g3)


def _ple_kernel(x_ref, p_ref, g4_ref, wg_ref, wp_ref, g5_ref, y_ref):
    x = x_ref[...]
    h = _rms(x, g4_ref[...]).astype(BF16)
    gate = jax.nn.sigmoid(jnp.dot(h, wg_ref[...], preferred_element_type=F32))
    e = jnp.dot(p_ref[...].astype(BF16), wp_ref[...], preferred_element_type=F32) * gate
    y_ref[...] = x + _rms(e, g5_ref[...])


def _ple(x, p, layer, g4, w_gate, w_proj, g5, *, tm=ROW_TILE):
    b, s, d = x.shape
    const = lambda bi, i: (0, 0)
    return pl.pallas_call(
        _ple_kernel,
        grid=(b, s // tm),
        in_specs=[
            pl.BlockSpec((None, tm, d), lambda bi, i: (bi, i, 0)),
            pl.BlockSpec((None, None, tm, p.shape[-1]), lambda bi, i: (layer, bi, i, 0)),
            pl.BlockSpec((1, d), const),
            pl.BlockSpec(w_gate.shape, const),
            pl.BlockSpec(w_proj.shape, const),
            pl.BlockSpec((1, d), const),
        ],
        out_specs=pl.BlockSpec((None, tm, d), lambda bi, i: (bi, i, 0)),
        out_shape=jax.ShapeDtypeStruct(x.shape, F32),
        compiler_params=pltpu.CompilerParams(
            dimension_semantics=("arbitrary", "arbitrary"),
            vmem_limit_bytes=VMEM_LIMIT_BYTES),
        name="ple_gate",
    )(x, p, g4, w_gate, w_proj, g5)


def _pad_heads(w, width):
    d = w.shape[0]
    w = w.reshape(d, N_HEADS, HEAD_DIM)
    w = jnp.pad(w, ((0, 0), (0, 0), (0, width - HEAD_DIM)))
    return w.reshape(d, N_HEADS * width)


def _prep_attn_weights(w_in, b_f, w_out):
    d = D_MODEL
    wq_t = _pad_heads(w_in[:, 0:d], HEAD_PAD).T.astype(BF16)
    wk = _pad_heads(w_in[:, d:2 * d], HEAD_PAD).astype(BF16)
    wv_t = _pad_heads(w_in[:, 2 * d:3 * d], V_ROWS).T.astype(BF16)
    wf = w_in[:, 3 * d:]
    wf_t = jnp.pad(jnp.concatenate([wf, wf, wf], axis=1),
                   ((0, 0), (0, HEAD_PAD - 3 * N_HEADS))).T.astype(BF16)
    bf_col = jnp.pad(jnp.concatenate([b_f, b_f, b_f]), (0, HEAD_PAD - 3 * N_HEADS))
    bf_col = bf_col.reshape(HEAD_PAD, 1).astype(F32)
    return wq_t, wk, wv_t, wf_t, bf_col, w_out.T.astype(BF16)


def kernel(x, p, norm_g, w_attn_in, b_forget, w_attn_out, w_conv_in, conv_w, w_conv_out,
           w_mlp_up, w_mlp_down, w_ple_proj, w_ple_gate):
    depth = norm_g.shape[0]
    pq_np, pk_np = _placement_matrices()
    pq_t = jnp.asarray(pq_np, dtype=BF16)
    pk = jnp.asarray(pk_np, dtype=BF16)
    for i in range(depth):
        g = norm_g[i].astype(F32)
        gi = lambda n: g[n:n + 1]
        j = i // 2
        if i % 2 == 0:
            wq_t, wk, wv_t, wf_t, bf_col, w_out_t = _prep_attn_weights(
                w_attn_in[j], b_forget[j], w_attn_out[j])
            q_t, k, v_t = _attn_in(x, gi(0), wq_t, wk, wv_t, wf_t, bf_col, pq_t, pk)
            o_t = _flash(q_t, k, v_t)
            x = _attn_out(o_t, x, w_out_t, gi(1))
        else:
            x = _conv_layer(x, gi(0), w_conv_in[j].astype(BF16), conv_w[j].astype(F32),
                            w_conv_out[j].astype(BF16), gi(1))
        x = _mlp(x, gi(2), w_mlp_up[i].astype(BF16), w_mlp_down[i].astype(BF16), gi(3))
        x = _ple(x, p, i, gi(4), w_ple_gate[i].astype(BF16), w_ple_proj[i].astype(BF16),
                 gi(5))
    return x
```

```python
import functools

import numpy as np
import jax
import jax.numpy as jnp
from jax import lax
from jax.experimental import pallas as pl
from jax.experimental.pallas import tpu as pltpu

F32 = jnp.float32
BF16 = jnp.bfloat16

D_MODEL = 1024
N_HEADS = 16
HEAD_DIM = D_MODEL // N_HEADS
HEAD_PAD = 128
QK_PAD = N_HEADS * HEAD_PAD
V_ROWS = 128
V_ONES_ROW = HEAD_DIM
CONV_WIDTH = 3
RMS_EPS = 1e-6
NEG_INF = -1e30
LOG2E = 1.4426950408889634
Q_SCALE = (HEAD_DIM ** -0.5) * LOG2E

VMEM_LIMIT_BYTES = 56 * 1024 * 1024

ROW_TILE = 512
V_BLOCK = ROW_TILE
ATTN_BLOCK = 1024
QUERY_TILE = 256

SLOT_CQ = (64, 65, 66)
SLOT_CK = (67, 68, 69)
ROW_ONES = 48


def _rms(xf, g):
    ms = jnp.mean(xf * xf, axis=-1, keepdims=True)
    return xf * lax.rsqrt(ms + RMS_EPS) * g


def _split3(x):
    hi = x.astype(BF16).astype(F32)
    r = x - hi
    mid = r.astype(BF16).astype(F32)
    lo = (r - mid).astype(BF16).astype(F32)
    return hi, mid, lo


def _placement_matrices():
    pq = np.zeros((HEAD_PAD, QK_PAD), np.float32)
    pk = np.zeros((HEAD_PAD, QK_PAD), np.float32)
    for h in range(N_HEADS):
        base = h * HEAD_PAD
        for part in range(3):
            pq[part * N_HEADS + h, base + SLOT_CQ[part]] = 1.0
            pk[part * N_HEADS + h, base + SLOT_CK[part]] = -1.0
            pq[ROW_ONES, base + SLOT_CK[part]] = 1.0
            pk[ROW_ONES, base + SLOT_CQ[part]] = 1.0
    return pq.T.copy(), pk


def _attn_in_kernel(x_ref, g_ref, wq_ref, wk_ref, wv_ref, wf_ref, bf_ref, pq_ref, pk_ref,
                    q_ref, k_ref, v_ref, carry_ref, *, tm):
    @pl.when(pl.program_id(1) == 0)
    def _():
        carry_ref[...] = jnp.zeros_like(carry_ref)

    hn = _rms(x_ref[...], g_ref[...])
    h = hn.astype(BF16)
    ht = hn.T.astype(BF16)

    f = jnp.dot(wf_ref[...], ht, preferred_element_type=F32) + bf_ref[...]
    logf = (jnp.minimum(f, 0.0) - jnp.log1p(jnp.exp(-jnp.abs(f)))) * LOG2E

    row = lax.broadcasted_iota(jnp.int32, (tm, tm), 0)
    col = lax.broadcasted_iota(jnp.int32, (tm, tm), 1)
    triu = (row <= col).astype(BF16)
    c = carry_ref[...]
    for part in _split3(logf):
        c = c + jnp.dot(part.astype(BF16), triu, preferred_element_type=F32)
    carry_ref[...] = c[:, tm - 1:tm]

    chi, cmid, clo = _split3(c)
    sub = lax.broadcasted_iota(jnp.int32, (HEAD_PAD, tm), 0)
    bias_t = jnp.where(sub < N_HEADS, chi,
                       jnp.where(sub < 2 * N_HEADS, cmid,
                                 jnp.where(sub < 3 * N_HEADS, clo,
                                           jnp.where(sub == ROW_ONES, 1.0, 0.0))))
    bias = bias_t.T.astype(BF16)
    bias_t = bias_t.astype(BF16)

    kk = (jnp.dot(h, wk_ref[...], preferred_element_type=F32)
          + jnp.dot(bias, pk_ref[...], preferred_element_type=F32))
    for hh in range(N_HEADS):
        k_ref[hh] = kk[:, hh * HEAD_PAD:(hh + 1) * HEAD_PAD].astype(BF16)

    qt = (jnp.dot(wq_ref[...], ht, preferred_element_type=F32) * Q_SCALE
          + jnp.dot(pq_ref[...], bias_t, preferred_element_type=F32))
    for hh in range(N_HEADS):
        q_ref[hh] = qt[hh * HEAD_PAD:(hh + 1) * HEAD_PAD, :].astype(BF16)

    vt = jnp.dot(wv_ref[...], ht, preferred_element_type=F32)
    ones_row = lax.broadcasted_iota(jnp.int32, (V_ROWS, tm), 0) == V_ONES_ROW
    for hh in range(N_HEADS):
        v_ref[hh] = jnp.where(ones_row, 1.0, vt[hh * V_ROWS:(hh + 1) * V_ROWS, :]).astype(BF16)


def _attn_in(x, g, wq_t, wk, wv_t, wf_t, bf_col, pq_t, pk, *, tm=ROW_TILE):
    b, s, d = x.shape
    const = lambda bi, i: (0, 0)
    return pl.pallas_call(
        functools.partial(_attn_in_kernel, tm=tm),
        grid=(b, s // tm),
        in_specs=[
            pl.BlockSpec((None, tm, d), lambda bi, i: (bi, i, 0)),
            pl.BlockSpec((1, d), const),
            pl.BlockSpec(wq_t.shape, const),
            pl.BlockSpec(wk.shape, const),
            pl.BlockSpec(wv_t.shape, const),
            pl.BlockSpec(wf_t.shape, const),
            pl.BlockSpec(bf_col.shape, const),
            pl.BlockSpec(pq_t.shape, const),
            pl.BlockSpec(pk.shape, const),
        ],
        out_specs=[
            pl.BlockSpec((None, N_HEADS, HEAD_PAD, tm), lambda bi, i: (bi, 0, 0, i)),
            pl.BlockSpec((None, N_HEADS, tm, HEAD_PAD), lambda bi, i: (bi, 0, i, 0)),
            pl.BlockSpec((None, N_HEADS, None, V_ROWS, tm), lambda bi, i: (bi, 0, i, 0, 0)),
        ],
        out_shape=[
            jax.ShapeDtypeStruct((b, N_HEADS, HEAD_PAD, s), BF16),
            jax.ShapeDtypeStruct((b, N_HEADS, s, HEAD_PAD), BF16),
            jax.ShapeDtypeStruct((b, N_HEADS, s // tm, V_ROWS, tm), BF16),
        ],
        scratch_shapes=[pltpu.VMEM((HEAD_PAD, 1), F32)],
        compiler_params=pltpu.CompilerParams(
            dimension_semantics=("arbitrary", "arbitrary"),
            vmem_limit_bytes=VMEM_LIMIT_BYTES),
        name="attn_in",
    )(x, g, wq_t, wk, wv_t, wf_t, bf_col, pq_t, pk)


def _flash_kernel(q_ref, k_ref, v_ref, o_ref, s_refs, smax_refs, m_ref, acc_ref, *, bk):
    i = pl.program_id(2)
    bq = 2 * bk
    sub = bk // V_BLOCK
    n_tiles = bq // QUERY_TILE
    m_ref[...] = jnp.full_like(m_ref, -jnp.inf)
    acc_ref[...] = jnp.zeros_like(acc_ref)

    def scores_tile(t, slot, c):
        cols = slice(c * QUERY_TILE, (c + 1) * QUERY_TILE)
        start = pl.multiple_of(t * bk, bk)
        s = jnp.dot(k_ref[pl.ds(start, bk), :], q_ref[:, cols],
                    preferred_element_type=F32)
        s_refs[slot][:, cols] = s
        smax_refs[slot][:, cols] = jnp.max(s, axis=0, keepdims=True)

    def consume_tile(t, slot, c, diag_block=None):
        cols = slice(c * QUERY_TILE, (c + 1) * QUERY_TILE)
        s = s_refs[slot][:, cols]
        s_max = smax_refs[slot][:, cols]
        if diag_block is not None and c * QUERY_TILE < (diag_block + 1) * bk:
            key = lax.broadcasted_iota(jnp.int32, (bk, QUERY_TILE), 0) + diag_block * bk
            qry = lax.broadcasted_iota(jnp.int32, (bk, QUERY_TILE), 1) + c * QUERY_TILE
            s = jnp.where(key <= qry, s, NEG_INF)
            s_max = jnp.max(s, axis=0, keepdims=True)
        m_prev = m_ref[:, cols]
        m_new = jnp.maximum(m_prev, s_max)
        alpha = jnp.exp2(m_prev - m_new)
        p = jnp.exp2(s - m_new).astype(BF16)
        pv = None
        for u in range(sub):
            part = jnp.dot(v_ref[t * sub + u], p[u * V_BLOCK:(u + 1) * V_BLOCK, :],
                           preferred_element_type=F32)
            pv = part if pv is None else pv + part
        acc_ref[:, cols] = alpha * acc_ref[:, cols] + pv
        m_ref[:, cols] = m_new

    for c in range(n_tiles):
        scores_tile(0, 0, c)

    def body(g, carry):
        t = 2 * g
        for half in range(2):
            for c in range(n_tiles):
                consume_tile(t + half, half, c)
                scores_tile(t + half + 1, 1 - half, c)
        return carry

    lax.fori_loop(0, i, body, 0)
    t = 2 * i
    first_visible = bk // QUERY_TILE
    for c in range(n_tiles):
        consume_tile(t, 0, c, diag_block=0)
        if c >= first_visible:
            scores_tile(t + 1, 1, c)
    for c in range(first_visible, n_tiles):
        consume_tile(t + 1, 1, c, diag_block=1)

    acc = acc_ref[...]
    o_ref[...] = (acc[0:HEAD_DIM] / acc[V_ONES_ROW:V_ONES_ROW + 1]).astype(o_ref.dtype)


def _flash(q_t, k, v_t, *, bk=ATTN_BLOCK):
    b, nh, s, hp = k.shape
    bq = 2 * bk
    assert v_t.shape == (b, nh, s // V_BLOCK, V_ROWS, V_BLOCK)
    score_bufs = [pltpu.VMEM((bk, bq), F32) for _ in range(2)]
    smax_bufs = [pltpu.VMEM((1, bq), F32) for _ in range(2)]
    return pl.pallas_call(
        functools.partial(_flash_kernel, bk=bk),
        grid=(b, nh, s // bq),
        in_specs=[
            pl.BlockSpec((None, None, hp, bq), lambda bi, h, i: (bi, h, 0, i)),
            pl.BlockSpec((None, None, s, hp), lambda bi, h, i: (bi, h, 0, 0)),
            pl.BlockSpec((None, None, s // V_BLOCK, V_ROWS, V_BLOCK),
                         lambda bi, h, i: (bi, h, 0, 0, 0)),
        ],
        out_specs=pl.BlockSpec((None, None, HEAD_DIM, bq), lambda bi, h, i: (bi, h, 0, i)),
        out_shape=jax.ShapeDtypeStruct((b, nh, HEAD_DIM, s), BF16),
        scratch_shapes=[score_bufs, smax_bufs,
                        pltpu.VMEM((1, bq), F32), pltpu.VMEM((V_ROWS, bq), F32)],
        compiler_params=pltpu.CompilerParams(
            dimension_semantics=("arbitrary", "arbitrary", "arbitrary"),
            vmem_limit_bytes=VMEM_LIMIT_BYTES),
        name="fox_flash",
    )(q_t, k, v_t)


def _attn_out_kernel(o_ref, x_ref, w_ref, g_ref, y_ref):
    o_t = jnp.concatenate([o_ref[hh] for hh in range(N_HEADS)], axis=0)
    m_t = jnp.dot(w_ref[...], o_t, preferred_element_type=F32)
    y_ref[...] = x_ref[...] + _rms(m_t.T, g_ref[...])


def _attn_out(o_t, x, w_out_t, g, *, tm=ROW_TILE):
    b, s, d = x.shape
    const = lambda bi, i: (0, 0)
    return pl.pallas_call(
        _attn_out_kernel,
        grid=(b, s // tm),
        in_specs=[
            pl.BlockSpec((None, N_HEADS, HEAD_DIM, tm), lambda bi, i: (bi, 0, 0, i)),
            pl.BlockSpec((None, tm, d), lambda bi, i: (bi, i, 0)),
            pl.BlockSpec(w_out_t.shape, const),
            pl.BlockSpec((1, d), const),
        ],
        out_specs=pl.BlockSpec((None, tm, d), lambda bi, i: (bi, i, 0)),
        out_shape=jax.ShapeDtypeStruct(x.shape, F32),
        compiler_params=pltpu.CompilerParams(
            dimension_semantics=("arbitrary", "arbitrary"),
            vmem_limit_bytes=VMEM_LIMIT_BYTES),
        name="attn_out",
    )(o_t, x, w_out_t, g)


HALO = 8


def _conv_kernel(x_ref, g0_ref, win_ref, cw_ref, wout_ref, g1_ref, y_ref, z_ref, *, tm):
    d = D_MODEL

    @pl.when(pl.program_id(1) == 0)
    def _():
        z_ref[0:HALO, :] = jnp.zeros((HALO, d), F32)

    @pl.when(pl.program_id(1) > 0)
    def _():
        z_ref[0:HALO, :] = z_ref[tm:tm + HALO, :]

    x = x_ref[...]
    h = _rms(x, g0_ref[...]).astype(BF16)
    c_gate = jnp.dot(h, win_ref[:, d:2 * d], preferred_element_type=F32)
    u = jnp.dot(h, win_ref[:, 2 * d:3 * d], preferred_element_type=F32)
    z = c_gate * u
    z_ref[HALO:HALO + tm, :] = z
    cw = cw_ref[...]
    zc = (cw[2:3, :] * z
          + cw[1:2, :] * z_ref[HALO - 1:HALO - 1 + tm, :]
          + cw[0:1, :] * z_ref[HALO - 2:HALO - 2 + tm, :])
    b_gate = jnp.dot(h, win_ref[:, 0:d], preferred_element_type=F32)
    y = (b_gate * zc).astype(BF16)
    m = jnp.dot(y, wout_ref[...], preferred_element_type=F32)
    y_ref[...] = x + _rms(m, g1_ref[...])


def _conv_layer(x, g0, w_in, conv_w, w_out, g1, *, tm=ROW_TILE):
    b, s, d = x.shape
    const = lambda bi, i: (0, 0)
    return pl.pallas_call(
        functools.partial(_conv_kernel, tm=tm),
        grid=(b, s // tm),
        in_specs=[
            pl.BlockSpec((None, tm, d), lambda bi, i: (bi, i, 0)),
            pl.BlockSpec((1, d), const),
            pl.BlockSpec(w_in.shape, const),
            pl.BlockSpec(conv_w.shape, const),
            pl.BlockSpec(w_out.shape, const),
            pl.BlockSpec((1, d), const),
        ],
        out_specs=pl.BlockSpec((None, tm, d), lambda bi, i: (bi, i, 0)),
        out_shape=jax.ShapeDtypeStruct(x.shape, F32),
        scratch_shapes=[pltpu.VMEM((tm + HALO, d), F32)],
        compiler_params=pltpu.CompilerParams(
            dimension_semantics=("arbitrary", "arbitrary"),
            vmem_limit_bytes=VMEM_LIMIT_BYTES),
        name="conv_mixer",
    )(x, g0, w_in, conv_w, w_out, g1)


FF_CHUNK = 1024


def _mlp_kernel(x_ref, g2_ref, wup_ref, wdn_ref, g3_ref, y_ref):
    x = x_ref[...]
    h = _rms(x, g2_ref[...]).astype(BF16)
    d_ff = wup_ref.shape[1]
    f = None
    for c in range(d_ff // FF_CHUNK):
        cols = slice(c * FF_CHUNK, (c + 1) * FF_CHUNK)
        u = jnp.dot(h, wup_ref[:, cols], preferred_element_type=F32)
        a = jnp.square(jnp.maximum(u, 0.0)).astype(BF16)
        part = jnp.dot(a, wdn_ref[cols, :], preferred_element_type=F32)
        f = part if f is None else f + part
    y_ref[...] = x + _rms(f, g3_ref[...])


def _mlp(x, g2, w_up, w_down, g3, *, tm=ROW_TILE):
    b, s, d = x.shape
    const = lambda bi, i: (0, 0)
    return pl.pallas_call(
        _mlp_kernel,
        grid=(b, s // tm),
        in_specs=[
            pl.BlockSpec((None, tm, d), lambda bi, i: (bi, i, 0)),
            pl.BlockSpec((1, d), const),
            pl.BlockSpec(w_up.shape, const),
            pl.BlockSpec(w_down.shape, const),
            pl.BlockSpec((1, d), const),
        ],
        out_specs=pl.BlockSpec((None, tm, d), lambda bi, i: (bi, i, 0)),
        out_shape=jax.ShapeDtypeStruct(x.shape, F32),
        compiler_params=pltpu.CompilerParams(
            dimension_semantics=("arbitrary", "arbitrary"),
            vmem_limit_bytes=VMEM_LIMIT_BYTES),
        name="sq_relu_mlp",
    )(x, g2, w_up, w_down, g3)


def _ple_kernel(x_ref, p_ref, g4_ref, wg_ref, wp_ref, g5_ref, y_ref):
    x = x_ref[...]
    h = _rms(x, g4_ref[...]).astype(BF16)
    gate = jax.nn.sigmoid(jnp.dot(h, wg_ref[...], preferred_element_type=F32))
    e = jnp.dot(p_ref[...].astype(BF16), wp_ref[...], preferred_element_type=F32) * gate
    y_ref[...] = x + _rms(e, g5_ref[...])


def _ple(x, p, layer, g4, w_gate, w_proj, g5, *, tm=ROW_TILE):
    b, s, d = x.shape
    const = lambda bi, i: (0, 0)
    return pl.pallas_call(
        _ple_kernel,
        grid=(b, s // tm),
        in_specs=[
            pl.BlockSpec((None, tm, d), lambda bi, i: (bi, i, 0)),
            pl.BlockSpec((None, None, tm, p.shape[-1]), lambda bi, i: (layer, bi, i, 0)),
            pl.BlockSpec((1, d), const),
            pl.BlockSpec(w_gate.shape, const),
            pl.BlockSpec(w_proj.shape, const),
            pl.BlockSpec((1, d), const),
        ],
        out_specs=pl.BlockSpec((None, tm, d), lambda bi, i: (bi, i, 0)),
        out_shape=jax.ShapeDtypeStruct(x.shape, F32),
        compiler_params=pltpu.CompilerParams(
            dimension_semantics=("arbitrary", "arbitrary"),
            vmem_limit_bytes=VMEM_LIMIT_BYTES),
        name="ple_gate",
    )(x, p, g4, w_gate, w_proj, g5)


def _pad_heads(w, width):
    d = w.shape[0]
    w = w.reshape(d, N_HEADS, HEAD_DIM)
    w = jnp.pad(w, ((0, 0), (0, 0), (0, width - HEAD_DIM)))
    return w.reshape(d, N_HEADS * width)


def _prep_attn_weights(w_in, b_f, w_out):
    d = D_MODEL
    wq_t = _pad_heads(w_in[:, 0:d], HEAD_PAD).T.astype(BF16)
    wk = _pad_heads(w_in[:, d:2 * d], HEAD_PAD).astype(BF16)
    wv_t = _pad_heads(w_in[:, 2 * d:3 * d], V_ROWS).T.astype(BF16)
    wf = w_in[:, 3 * d:]
    wf_t = jnp.pad(jnp.concatenate([wf, wf, wf], axis=1),
                   ((0, 0), (0, HEAD_PAD - 3 * N_HEADS))).T.astype(BF16)
    bf_col = jnp.pad(jnp.concatenate([b_f, b_f, b_f]), (0, HEAD_PAD - 3 * N_HEADS))
    bf_col = bf_col.reshape(HEAD_PAD, 1).astype(F32)
    return wq_t, wk, wv_t, wf_t, bf_col, w_out.T.astype(BF16)


def kernel(x, p, norm_g, w_attn_in, b_forget, w_attn_out, w_conv_in, conv_w, w_conv_out,
           w_mlp_up, w_mlp_down, w_ple_proj, w_ple_gate):
    depth = norm_g.shape[0]
    pq_np, pk_np = _placement_matrices()
    pq_t = jnp.asarray(pq_np, dtype=BF16)
    pk = jnp.asarray(pk_np, dtype=BF16)
    for i in range(depth):
        g = norm_g[i].astype(F32)
        gi = lambda n: g[n:n + 1]
        j = i // 2
        if i % 2 == 0:
            wq_t, wk, wv_t, wf_t, bf_col, w_out_t = _prep_attn_weights(
                w_attn_in[j], b_forget[j], w_attn_out[j])
            q_t, k, v_t = _attn_in(x, gi(0), wq_t, wk, wv_t, wf_t, bf_col, pq_t, pk)
            o_t = _flash(q_t, k, v_t)
            x = _attn_out(o_t, x, w_out_t, gi(1))
        else:
            x = _conv_layer(x, gi(0), w_conv_in[j].astype(BF16), conv_w[j].astype(F32),
                            w_conv_out[j].astype(BF16), gi(1))
        x = _mlp(x, gi(2), w_mlp_up[i].astype(BF16), w_mlp_down[i].astype(BF16), gi(3))
        x = _ple(x, p, i, gi(4), w_ple_gate[i].astype(BF16), w_ple_proj[i].astype(BF16),
                 gi(5))
    return x
```

```python
import functools

import numpy as np
import jax
import jax.numpy as jnp
from jax import lax
from jax.experimental import pallas as pl
from jax.experimental.pallas import tpu as pltpu

F32 = jnp.float32
BF16 = jnp.bfloat16

D_MODEL = 1024
N_HEADS = 16
HEAD_DIM = D_MODEL // N_HEADS
HEAD_PAD = 128
QK_PAD = N_HEADS * HEAD_PAD
V_ROWS = 128
V_ONES_ROW = HEAD_DIM
CONV_WIDTH = 3
RMS_EPS = 1e-6
NEG_INF = -1e30
LOG2E = 1.4426950408889634
Q_SCALE = (HEAD_DIM ** -0.5) * LOG2E

VMEM_LIMIT_BYTES = 56 * 1024 * 1024

ROW_TILE = 512
V_BLOCK = ROW_TILE
ATTN_BLOCK = 1024
QUERY_TILE = 256
SCORE_LOOKAHEAD = 4

SLOT_CQ = (64, 65, 66)
SLOT_CK = (67, 68, 69)
ROW_ONES = 48


def _rms(xf, g):
    ms = jnp.mean(xf * xf, axis=-1, keepdims=True)
    return xf * lax.rsqrt(ms + RMS_EPS) * g


def _split3(x):
    hi = x.astype(BF16).astype(F32)
    r = x - hi
    mid = r.astype(BF16).astype(F32)
    lo = (r - mid).astype(BF16).astype(F32)
    return hi, mid, lo


def _k_placement_matrix():
    pk = np.zeros((HEAD_PAD, QK_PAD), np.float32)
    for h in range(N_HEADS):
        base = h * HEAD_PAD
        for part in range(3):
            pk[part * N_HEADS + h, base + SLOT_CK[part]] = -1.0
            pk[ROW_ONES, base + SLOT_CQ[part]] = 1.0
    return pk


def _attn_in_kernel(x_ref, g_ref, wq_ref, wk_ref, wv_ref, wf_ref, bf_ref, pk_ref,
                    q_ref, k_ref, v_ref, carry_ref, *, tm):
    @pl.when(pl.program_id(1) == 0)
    def _():
        carry_ref[...] = jnp.zeros_like(carry_ref)

    hn = _rms(x_ref[...], g_ref[...])
    h = hn.astype(BF16)
    ht = hn.T.astype(BF16)

    f = jnp.dot(wf_ref[...], ht, preferred_element_type=F32) + bf_ref[...]
    logf = (jnp.minimum(f, 0.0) - jnp.log1p(jnp.exp(-jnp.abs(f)))) * LOG2E

    row = lax.broadcasted_iota(jnp.int32, (tm, tm), 0)
    col = lax.broadcasted_iota(jnp.int32, (tm, tm), 1)
    triu = (row <= col).astype(BF16)
    c = carry_ref[...]
    for part in _split3(logf):
        c = c + jnp.dot(part.astype(BF16), triu, preferred_element_type=F32)
    carry_ref[...] = c[:, tm - 1:tm]

    chi, cmid, clo = _split3(c)

    sub = lax.broadcasted_iota(jnp.int32, (HEAD_PAD, tm), 0)
    bias_t = jnp.where(sub < N_HEADS, chi,
                       jnp.where(sub < 2 * N_HEADS, cmid,
                                 jnp.where(sub < 3 * N_HEADS, clo,
                                           jnp.where(sub == ROW_ONES, 1.0, 0.0))))
    bias = bias_t.T.astype(BF16)
    kk = jnp.dot(h, wk_ref[...], preferred_element_type=F32)
    kb = jnp.dot(bias, pk_ref[...], preferred_element_type=F32)
    low = lax.broadcasted_iota(jnp.int32, (tm, HEAD_PAD), 1) < HEAD_DIM
    for pr in range(N_HEADS // 2):
        pair = kk[:, pr * HEAD_PAD:(pr + 1) * HEAD_PAD]
        for odd in range(2):
            hh = 2 * pr + odd
            feats = pltpu.roll(pair, HEAD_DIM, axis=1) if odd else pair
            k_ref[hh] = (jnp.where(low, feats, 0.0)
                         + kb[:, hh * HEAD_PAD:(hh + 1) * HEAD_PAD]).astype(BF16)

    qt = jnp.dot(wq_ref[...], ht, preferred_element_type=F32) * Q_SCALE
    slot = lax.broadcasted_iota(jnp.int32, (HEAD_PAD - HEAD_DIM, tm), 0)
    for hh in range(N_HEADS):
        bias_rows = jnp.where(slot == 0, chi[hh:hh + 1],
                              jnp.where(slot == 1, cmid[hh:hh + 1],
                                        jnp.where(slot == 2, clo[hh:hh + 1],
                                                  jnp.where(slot < 6, 1.0, 0.0))))
        q_ref[hh] = jnp.concatenate(
            [qt[hh * HEAD_DIM:(hh + 1) * HEAD_DIM], bias_rows], axis=0).astype(BF16)

    vt = jnp.dot(wv_ref[...], ht, preferred_element_type=F32)
    pad_rows = jnp.where(
        lax.broadcasted_iota(jnp.int32, (V_ROWS - HEAD_DIM, tm), 0) == 0, 1.0, 0.0)
    for hh in range(N_HEADS):
        v_ref[hh] = jnp.concatenate(
            [vt[hh * HEAD_DIM:(hh + 1) * HEAD_DIM], pad_rows], axis=0).astype(BF16)


def _attn_in(x, g, wq_t, wk, wv_t, wf_t, bf_col, pk, *, tm=ROW_TILE):
    b, s, d = x.shape
    const = lambda bi, i: (0, 0)
    return pl.pallas_call(
        functools.partial(_attn_in_kernel, tm=tm),
        grid=(b, s // tm),
        in_specs=[
            pl.BlockSpec((None, tm, d), lambda bi, i: (bi, i, 0)),
            pl.BlockSpec((1, d), const),
            pl.BlockSpec(wq_t.shape, const),
            pl.BlockSpec(wk.shape, const),
            pl.BlockSpec(wv_t.shape, const),
            pl.BlockSpec(wf_t.shape, const),
            pl.BlockSpec(bf_col.shape, const),
            pl.BlockSpec(pk.shape, const),
        ],
        out_specs=[
            pl.BlockSpec((None, N_HEADS, HEAD_PAD, tm), lambda bi, i: (bi, 0, 0, i)),
            pl.BlockSpec((None, N_HEADS, tm, HEAD_PAD), lambda bi, i: (bi, 0, i, 0)),
            pl.BlockSpec((None, N_HEADS, None, V_ROWS, tm), lambda bi, i: (bi, 0, i, 0, 0)),
        ],
        out_shape=[
            jax.ShapeDtypeStruct((b, N_HEADS, HEAD_PAD, s), BF16),
            jax.ShapeDtypeStruct((b, N_HEADS, s, HEAD_PAD), BF16),
            jax.ShapeDtypeStruct((b, N_HEADS, s // tm, V_ROWS, tm), BF16),
        ],
        scratch_shapes=[pltpu.VMEM((HEAD_PAD, 1), F32)],
        compiler_params=pltpu.CompilerParams(
            dimension_semantics=("arbitrary", "arbitrary"),
            vmem_limit_bytes=VMEM_LIMIT_BYTES),
        name="attn_in",
    )(x, g, wq_t, wk, wv_t, wf_t, bf_col, pk)


def _flash_kernel(q_ref, k_ref, v_ref, o_ref, s_refs, smax_refs, m_ref, acc_ref, *, bk):
    i = pl.program_id(2)
    bq = 2 * bk
    sub = bk // V_BLOCK
    n_tiles = bq // QUERY_TILE
    m_ref[...] = jnp.full_like(m_ref, -jnp.inf)
    acc_ref[...] = jnp.zeros_like(acc_ref)

    def scores_tile(t, slot, c, rows=bk):
        cols = slice(c * QUERY_TILE, (c + 1) * QUERY_TILE)
        start = pl.multiple_of(t * bk, bk)
        s = jnp.dot(k_ref[pl.ds(start, rows), :], q_ref[:, cols],
                    preferred_element_type=F32)
        s_refs[slot][0:rows, cols] = s
        smax_refs[slot][:, cols] = jnp.max(s, axis=0, keepdims=True)

    def consume_tile(t, slot, c, rows=bk, key_offset=None):
        cols = slice(c * QUERY_TILE, (c + 1) * QUERY_TILE)
        s = s_refs[slot][0:rows, cols]
        s_max = smax_refs[slot][:, cols]
        if key_offset is not None:
            key = lax.broadcasted_iota(jnp.int32, (rows, QUERY_TILE), 0) + key_offset
            qry = lax.broadcasted_iota(jnp.int32, (rows, QUERY_TILE), 1) + c * QUERY_TILE
            s = jnp.where(key <= qry, s, NEG_INF)
            s_max = jnp.max(s, axis=0, keepdims=True)
        m_prev = m_ref[:, cols]
        m_new = jnp.maximum(m_prev, s_max)
        alpha = jnp.exp2(m_prev - m_new)
        p = jnp.exp2(s - m_new).astype(BF16)
        pv = None
        for u in range(pl.cdiv(rows, V_BLOCK)):
            n = min(V_BLOCK, rows - u * V_BLOCK)
            part = jnp.dot(v_ref[t * sub + u, :, 0:n], p[u * V_BLOCK:u * V_BLOCK + n, :],
                           preferred_element_type=F32)
            pv = part if pv is None else pv + part
        acc_ref[:, cols] = alpha * acc_ref[:, cols] + pv
        m_ref[:, cols] = m_new

    diag = []
    for r in range(2):
        for c in range(n_tiles):
            rows = min(bk, (c + 1) * QUERY_TILE - r * bk)
            if rows > 0:
                diag.append((r, c, rows, rows < bk or c * QUERY_TILE < (r + 1) * bk))

    la = SCORE_LOOKAHEAD

    @pl.when(i == 0)
    def _():
        for r, c, rows, _ in diag[:la]:
            scores_tile(r, r, c, rows)

    @pl.when(i > 0)
    def _():
        for c in range(la):
            scores_tile(0, 0, c)

    def body(g, carry):
        t = 2 * g
        for half in range(2):
            for c in range(n_tiles):
                consume_tile(t + half, half, c)
                ahead = half + (c + la) // n_tiles
                scores_tile(t + ahead, ahead % 2, (c + la) % n_tiles)
        return carry

    lax.fori_loop(0, i, body, 0)
    t = 2 * i
    for idx, (r, c, rows, masked) in enumerate(diag):
        consume_tile(t + r, r, c, rows, key_offset=r * bk if masked else None)
        if idx + la < len(diag):
            r2, c2, rows2, _ = diag[idx + la]
            scores_tile(t + r2, r2, c2, rows2)

    acc = acc_ref[...]
    o_ref[...] = (acc[0:HEAD_DIM] / acc[V_ONES_ROW:V_ONES_ROW + 1]).astype(o_ref.dtype)


def _flash(q_t, k, v_t, *, bk=ATTN_BLOCK):
    b, nh, s, hp = k.shape
    bq = 2 * bk
    assert v_t.shape == (b, nh, s // V_BLOCK, V_ROWS, V_BLOCK)
    score_bufs = [pltpu.VMEM((bk, bq), F32) for _ in range(2)]
    smax_bufs = [pltpu.VMEM((1, bq), F32) for _ in range(2)]
    return pl.pallas_call(
        functools.partial(_flash_kernel, bk=bk),
        grid=(b, nh, s // bq),
        in_specs=[
            pl.BlockSpec((None, None, hp, bq), lambda bi, h, i: (bi, h, 0, i)),
            pl.BlockSpec((None, None, s, hp), lambda bi, h, i: (bi, h, 0, 0)),
            pl.BlockSpec((None, None, s // V_BLOCK, V_ROWS, V_BLOCK),
                         lambda bi, h, i: (bi, h, 0, 0, 0)),
        ],
        out_specs=pl.BlockSpec((None, None, HEAD_DIM, bq), lambda bi, h, i: (bi, h, 0, i)),
        out_shape=jax.ShapeDtypeStruct((b, nh, HEAD_DIM, s), BF16),
        scratch_shapes=[score_bufs, smax_bufs,
                        pltpu.VMEM((1, bq), F32), pltpu.VMEM((V_ROWS, bq), F32)],
        compiler_params=pltpu.CompilerParams(
            dimension_semantics=("arbitrary", "arbitrary", "arbitrary"),
            vmem_limit_bytes=VMEM_LIMIT_BYTES),
        name="fox_flash",
    )(q_t, k, v_t)


def _attn_out_kernel(o_ref, x_ref, w_ref, g_ref, y_ref):
    o_t = jnp.concatenate([o_ref[hh] for hh in range(N_HEADS)], axis=0)
    m_t = jnp.dot(w_ref[...], o_t, preferred_element_type=F32)
    y_ref[...] = x_ref[...] + _rms(m_t.T, g_ref[...])


def _attn_out(o_t, x, w_out_t, g, *, tm=ROW_TILE):
    b, s, d = x.shape
    const = lambda bi, i: (0, 0)
    return pl.pallas_call(
        _attn_out_kernel,
        grid=(b, s // tm),
        in_specs=[
            pl.BlockSpec((None, N_HEADS, HEAD_DIM, tm), lambda bi, i: (bi, 0, 0, i)),
            pl.BlockSpec((None, tm, d), lambda bi, i: (bi, i, 0)),
            pl.BlockSpec(w_out_t.shape, const),
            pl.BlockSpec((1, d), const),
        ],
        out_specs=pl.BlockSpec((None, tm, d), lambda bi, i: (bi, i, 0)),
        out_shape=jax.ShapeDtypeStruct(x.shape, F32),
        compiler_params=pltpu.CompilerParams(
            dimension_semantics=("arbitrary", "arbitrary"),
            vmem_limit_bytes=VMEM_LIMIT_BYTES),
        name="attn_out",
    )(o_t, x, w_out_t, g)


HALO = 8


def _conv_kernel(x_ref, g0_ref, win_ref, cw_ref, wout_ref, g1_ref, y_ref, z_ref, *, tm):
    d = D_MODEL

    @pl.when(pl.program_id(1) == 0)
    def _():
        z_ref[0:HALO, :] = jnp.zeros((HALO, d), F32)

    @pl.when(pl.program_id(1) > 0)
    def _():
        z_ref[0:HALO, :] = z_ref[tm:tm + HALO, :]

    x = x_ref[...]
    h = _rms(x, g0_ref[...]).astype(BF16)
    c_gate = jnp.dot(h, win_ref[:, d:2 * d], preferred_element_type=F32)
    u = jnp.dot(h, win_ref[:, 2 * d:3 * d], preferred_element_type=F32)
    z = c_gate * u
    z_ref[HALO:HALO + tm, :] = z
    cw = cw_ref[...]
    zc = (cw[2:3, :] * z
          + cw[1:2, :] * z_ref[HALO - 1:HALO - 1 + tm, :]
          + cw[0:1, :] * z_ref[HALO - 2:HALO - 2 + tm, :])
    b_gate = jnp.dot(h, win_ref[:, 0:d], preferred_element_type=F32)
    y = (b_gate * zc).astype(BF16)
    m = jnp.dot(y, wout_ref[...], preferred_element_type=F32)
    y_ref[...] = x + _rms(m, g1_ref[...])


def _conv_layer(x, g0, w_in, conv_w, w_out, g1, *, tm=ROW_TILE):
    b, s, d = x.shape
    const = lambda bi, i: (0, 0)
    return pl.pallas_call(
        functools.partial(_conv_kernel, tm=tm),
        grid=(b, s // tm),
        in_specs=[
            pl.BlockSpec((None, tm, d), lambda bi, i: (bi, i, 0)),
            pl.BlockSpec((1, d), const),
            pl.BlockSpec(w_in.shape, const),
            pl.BlockSpec(conv_w.shape, const),
            pl.BlockSpec(w_out.shape, const),
            pl.BlockSpec((1, d), const),
        ],
        out_specs=pl.BlockSpec((None, tm, d), lambda bi, i: (bi, i, 0)),
        out_shape=jax.ShapeDtypeStruct(x.shape, F32),
        scratch_shapes=[pltpu.VMEM((tm + HALO, d), F32)],
        compiler_params=pltpu.CompilerParams(
            dimension_semantics=("arbitrary", "arbitrary"),
            vmem_limit_bytes=VMEM_LIMIT_BYTES),
        name="conv_mixer",
    )(x, g0, w_in, conv_w, w_out, g1)


FF_CHUNK = 1024


def _mlp_kernel(x_ref, g2_ref, wup_ref, wdn_ref, g3_ref, y_ref):
    x = x_ref[...]
    h = _rms(x, g2_ref[...]).astype(BF16)
    d_ff = wup_ref.shape[1]
    f = None
    for c in range(d_ff // FF_CHUNK):
        cols = slice(c * FF_CHUNK, (c + 1) * FF_CHUNK)
        u = jnp.dot(h, wup_ref[:, cols], preferred_element_type=F32)
        a = jnp.square(jnp.maximum(u, 0.0)).astype(BF16)
        part = jnp.dot(a, wdn_ref[cols, :], preferred_element_type=F32)
        f = part if f is None else f + part
    y_ref[...] = x + _rms(f, g3_ref[...])


def _mlp(x, g2, w_up, w_down, g3, *, tm=ROW_TILE):
    b, s, d = x.shape
    const = lambda bi, i: (0, 0)
    return pl.pallas_call(
        _mlp_kernel,
        grid=(b, s // tm),
        in_specs=[
            pl.BlockSpec((None, tm, d), lambda bi, i: (bi, i, 0)),
            pl.BlockSpec((1, d), const),
            pl.BlockSpec(w_up.shape, const),
            pl.BlockSpec(w_down.shape, const),
            pl.BlockSpec((1, d), const),
        ],
        out_specs=pl.BlockSpec((None, tm, d), lambda bi, i: (bi, i, 0)),
        out_shape=jax.ShapeDtypeStruct(x.shape, F32),
        compiler_params=pltpu.CompilerParams(
            dimension_semantics=("arbitrary", "arbitrary"),
            vmem_limit_bytes=VMEM_LIMIT_BYTES),
        name="sq_relu_mlp",
    )(x, g2, w_up, w_down, g3)


def _ple_kernel(x_ref, p_ref, g4_ref, wg_ref, wp_ref, g5_ref, y_ref):
    x = x_ref[...]
    h = _rms(x, g4_ref[...]).astype(BF16)
    gate = jax.nn.sigmoid(jnp.dot(h, wg_ref[...], preferred_element_type=F32))
    e = jnp.dot(p_ref[...].astype(BF16), wp_ref[...], preferred_element_type=F32) * gate
    y_ref[...] = x + _rms(e, g5_ref[...])


def _ple(x, p, layer, g4, w_gate, w_proj, g5, *, tm=ROW_TILE):
    b, s, d = x.shape
    const = lambda bi, i: (0, 0)
    return pl.pallas_call(
        _ple_kernel,
        grid=(b, s // tm),
        in_specs=[
            pl.BlockSpec((None, tm, d), lambda bi, i: (bi, i, 0)),
            pl.BlockSpec((None, None, tm, p.shape[-1]), lambda bi, i: (layer, bi, i, 0)),
            pl.BlockSpec((1, d), const),
            pl.BlockSpec(w_gate.shape, const),
            pl.BlockSpec(w_proj.shape, const),
            pl.BlockSpec((1, d), const),
        ],
        out_specs=pl.BlockSpec((None, tm, d), lambda bi, i: (bi, i, 0)),
        out_shape=jax.ShapeDtypeStruct(x.shape, F32),
        compiler_params=pltpu.CompilerParams(
            dimension_semantics=("arbitrary", "arbitrary"),
            vmem_limit_bytes=VMEM_LIMIT_BYTES),
        name="ple_gate",
    )(x, p, g4, w_gate, w_proj, g5)


def _prep_attn_weights(w_in, b_f, w_out):
    d = D_MODEL
    wq_t = w_in[:, 0:d].T.astype(BF16)
    wk = w_in[:, d:2 * d].astype(BF16)
    wv_t = w_in[:, 2 * d:3 * d].T.astype(BF16)
    wf = w_in[:, 3 * d:]
    wf_t = jnp.pad(jnp.concatenate([wf, wf, wf], axis=1),
                   ((0, 0), (0, HEAD_PAD - 3 * N_HEADS))).T.astype(BF16)
    bf_col = jnp.pad(jnp.concatenate([b_f, b_f, b_f]), (0, HEAD_PAD - 3 * N_HEADS))
    bf_col = bf_col.reshape(HEAD_PAD, 1).astype(F32)
    return wq_t, wk, wv_t, wf_t, bf_col, w_out.T.astype(BF16)


def kernel(x, p, norm_g, w_attn_in, b_forget, w_attn_out, w_conv_in, conv_w, w_conv_out,
           w_mlp_up, w_mlp_down, w_ple_proj, w_ple_gate):
    depth = norm_g.shape[0]
    pk = jnp.asarray(_k_placement_matrix(), dtype=BF16)
    for i in range(depth):
        g = norm_g[i].astype(F32)
        gi = lambda n: g[n:n + 1]
        j = i // 2
        if i % 2 == 0:
            wq_t, wk, wv_t, wf_t, bf_col, w_out_t = _prep_attn_weights(
                w_attn_in[j], b_forget[j], w_attn_out[j])
            q_t, k, v_t = _attn_in(x, gi(0), wq_t, wk, wv_t, wf_t, bf_col, pk)
            o_t = _flash(q_t, k, v_t)
            x = _attn_out(o_t, x, w_out_t, gi(1))
        else:
            x = _conv_layer(x, gi(0), w_conv_in[j].astype(BF16), conv_w[j].astype(F32),
                            w_conv_out[j].astype(BF16), gi(1))
        x = _mlp(x, gi(2), w_mlp_up[i].astype(BF16), w_mlp_down[i].astype(BF16), gi(3))
        x = _ple(x, p, i, gi(4), w_ple_gate[i].astype(BF16), w_ple_proj[i].astype(BF16),
                 gi(5))
    return x
```

```python
import functools

import numpy as np
import jax
import jax.numpy as jnp
from jax import lax
from jax.experimental import pallas as pl
from jax.experimental.pallas import tpu as pltpu

F32 = jnp.float32
BF16 = jnp.bfloat16

D_MODEL = 1024
N_HEADS = 16
HEAD_DIM = D_MODEL // N_HEADS
HEAD_PAD = 128
QK_PAD = N_HEADS * HEAD_PAD
V_ROWS = 80
V_ONES_ROW = HEAD_DIM
CONV_WIDTH = 3
RMS_EPS = 1e-6
NEG_INF = -1e30
LOG2E = 1.4426950408889634
Q_SCALE = (HEAD_DIM ** -0.5) * LOG2E

VMEM_LIMIT_BYTES = 56 * 1024 * 1024

ROW_TILE = 512
V_BLOCK = ROW_TILE
ATTN_BLOCK = 1024
QUERY_TILE = 256
SCORE_LOOKAHEAD = 4

SLOT_CQ = (64, 65, 66)
SLOT_CK = (67, 68, 69)
ROW_ONES = 48


def _rms(xf, g):
    ms = jnp.mean(xf * xf, axis=-1, keepdims=True)
    return xf * lax.rsqrt(ms + RMS_EPS) * g


def _split3(x):
    hi = x.astype(BF16).astype(F32)
    r = x - hi
    mid = r.astype(BF16).astype(F32)
    lo = (r - mid).astype(BF16).astype(F32)
    return hi, mid, lo


def _k_placement_matrix():
    pk = np.zeros((HEAD_PAD, QK_PAD), np.float32)
    for h in range(N_HEADS):
        base = h * HEAD_PAD
        for part in range(3):
            pk[part * N_HEADS + h, base + SLOT_CK[part]] = -1.0
            pk[ROW_ONES, base + SLOT_CQ[part]] = 1.0
    return pk


def _attn_in_kernel(x_ref, g_ref, wq_ref, wk_ref, wv_ref, wf_ref, bf_ref, pk_ref,
                    q_ref, k_ref, v_ref, carry_ref, *, tm):
    @pl.when(pl.program_id(1) == 0)
    def _():
        carry_ref[...] = jnp.zeros_like(carry_ref)

    hn = _rms(x_ref[...], g_ref[...])
    h = hn.astype(BF16)
    ht = hn.T.astype(BF16)

    f = jnp.dot(wf_ref[...], ht, preferred_element_type=F32) + bf_ref[...]
    logf = (jnp.minimum(f, 0.0) - jnp.log1p(jnp.exp(-jnp.abs(f)))) * LOG2E

    row = lax.broadcasted_iota(jnp.int32, (tm, tm), 0)
    col = lax.broadcasted_iota(jnp.int32, (tm, tm), 1)
    triu = (row <= col).astype(BF16)
    c = carry_ref[...]
    for part in _split3(logf):
        c = c + jnp.dot(part.astype(BF16), triu, preferred_element_type=F32)
    carry_ref[...] = c[:, tm - 1:tm]

    chi, cmid, clo = _split3(c)

    sub = lax.broadcasted_iota(jnp.int32, (HEAD_PAD, tm), 0)
    bias_t = jnp.where(sub < N_HEADS, chi,
                       jnp.where(sub < 2 * N_HEADS, cmid,
                                 jnp.where(sub < 3 * N_HEADS, clo,
                                           jnp.where(sub == ROW_ONES, 1.0, 0.0))))
    bias = bias_t.T.astype(BF16)
    kk = jnp.dot(h, wk_ref[...], preferred_element_type=F32)
    kb = jnp.dot(bias, pk_ref[...], preferred_element_type=F32)
    low = lax.broadcasted_iota(jnp.int32, (tm, HEAD_PAD), 1) < HEAD_DIM
    for pr in range(N_HEADS // 2):
        pair = kk[:, pr * HEAD_PAD:(pr + 1) * HEAD_PAD]
        for odd in range(2):
            hh = 2 * pr + odd
            feats = pltpu.roll(pair, HEAD_DIM, axis=1) if odd else pair
            k_ref[hh] = (jnp.where(low, feats, 0.0)
                         + kb[:, hh * HEAD_PAD:(hh + 1) * HEAD_PAD]).astype(BF16)

    qt = jnp.dot(wq_ref[...], ht, preferred_element_type=F32) * Q_SCALE
    slot = lax.broadcasted_iota(jnp.int32, (HEAD_PAD - HEAD_DIM, tm), 0)
    for hh in range(N_HEADS):
        bias_rows = jnp.where(slot == 0, chi[hh:hh + 1],
                              jnp.where(slot == 1, cmid[hh:hh + 1],
                                        jnp.where(slot == 2, clo[hh:hh + 1],
                                                  jnp.where(slot < 6, 1.0, 0.0))))
        q_ref[hh] = jnp.concatenate(
            [qt[hh * HEAD_DIM:(hh + 1) * HEAD_DIM], bias_rows], axis=0).astype(BF16)

    vt = jnp.dot(wv_ref[...], ht, preferred_element_type=F32)
    pad_rows = jnp.where(
        lax.broadcasted_iota(jnp.int32, (V_ROWS - HEAD_DIM, tm), 0) == 0, 1.0, 0.0)
    for hh in range(N_HEADS):
        v_ref[hh] = jnp.concatenate(
            [vt[hh * HEAD_DIM:(hh + 1) * HEAD_DIM], pad_rows], axis=0).astype(BF16)


def _attn_in(x, g, wq_t, wk, wv_t, wf_t, bf_col, pk, *, tm=ROW_TILE):
    b, s, d = x.shape
    const = lambda bi, i: (0, 0)
    return pl.pallas_call(
        functools.partial(_attn_in_kernel, tm=tm),
        grid=(b, s // tm),
        in_specs=[
            pl.BlockSpec((None, tm, d), lambda bi, i: (bi, i, 0)),
            pl.BlockSpec((1, d), const),
            pl.BlockSpec(wq_t.shape, const),
            pl.BlockSpec(wk.shape, const),
            pl.BlockSpec(wv_t.shape, const),
            pl.BlockSpec(wf_t.shape, const),
            pl.BlockSpec(bf_col.shape, const),
            pl.BlockSpec(pk.shape, const),
        ],
        out_specs=[
            pl.BlockSpec((None, N_HEADS, HEAD_PAD, tm), lambda bi, i: (bi, 0, 0, i)),
            pl.BlockSpec((None, N_HEADS, tm, HEAD_PAD), lambda bi, i: (bi, 0, i, 0)),
            pl.BlockSpec((None, N_HEADS, None, V_ROWS, tm), lambda bi, i: (bi, 0, i, 0, 0)),
        ],
        out_shape=[
            jax.ShapeDtypeStruct((b, N_HEADS, HEAD_PAD, s), BF16),
            jax.ShapeDtypeStruct((b, N_HEADS, s, HEAD_PAD), BF16),
            jax.ShapeDtypeStruct((b, N_HEADS, s // tm, V_ROWS, tm), BF16),
        ],
        scratch_shapes=[pltpu.VMEM((HEAD_PAD, 1), F32)],
        compiler_params=pltpu.CompilerParams(
            dimension_semantics=("arbitrary", "arbitrary"),
            vmem_limit_bytes=VMEM_LIMIT_BYTES),
        name="attn_in",
    )(x, g, wq_t, wk, wv_t, wf_t, bf_col, pk)


def _flash_kernel(q_ref, k_ref, v_ref, o_ref, s_refs, smax_refs, m_ref, acc_ref, *, bk):
    i = pl.program_id(2)
    bq = 2 * bk
    sub = bk // V_BLOCK
    n_tiles = bq // QUERY_TILE
    m_ref[...] = jnp.full_like(m_ref, -jnp.inf)
    acc_ref[...] = jnp.zeros_like(acc_ref)

    def scores_tile(t, slot, c, rows=bk):
        cols = slice(c * QUERY_TILE, (c + 1) * QUERY_TILE)
        start = pl.multiple_of(t * bk, bk)
        s = jnp.dot(k_ref[pl.ds(start, rows), :], q_ref[:, cols],
                    preferred_element_type=F32)
        s_refs[slot][0:rows, cols] = s
        smax_refs[slot][:, cols] = jnp.max(s, axis=0, keepdims=True)

    def consume_tile(t, slot, c, rows=bk, key_offset=None):
        cols = slice(c * QUERY_TILE, (c + 1) * QUERY_TILE)
        s = s_refs[slot][0:rows, cols]
        s_max = smax_refs[slot][:, cols]
        if key_offset is not None:
            key = lax.broadcasted_iota(jnp.int32, (rows, QUERY_TILE), 0) + key_offset
            qry = lax.broadcasted_iota(jnp.int32, (rows, QUERY_TILE), 1) + c * QUERY_TILE
            s = jnp.where(key <= qry, s, NEG_INF)
            s_max = jnp.max(s, axis=0, keepdims=True)
        m_prev = m_ref[:, cols]
        m_new = jnp.maximum(m_prev, s_max)
        alpha = jnp.exp2(m_prev - m_new)
        p = jnp.exp2(s - m_new).astype(BF16)
        pv = None
        for u in range(pl.cdiv(rows, V_BLOCK)):
            n = min(V_BLOCK, rows - u * V_BLOCK)
            part = jnp.dot(v_ref[t * sub + u, :, 0:n], p[u * V_BLOCK:u * V_BLOCK + n, :],
                           preferred_element_type=F32)
            pv = part if pv is None else pv + part
        acc_ref[:, cols] = alpha * acc_ref[:, cols] + pv
        m_ref[:, cols] = m_new

    diag = []
    for r in range(2):
        for c in range(n_tiles):
            rows = min(bk, (c + 1) * QUERY_TILE - r * bk)
            if rows > 0:
                diag.append((r, c, rows, rows < bk or c * QUERY_TILE < (r + 1) * bk))

    la = SCORE_LOOKAHEAD

    @pl.when(i == 0)
    def _():
        for r, c, rows, _ in diag[:la]:
            scores_tile(r, r, c, rows)

    @pl.when(i > 0)
    def _():
        for c in range(la):
            scores_tile(0, 0, c)

    def body(g, carry):
        t = 2 * g
        for half in range(2):
            for c in range(n_tiles):
                consume_tile(t + half, half, c)
                ahead = half + (c + la) // n_tiles
                scores_tile(t + ahead, ahead % 2, (c + la) % n_tiles)
        return carry

    lax.fori_loop(0, i, body, 0)
    t = 2 * i
    for idx, (r, c, rows, masked) in enumerate(diag):
        consume_tile(t + r, r, c, rows, key_offset=r * bk if masked else None)
        if idx + la < len(diag):
            r2, c2, rows2, _ = diag[idx + la]
            scores_tile(t + r2, r2, c2, rows2)

    acc = acc_ref[...]
    o_ref[...] = (acc[0:HEAD_DIM] / acc[V_ONES_ROW:V_ONES_ROW + 1]).astype(o_ref.dtype)


def _flash(q_t, k, v_t, *, bk=ATTN_BLOCK):
    b, nh, s, hp = k.shape
    bq = 2 * bk
    assert v_t.shape == (b, nh, s // V_BLOCK, V_ROWS, V_BLOCK)
    score_bufs = [pltpu.VMEM((bk, bq), F32) for _ in range(2)]
    smax_bufs = [pltpu.VMEM((1, bq), F32) for _ in range(2)]
    return pl.pallas_call(
        functools.partial(_flash_kernel, bk=bk),
        grid=(b, nh, s // bq),
        in_specs=[
            pl.BlockSpec((None, None, hp, bq), lambda bi, h, i: (bi, h, 0, i)),
            pl.BlockSpec((None, None, s, hp), lambda bi, h, i: (bi, h, 0, 0)),
            pl.BlockSpec((None, None, s // V_BLOCK, V_ROWS, V_BLOCK),
                         lambda bi, h, i: (bi, h, 0, 0, 0)),
        ],
        out_specs=pl.BlockSpec((None, None, HEAD_DIM, bq), lambda bi, h, i: (bi, h, 0, i)),
        out_shape=jax.ShapeDtypeStruct((b, nh, HEAD_DIM, s), BF16),
        scratch_shapes=[score_bufs, smax_bufs,
                        pltpu.VMEM((1, bq), F32), pltpu.VMEM((V_ROWS, bq), F32)],
        compiler_params=pltpu.CompilerParams(
            dimension_semantics=("arbitrary", "arbitrary", "arbitrary"),
            vmem_limit_bytes=VMEM_LIMIT_BYTES),
        name="fox_flash",
    )(q_t, k, v_t)


def _attn_out_kernel(o_ref, x_ref, w_ref, g_ref, y_ref):
    o_t = jnp.concatenate([o_ref[hh] for hh in range(N_HEADS)], axis=0)
    m_t = jnp.dot(w_ref[...], o_t, preferred_element_type=F32)
    y_ref[...] = x_ref[...] + _rms(m_t.T, g_ref[...])


def _attn_out(o_t, x, w_out_t, g, *, tm=ROW_TILE):
    b, s, d = x.shape
    const = lambda bi, i: (0, 0)
    return pl.pallas_call(
        _attn_out_kernel,
        grid=(b, s // tm),
        in_specs=[
            pl.BlockSpec((None, N_HEADS, HEAD_DIM, tm), lambda bi, i: (bi, 0, 0, i)),
            pl.BlockSpec((None, tm, d), lambda bi, i: (bi, i, 0)),
            pl.BlockSpec(w_out_t.shape, const),
            pl.BlockSpec((1, d), const),
        ],
        out_specs=pl.BlockSpec((None, tm, d), lambda bi, i: (bi, i, 0)),
        out_shape=jax.ShapeDtypeStruct(x.shape, F32),
        compiler_params=pltpu.CompilerParams(
            dimension_semantics=("arbitrary", "arbitrary"),
            vmem_limit_bytes=VMEM_LIMIT_BYTES),
        name="attn_out",
    )(o_t, x, w_out_t, g)


HALO = 8


def _conv_kernel(x_ref, g0_ref, win_ref, cw_ref, wout_ref, g1_ref, y_ref, z_ref, *, tm):
    d = D_MODEL

    @pl.when(pl.program_id(1) == 0)
    def _():
        z_ref[0:HALO, :] = jnp.zeros((HALO, d), F32)

    @pl.when(pl.program_id(1) > 0)
    def _():
        z_ref[0:HALO, :] = z_ref[tm:tm + HALO, :]

    x = x_ref[...]
    h = _rms(x, g0_ref[...]).astype(BF16)
    c_gate = jnp.dot(h, win_ref[:, d:2 * d], preferred_element_type=F32)
    u = jnp.dot(h, win_ref[:, 2 * d:3 * d], preferred_element_type=F32)
    z = c_gate * u
    z_ref[HALO:HALO + tm, :] = z
    cw = cw_ref[...]
    zc = (cw[2:3, :] * z
          + cw[1:2, :] * z_ref[HALO - 1:HALO - 1 + tm, :]
          + cw[0:1, :] * z_ref[HALO - 2:HALO - 2 + tm, :])
    b_gate = jnp.dot(h, win_ref[:, 0:d], preferred_element_type=F32)
    y = (b_gate * zc).astype(BF16)
    m = jnp.dot(y, wout_ref[...], preferred_element_type=F32)
    y_ref[...] = x + _rms(m, g1_ref[...])


def _conv_layer(x, g0, w_in, conv_w, w_out, g1, *, tm=ROW_TILE):
    b, s, d = x.shape
    const = lambda bi, i: (0, 0)
    return pl.pallas_call(
        functools.partial(_conv_kernel, tm=tm),
        grid=(b, s // tm),
        in_specs=[
            pl.BlockSpec((None, tm, d), lambda bi, i: (bi, i, 0)),
            pl.BlockSpec((1, d), const),
            pl.BlockSpec(w_in.shape, const),
            pl.BlockSpec(conv_w.shape, const),
            pl.BlockSpec(w_out.shape, const),
            pl.BlockSpec((1, d), const),
        ],
        out_specs=pl.BlockSpec((None, tm, d), lambda bi, i: (bi, i, 0)),
        out_shape=jax.ShapeDtypeStruct(x.shape, F32),
        scratch_shapes=[pltpu.VMEM((tm + HALO, d), F32)],
        compiler_params=pltpu.CompilerParams(
            dimension_semantics=("arbitrary", "arbitrary"),
            vmem_limit_bytes=VMEM_LIMIT_BYTES),
        name="conv_mixer",
    )(x, g0, w_in, conv_w, w_out, g1)


FF_CHUNK = 1024


def _mlp_kernel(x_ref, g2_ref, wup_ref, wdn_ref, g3_ref, y_ref):
    x = x_ref[...]
    h = _rms(x, g2_ref[...]).astype(BF16)
    d_ff = wup_ref.shape[1]
    f = None
    for c in range(d_ff // FF_CHUNK):
        cols = slice(c * FF_CHUNK, (c + 1) * FF_CHUNK)
        u = jnp.dot(h, wup_ref[:, cols], preferred_element_type=F32)
        a = jnp.square(jnp.maximum(u, 0.0)).astype(BF16)
        part = jnp.dot(a, wdn_ref[cols, :], preferred_element_type=F32)
        f = part if f is None else f + part
    y_ref[...] = x + _rms(f, g3_ref[...])


def _mlp(x, g2, w_up, w_down, g3, *, tm=ROW_TILE):
    b, s, d = x.shape
    const = lambda bi, i: (0, 0)
    return pl.pallas_call(
        _mlp_kernel,
        grid=(b, s // tm),
        in_specs=[
            pl.BlockSpec((None, tm, d), lambda bi, i: (bi, i, 0)),
            pl.BlockSpec((1, d), const),
            pl.BlockSpec(w_up.shape, const),
            pl.BlockSpec(w_down.shape, const),
            pl.BlockSpec((1, d), const),
        ],
        out_specs=pl.BlockSpec((None, tm, d), lambda bi, i: (bi, i, 0)),
        out_shape=jax.ShapeDtypeStruct(x.shape, F32),
        compiler_params=pltpu.CompilerParams(
            dimension_semantics=("arbitrary", "arbitrary"),
            vmem_limit_bytes=VMEM_LIMIT_BYTES),
        name="sq_relu_mlp",
    )(x, g2, w_up, w_down, g3)


def _ple_kernel(x_ref, p_ref, g4_ref, wg_ref, wp_ref, g5_ref, y_ref):
    x = x_ref[...]
    h = _rms(x, g4_ref[...]).astype(BF16)
    gate = jax.nn.sigmoid(jnp.dot(h, wg_ref[...], preferred_element_type=F32))
    e = jnp.dot(p_ref[...].astype(BF16), wp_ref[...], preferred_element_type=F32) * gate
    y_ref[...] = x + _rms(e, g5_ref[...])


def _ple(x, p, layer, g4, w_gate, w_proj, g5, *, tm=ROW_TILE):
    b, s, d = x.shape
    const = lambda bi, i: (0, 0)
    return pl.pallas_call(
        _ple_kernel,
        grid=(b, s // tm),
        in_specs=[
            pl.BlockSpec((None, tm, d), lambda bi, i: (bi, i, 0)),
            pl.BlockSpec((None, None, tm, p.shape[-1]), lambda bi, i: (layer, bi, i, 0)),
            pl.BlockSpec((1, d), const),
            pl.BlockSpec(w_gate.shape, const),
            pl.BlockSpec(w_proj.shape, const),
            pl.BlockSpec((1, d), const),
        ],
        out_specs=pl.BlockSpec((None, tm, d), lambda bi, i: (bi, i, 0)),
        out_shape=jax.ShapeDtypeStruct(x.shape, F32),
        compiler_params=pltpu.CompilerParams(
            dimension_semantics=("arbitrary", "arbitrary"),
            vmem_limit_bytes=VMEM_LIMIT_BYTES),
        name="ple_gate",
    )(x, p, g4, w_gate, w_proj, g5)


def _prep_attn_weights(w_in, b_f, w_out):
    d = D_MODEL
    wq_t = w_in[:, 0:d].T.astype(BF16)
    wk = w_in[:, d:2 * d].astype(BF16)
    wv_t = w_in[:, 2 * d:3 * d].T.astype(BF16)
    wf = w_in[:, 3 * d:]
    wf_t = jnp.pad(jnp.concatenate([wf, wf, wf], axis=1),
                   ((0, 0), (0, HEAD_PAD - 3 * N_HEADS))).T.astype(BF16)
    bf_col = jnp.pad(jnp.concatenate([b_f, b_f, b_f]), (0, HEAD_PAD - 3 * N_HEADS))
    bf_col = bf_col.reshape(HEAD_PAD, 1).astype(F32)
    return wq_t, wk, wv_t, wf_t, bf_col, w_out.T.astype(BF16)


def kernel(x, p, norm_g, w_attn_in, b_forget, w_attn_out, w_conv_in, conv_w, w_conv_out,
           w_mlp_up, w_mlp_down, w_ple_proj, w_ple_gate):
    depth = norm_g.shape[0]
    pk = jnp.asarray(_k_placement_matrix(), dtype=BF16)
    for i in range(depth):
        g = norm_g[i].astype(F32)
        gi = lambda n: g[n:n + 1]
        j = i // 2
        if i % 2 == 0:
            wq_t, wk, wv_t, wf_t, bf_col, w_out_t = _prep_attn_weights(
                w_attn_in[j], b_forget[j], w_attn_out[j])
            q_t, k, v_t = _attn_in(x, gi(0), wq_t, wk, wv_t, wf_t, bf_col, pk)
            o_t = _flash(q_t, k, v_t)
            x = _attn_out(o_t, x, w_out_t, gi(1))
        else:
            x = _conv_layer(x, gi(0), w_conv_in[j].astype(BF16), conv_w[j].astype(F32),
                            w_conv_out[j].astype(BF16), gi(1))
        x = _mlp(x, gi(2), w_mlp_up[i].astype(BF16), w_mlp_down[i].astype(BF16), gi(3))
        x = _ple(x, p, i, gi(4), w_ple_gate[i].astype(BF16), w_ple_proj[i].astype(BF16),
                 gi(5))
    return x
```

```python
import functools

import numpy as np
import jax
import jax.numpy as jnp
from jax import lax
from jax.experimental import pallas as pl
from jax.experimental.pallas import tpu as pltpu

F32 = jnp.float32
BF16 = jnp.bfloat16

D_MODEL = 1024
N_HEADS = 16
HEAD_DIM = D_MODEL // N_HEADS
HEAD_PAD = 128
QK_PAD = N_HEADS * HEAD_PAD
V_ROWS = 128
V_ONES_ROW = HEAD_DIM
CONV_WIDTH = 3
RMS_EPS = 1e-6
NEG_INF = -1e30
LOG2E = 1.4426950408889634
Q_SCALE = (HEAD_DIM ** -0.5) * LOG2E

VMEM_LIMIT_BYTES = 56 * 1024 * 1024

ROW_TILE = 512
V_BLOCK = ROW_TILE
ATTN_BLOCK = 1024
QUERY_TILE = 256
SCORE_LOOKAHEAD = 4

SLOT_CQ = (64, 65, 66)
SLOT_CK = (67, 68, 69)
ROW_ONES = 48


def _rms(xf, g):
    ms = jnp.mean(xf * xf, axis=-1, keepdims=True)
    return xf * lax.rsqrt(ms + RMS_EPS) * g


def _split3(x):
    hi = x.astype(BF16).astype(F32)
    r = x - hi
    mid = r.astype(BF16).astype(F32)
    lo = (r - mid).astype(BF16).astype(F32)
    return hi, mid, lo


def _k_placement_matrix():
    pk = np.zeros((HEAD_PAD, QK_PAD), np.float32)
    for h in range(N_HEADS):
        base = h * HEAD_PAD
        for part in range(3):
            pk[part * N_HEADS + h, base + SLOT_CK[part]] = -1.0
            pk[ROW_ONES, base + SLOT_CQ[part]] = 1.0
    return pk


def _attn_in_kernel(x_ref, g_ref, wq_ref, wk_ref, wv_ref, wf_ref, bf_ref, pk_ref,
                    q_ref, k_ref, v_ref, carry_ref, *, tm):
    @pl.when(pl.program_id(1) == 0)
    def _():
        carry_ref[...] = jnp.zeros_like(carry_ref)

    hn = _rms(x_ref[...], g_ref[...])
    h = hn.astype(BF16)
    ht = hn.T.astype(BF16)

    f = jnp.dot(wf_ref[...], ht, preferred_element_type=F32) + bf_ref[...]
    logf = (jnp.minimum(f, 0.0) - jnp.log1p(jnp.exp(-jnp.abs(f)))) * LOG2E

    row = lax.broadcasted_iota(jnp.int32, (tm, tm), 0)
    col = lax.broadcasted_iota(jnp.int32, (tm, tm), 1)
    triu = (row <= col).astype(BF16)
    c = carry_ref[...]
    for part in _split3(logf):
        c = c + jnp.dot(part.astype(BF16), triu, preferred_element_type=F32)
    carry_ref[...] = c[:, tm - 1:tm]

    chi, cmid, clo = _split3(c)

    sub = lax.broadcasted_iota(jnp.int32, (HEAD_PAD, tm), 0)
    bias_t = jnp.where(sub < N_HEADS, chi,
                       jnp.where(sub < 2 * N_HEADS, cmid,
                                 jnp.where(sub < 3 * N_HEADS, clo,
                                           jnp.where(sub == ROW_ONES, 1.0, 0.0))))
    bias = bias_t.T.astype(BF16)
    kk = jnp.dot(h, wk_ref[...], preferred_element_type=F32)
    kb = jnp.dot(bias, pk_ref[...], preferred_element_type=F32)
    low = lax.broadcasted_iota(jnp.int32, (tm, HEAD_PAD), 1) < HEAD_DIM
    for pr in range(N_HEADS // 2):
        pair = kk[:, pr * HEAD_PAD:(pr + 1) * HEAD_PAD]
        for odd in range(2):
            hh = 2 * pr + odd
            feats = pltpu.roll(pair, HEAD_DIM, axis=1) if odd else pair
            k_ref[hh] = (jnp.where(low, feats, 0.0)
                         + kb[:, hh * HEAD_PAD:(hh + 1) * HEAD_PAD]).astype(BF16)

    qt = jnp.dot(wq_ref[...], ht, preferred_element_type=F32) * Q_SCALE
    slot = lax.broadcasted_iota(jnp.int32, (HEAD_PAD - HEAD_DIM, tm), 0)
    for hh in range(N_HEADS):
        bias_rows = jnp.where(slot == 0, chi[hh:hh + 1],
                              jnp.where(slot == 1, cmid[hh:hh + 1],
                                        jnp.where(slot == 2, clo[hh:hh + 1],
                                                  jnp.where(slot < 6, 1.0, 0.0))))
        q_ref[hh] = jnp.concatenate(
            [qt[hh * HEAD_DIM:(hh + 1) * HEAD_DIM], bias_rows], axis=0).astype(BF16)

    vt = jnp.dot(wv_ref[...], ht, preferred_element_type=F32)
    pad_rows = jnp.where(
        lax.broadcasted_iota(jnp.int32, (V_ROWS - HEAD_DIM, tm), 0) == 0, 1.0, 0.0)
    for hh in range(N_HEADS):
        v_ref[hh] = jnp.concatenate(
            [vt[hh * HEAD_DIM:(hh + 1) * HEAD_DIM], pad_rows], axis=0).astype(BF16)


def _attn_in(x, g, wq_t, wk, wv_t, wf_t, bf_col, pk, *, tm=ROW_TILE):
    b, s, d = x.shape
    const = lambda bi, i: (0, 0)
    return pl.pallas_call(
        functools.partial(_attn_in_kernel, tm=tm),
        grid=(b, s // tm),
        in_specs=[
            pl.BlockSpec((None, tm, d), lambda bi, i: (bi, i, 0)),
            pl.BlockSpec((1, d), const),
            pl.BlockSpec(wq_t.shape, const),
            pl.BlockSpec(wk.shape, const),
            pl.BlockSpec(wv_t.shape, const),
            pl.BlockSpec(wf_t.shape, const),
            pl.BlockSpec(bf_col.shape, const),
            pl.BlockSpec(pk.shape, const),
        ],
        out_specs=[
            pl.BlockSpec((None, N_HEADS, HEAD_PAD, tm), lambda bi, i: (bi, 0, 0, i)),
            pl.BlockSpec((None, N_HEADS, tm, HEAD_PAD), lambda bi, i: (bi, 0, i, 0)),
            pl.BlockSpec((None, N_HEADS, None, V_ROWS, tm), lambda bi, i: (bi, 0, i, 0, 0)),
        ],
        out_shape=[
            jax.ShapeDtypeStruct((b, N_HEADS, HEAD_PAD, s), BF16),
            jax.ShapeDtypeStruct((b, N_HEADS, s, HEAD_PAD), BF16),
            jax.ShapeDtypeStruct((b, N_HEADS, s // tm, V_ROWS, tm), BF16),
        ],
        scratch_shapes=[pltpu.VMEM((HEAD_PAD, 1), F32)],
        compiler_params=pltpu.CompilerParams(
            dimension_semantics=("arbitrary", "arbitrary"),
            vmem_limit_bytes=VMEM_LIMIT_BYTES),
        name="attn_in",
    )(x, g, wq_t, wk, wv_t, wf_t, bf_col, pk)


def _flash_kernel(q_ref, k_ref, v_ref, o_ref, s_refs, smax_refs, m_ref, acc_ref, *, bk):
    i = pl.program_id(2)
    bq = 2 * bk
    sub = bk // V_BLOCK
    n_tiles = bq // QUERY_TILE
    m_ref[...] = jnp.full_like(m_ref, -jnp.inf)
    acc_ref[...] = jnp.zeros_like(acc_ref)

    def scores_tile(t, slot, c, rows=bk):
        cols = slice(c * QUERY_TILE, (c + 1) * QUERY_TILE)
        start = pl.multiple_of(t * bk, bk)
        s = jnp.dot(k_ref[pl.ds(start, rows), :], q_ref[:, cols],
                    preferred_element_type=F32)
        s_refs[slot][0:rows, cols] = s
        smax_refs[slot][:, cols] = jnp.max(s, axis=0, keepdims=True)

    def consume_tile(t, slot, c, rows=bk, key_offset=None):
        cols = slice(c * QUERY_TILE, (c + 1) * QUERY_TILE)
        s = s_refs[slot][0:rows, cols]
        s_max = smax_refs[slot][:, cols]
        if key_offset is not None:
            key = lax.broadcasted_iota(jnp.int32, (rows, QUERY_TILE), 0) + key_offset
            qry = lax.broadcasted_iota(jnp.int32, (rows, QUERY_TILE), 1) + c * QUERY_TILE
            s = jnp.where(key <= qry, s, NEG_INF)
            s_max = jnp.max(s, axis=0, keepdims=True)
        m_prev = m_ref[:, cols]
        m_new = jnp.maximum(m_prev, s_max)
        alpha = jnp.exp2(m_prev - m_new)
        p = jnp.exp2(s - m_new).astype(BF16)
        pv = None
        for u in range(pl.cdiv(rows, V_BLOCK)):
            n = min(V_BLOCK, rows - u * V_BLOCK)
            part = jnp.dot(v_ref[t * sub + u, :, 0:n], p[u * V_BLOCK:u * V_BLOCK + n, :],
                           preferred_element_type=F32)
            pv = part if pv is None else pv + part
        acc_ref[:, cols] = alpha * acc_ref[:, cols] + pv
        m_ref[:, cols] = m_new

    diag = []
    for r in range(2):
        for c in range(n_tiles):
            rows = min(bk, (c + 1) * QUERY_TILE - r * bk)
            if rows > 0:
                diag.append((r, c, rows, rows < bk or c * QUERY_TILE < (r + 1) * bk))

    la = SCORE_LOOKAHEAD

    @pl.when(i == 0)
    def _():
        for r, c, rows, _ in diag[:la]:
            scores_tile(r, r, c, rows)

    @pl.when(i > 0)
    def _():
        for c in range(la):
            scores_tile(0, 0, c)

    def body(g, carry):
        t = 2 * g
        for half in range(2):
            for c in range(n_tiles):
                consume_tile(t + half, half, c)
                ahead = half + (c + la) // n_tiles
                scores_tile(t + ahead, ahead % 2, (c + la) % n_tiles)
        return carry

    lax.fori_loop(0, i, body, 0)
    t = 2 * i
    for idx, (r, c, rows, masked) in enumerate(diag):
        consume_tile(t + r, r, c, rows, key_offset=r * bk if masked else None)
        if idx + la < len(diag):
            r2, c2, rows2, _ = diag[idx + la]
            scores_tile(t + r2, r2, c2, rows2)

    acc = acc_ref[...]
    o_ref[...] = (acc[0:HEAD_DIM] / acc[V_ONES_ROW:V_ONES_ROW + 1]).astype(o_ref.dtype)


def _flash(q_t, k, v_t, *, bk=ATTN_BLOCK):
    b, nh, s, hp = k.shape
    bq = 2 * bk
    assert v_t.shape == (b, nh, s // V_BLOCK, V_ROWS, V_BLOCK)
    score_bufs = [pltpu.VMEM((bk, bq), F32) for _ in range(2)]
    smax_bufs = [pltpu.VMEM((1, bq), F32) for _ in range(2)]
    return pl.pallas_call(
        functools.partial(_flash_kernel, bk=bk),
        grid=(b, nh, s // bq),
        in_specs=[
            pl.BlockSpec((None, None, hp, bq), lambda bi, h, i: (bi, h, 0, i)),
            pl.BlockSpec((None, None, s, hp), lambda bi, h, i: (bi, h, 0, 0)),
            pl.BlockSpec((None, None, s // V_BLOCK, V_ROWS, V_BLOCK),
                         lambda bi, h, i: (bi, h, 0, 0, 0)),
        ],
        out_specs=pl.BlockSpec((None, None, HEAD_DIM, bq), lambda bi, h, i: (bi, h, 0, i)),
        out_shape=jax.ShapeDtypeStruct((b, nh, HEAD_DIM, s), BF16),
        scratch_shapes=[score_bufs, smax_bufs,
                        pltpu.VMEM((1, bq), F32), pltpu.VMEM((V_ROWS, bq), F32)],
        compiler_params=pltpu.CompilerParams(
            dimension_semantics=("arbitrary", "arbitrary", "arbitrary"),
            vmem_limit_bytes=VMEM_LIMIT_BYTES),
        name="fox_flash",
    )(q_t, k, v_t)


def _attn_out_kernel(o_ref, x_ref, w_ref, g_ref, y_ref):
    for r0 in range(0, x_ref.shape[0], ROW_TILE):
        rows = slice(r0, r0 + ROW_TILE)
        o_t = jnp.concatenate([o_ref[hh, :, rows] for hh in range(N_HEADS)], axis=0)
        m_t = jnp.dot(w_ref[...], o_t, preferred_element_type=F32)
        y_ref[rows, :] = x_ref[rows, :] + _rms(m_t.T, g_ref[...])


def _attn_out(o_t, x, w_out_t, g, *, tm=2 * ROW_TILE):
    b, s, d = x.shape
    const = lambda bi, i: (0, 0)
    return pl.pallas_call(
        _attn_out_kernel,
        grid=(b, s // tm),
        in_specs=[
            pl.BlockSpec((None, N_HEADS, HEAD_DIM, tm), lambda bi, i: (bi, 0, 0, i)),
            pl.BlockSpec((None, tm, d), lambda bi, i: (bi, i, 0)),
            pl.BlockSpec(w_out_t.shape, const),
            pl.BlockSpec((1, d), const),
        ],
        out_specs=pl.BlockSpec((None, tm, d), lambda bi, i: (bi, i, 0)),
        out_shape=jax.ShapeDtypeStruct(x.shape, F32),
        compiler_params=pltpu.CompilerParams(
            dimension_semantics=("arbitrary", "arbitrary"),
            vmem_limit_bytes=VMEM_LIMIT_BYTES),
        name="attn_out",
    )(o_t, x, w_out_t, g)


HALO = 8


def _conv_kernel(x_ref, g0_ref, win_ref, cw_ref, wout_ref, g1_ref, y_ref, z_ref, *, tm):
    d = D_MODEL

    @pl.when(pl.program_id(1) == 0)
    def _():
        z_ref[0:HALO, :] = jnp.zeros((HALO, d), F32)

    @pl.when(pl.program_id(1) > 0)
    def _():
        z_ref[0:HALO, :] = z_ref[tm:tm + HALO, :]

    x = x_ref[...]
    h = _rms(x, g0_ref[...]).astype(BF16)
    c_gate = jnp.dot(h, win_ref[:, d:2 * d], preferred_element_type=F32)
    u = jnp.dot(h, win_ref[:, 2 * d:3 * d], preferred_element_type=F32)
    z = c_gate * u
    z_ref[HALO:HALO + tm, :] = z
    cw = cw_ref[...]
    zc = (cw[2:3, :] * z
          + cw[1:2, :] * z_ref[HALO - 1:HALO - 1 + tm, :]
          + cw[0:1, :] * z_ref[HALO - 2:HALO - 2 + tm, :])
    b_gate = jnp.dot(h, win_ref[:, 0:d], preferred_element_type=F32)
    y = (b_gate * zc).astype(BF16)
    m = jnp.dot(y, wout_ref[...], preferred_element_type=F32)
    y_ref[...] = x + _rms(m, g1_ref[...])


def _conv_layer(x, g0, w_in, conv_w, w_out, g1, *, tm=ROW_TILE):
    b, s, d = x.shape
    const = lambda bi, i: (0, 0)
    return pl.pallas_call(
        functools.partial(_conv_kernel, tm=tm),
        grid=(b, s // tm),
        in_specs=[
            pl.BlockSpec((None, tm, d), lambda bi, i: (bi, i, 0)),
            pl.BlockSpec((1, d), const),
            pl.BlockSpec(w_in.shape, const),
            pl.BlockSpec(conv_w.shape, const),
            pl.BlockSpec(w_out.shape, const),
            pl.BlockSpec((1, d), const),
        ],
        out_specs=pl.BlockSpec((None, tm, d), lambda bi, i: (bi, i, 0)),
        out_shape=jax.ShapeDtypeStruct(x.shape, F32),
        scratch_shapes=[pltpu.VMEM((tm + HALO, d), F32)],
        compiler_params=pltpu.CompilerParams(
            dimension_semantics=("arbitrary", "arbitrary"),
            vmem_limit_bytes=VMEM_LIMIT_BYTES),
        name="conv_mixer",
    )(x, g0, w_in, conv_w, w_out, g1)


FF_CHUNK = 1024


def _mlp_kernel(x_ref, g2_ref, wup_ref, wdn_ref, g3_ref, y_ref):
    x = x_ref[...]
    h = _rms(x, g2_ref[...]).astype(BF16)
    d_ff = wup_ref.shape[1]
    f = None
    for c in range(d_ff // FF_CHUNK):
        cols = slice(c * FF_CHUNK, (c + 1) * FF_CHUNK)
        u = jnp.dot(h, wup_ref[:, cols], preferred_element_type=F32)
        a = jnp.square(jnp.maximum(u, 0.0)).astype(BF16)
        part = jnp.dot(a, wdn_ref[cols, :], preferred_element_type=F32)
        f = part if f is None else f + part
    y_ref[...] = x + _rms(f, g3_ref[...])


def _mlp(x, g2, w_up, w_down, g3, *, tm=ROW_TILE):
    b, s, d = x.shape
    const = lambda bi, i: (0, 0)
    return pl.pallas_call(
        _mlp_kernel,
        grid=(b, s // tm),
        in_specs=[
            pl.BlockSpec((None, tm, d), lambda bi, i: (bi, i, 0)),
            pl.BlockSpec((1, d), const),
            pl.BlockSpec(w_up.shape, const),
            pl.BlockSpec(w_down.shape, const),
            pl.BlockSpec((1, d), const),
        ],
        out_specs=pl.BlockSpec((None, tm, d), lambda bi, i: (bi, i, 0)),
        out_shape=jax.ShapeDtypeStruct(x.shape, F32),
        compiler_params=pltpu.CompilerParams(
            dimension_semantics=("arbitrary", "arbitrary"),
            vmem_limit_bytes=VMEM_LIMIT_BYTES),
        name="sq_relu_mlp",
    )(x, g2, w_up, w_down, g3)


def _ple_kernel(x_ref, p_ref, g4_ref, wg_ref, wp_ref, g5_ref, y_ref):
    for r0 in range(0, x_ref.shape[0], ROW_TILE):
        rows = slice(r0, r0 + ROW_TILE)
        x = x_ref[rows, :]
        h = _rms(x, g4_ref[...]).astype(BF16)
        gate = jax.nn.sigmoid(jnp.dot(h, wg_ref[...], preferred_element_type=F32))
        e = jnp.dot(p_ref[rows, :].astype(BF16), wp_ref[...],
                    preferred_element_type=F32) * gate
        y_ref[rows, :] = x + _rms(e, g5_ref[...])


def _ple(x, p, layer, g4, w_gate, w_proj, g5, *, tm=2 * ROW_TILE):
    b, s, d = x.shape
    const = lambda bi, i: (0, 0)
    return pl.pallas_call(
        _ple_kernel,
        grid=(b, s // tm),
        in_specs=[
            pl.BlockSpec((None, tm, d), lambda bi, i: (bi, i, 0)),
            pl.BlockSpec((None, None, tm, p.shape[-1]), lambda bi, i: (layer, bi, i, 0)),
            pl.BlockSpec((1, d), const),
            pl.BlockSpec(w_gate.shape, const),
            pl.BlockSpec(w_proj.shape, const),
            pl.BlockSpec((1, d), const),
        ],
        out_specs=pl.BlockSpec((None, tm, d), lambda bi, i: (bi, i, 0)),
        out_shape=jax.ShapeDtypeStruct(x.shape, F32),
        compiler_params=pltpu.CompilerParams(
            dimension_semantics=("arbitrary", "arbitrary"),
            vmem_limit_bytes=VMEM_LIMIT_BYTES),
        name="ple_gate",
    )(x, p, g4, w_gate, w_proj, g5)


def _prep_attn_weights(w_in, b_f, w_out):
    d = D_MODEL
    wq_t = w_in[:, 0:d].T.astype(BF16)
    wk = w_in[:, d:2 * d].astype(BF16)
    wv_t = w_in[:, 2 * d:3 * d].T.astype(BF16)
    wf = w_in[:, 3 * d:]
    wf_t = jnp.pad(jnp.concatenate([wf, wf, wf], axis=1),
                   ((0, 0), (0, HEAD_PAD - 3 * N_HEADS))).T.astype(BF16)
    bf_col = jnp.pad(jnp.concatenate([b_f, b_f, b_f]), (0, HEAD_PAD - 3 * N_HEADS))
    bf_col = bf_col.reshape(HEAD_PAD, 1).astype(F32)
    return wq_t, wk, wv_t, wf_t, bf_col, w_out.T.astype(BF16)


def kernel(x, p, norm_g, w_attn_in, b_forget, w_attn_out, w_conv_in, conv_w, w_conv_out,
           w_mlp_up, w_mlp_down, w_ple_proj, w_ple_gate):
    depth = norm_g.shape[0]
    pk = jnp.asarray(_k_placement_matrix(), dtype=BF16)
    for i in range(depth):
        g = norm_g[i].astype(F32)
        gi = lambda n: g[n:n + 1]
        j = i // 2
        if i % 2 == 0:
            wq_t, wk, wv_t, wf_t, bf_col, w_out_t = _prep_attn_weights(
                w_attn_in[j], b_forget[j], w_attn_out[j])
            q_t, k, v_t = _attn_in(x, gi(0), wq_t, wk, wv_t, wf_t, bf_col, pk)
            o_t = _flash(q_t, k, v_t)
            x = _attn_out(o_t, x, w_out_t, gi(1))
        else:
            x = _conv_layer(x, gi(0), w_conv_in[j].astype(BF16), conv_w[j].astype(F32),
                            w_conv_out[j].astype(BF16), gi(1))
        x = _mlp(x, gi(2), w_mlp_up[i].astype(BF16), w_mlp_down[i].astype(BF16), gi(3))
        x = _ple(x, p, i, gi(4), w_ple_gate[i].astype(BF16), w_ple_proj[i].astype(BF16),
                 gi(5))
    return x
```

```python
import functools

import numpy as np
import jax
import jax.numpy as jnp
from jax import lax
from jax.experimental import pallas as pl
from jax.experimental.pallas import tpu as pltpu

F32 = jnp.float32
BF16 = jnp.bfloat16

D_MODEL = 1024
N_HEADS = 16
HEAD_DIM = D_MODEL // N_HEADS
HEAD_PAD = 128
QK_PAD = N_HEADS * HEAD_PAD
V_ROWS = 128
V_ONES_ROW = HEAD_DIM
CONV_WIDTH = 3
RMS_EPS = 1e-6
NEG_INF = -1e30
LOG2E = 1.4426950408889634
Q_SCALE = (HEAD_DIM ** -0.5) * LOG2E

VMEM_LIMIT_BYTES = 56 * 1024 * 1024

ROW_TILE = 512
V_BLOCK = ROW_TILE
ATTN_BLOCK = 1024
QUERY_TILE = 256
SCORE_LOOKAHEAD = 4

SLOT_CQ = (64, 65, 66)
SLOT_CK = (67, 68, 69)
ROW_ONES = 48


def _rms(xf, g):
    ms = jnp.mean(xf * xf, axis=-1, keepdims=True)
    return xf * lax.rsqrt(ms + RMS_EPS) * g


def _split3(x):
    hi = x.astype(BF16).astype(F32)
    r = x - hi
    mid = r.astype(BF16).astype(F32)
    lo = (r - mid).astype(BF16).astype(F32)
    return hi, mid, lo


def _k_placement_matrix():
    pk = np.zeros((HEAD_PAD, QK_PAD), np.float32)
    for h in range(N_HEADS):
        base = h * HEAD_PAD
        for part in range(3):
            pk[part * N_HEADS + h, base + SLOT_CK[part]] = -1.0
            pk[ROW_ONES, base + SLOT_CQ[part]] = 1.0
    return pk


def _attn_in_kernel(x_ref, g_ref, wq_ref, wk_ref, wv_ref, wf_ref, bf_ref, pk_ref,
                    q_ref, k_ref, v_ref, carry_ref, *, tm):
    @pl.when(pl.program_id(1) == 0)
    def _():
        carry_ref[...] = jnp.zeros_like(carry_ref)

    hn = _rms(x_ref[...], g_ref[...])
    h = hn.astype(BF16)
    ht = hn.T.astype(BF16)

    f = jnp.dot(wf_ref[...], ht, preferred_element_type=F32) + bf_ref[...]
    logf = (jnp.minimum(f, 0.0) - jnp.log1p(jnp.exp(-jnp.abs(f)))) * LOG2E

    row = lax.broadcasted_iota(jnp.int32, (tm, tm), 0)
    col = lax.broadcasted_iota(jnp.int32, (tm, tm), 1)
    triu = (row <= col).astype(BF16)
    c = carry_ref[...]
    for part in _split3(logf):
        c = c + jnp.dot(part.astype(BF16), triu, preferred_element_type=F32)
    carry_ref[...] = c[:, tm - 1:tm]

    chi, cmid, clo = _split3(c)

    sub = lax.broadcasted_iota(jnp.int32, (HEAD_PAD, tm), 0)
    bias_t = jnp.where(sub < N_HEADS, chi,
                       jnp.where(sub < 2 * N_HEADS, cmid,
                                 jnp.where(sub < 3 * N_HEADS, clo,
                                           jnp.where(sub == ROW_ONES, 1.0, 0.0))))
    bias = bias_t.T.astype(BF16)
    kk = jnp.dot(h, wk_ref[...], preferred_element_type=F32)
    kb = jnp.dot(bias, pk_ref[...], preferred_element_type=F32)
    low = lax.broadcasted_iota(jnp.int32, (tm, HEAD_PAD), 1) < HEAD_DIM
    for pr in range(N_HEADS // 2):
        pair = kk[:, pr * HEAD_PAD:(pr + 1) * HEAD_PAD]
        for odd in range(2):
            hh = 2 * pr + odd
            feats = pltpu.roll(pair, HEAD_DIM, axis=1) if odd else pair
            k_ref[hh] = (jnp.where(low, feats, 0.0)
                         + kb[:, hh * HEAD_PAD:(hh + 1) * HEAD_PAD]).astype(BF16)

    qt = jnp.dot(wq_ref[...], ht, preferred_element_type=F32) * Q_SCALE
    slot = lax.broadcasted_iota(jnp.int32, (HEAD_PAD - HEAD_DIM, tm), 0)
    for hh in range(N_HEADS):
        bias_rows = jnp.where(slot == 0, chi[hh:hh + 1],
                              jnp.where(slot == 1, cmid[hh:hh + 1],
                                        jnp.where(slot == 2, clo[hh:hh + 1],
                                                  jnp.where(slot < 6, 1.0, 0.0))))
        q_ref[hh] = jnp.concatenate(
            [qt[hh * HEAD_DIM:(hh + 1) * HEAD_DIM], bias_rows], axis=0).astype(BF16)

    vt = jnp.dot(wv_ref[...], ht, preferred_element_type=F32)
    pad_rows = jnp.where(
        lax.broadcasted_iota(jnp.int32, (V_ROWS - HEAD_DIM, tm), 0) == 0, 1.0, 0.0)
    for hh in range(N_HEADS):
        v_ref[hh] = jnp.concatenate(
            [vt[hh * HEAD_DIM:(hh + 1) * HEAD_DIM], pad_rows], axis=0).astype(BF16)


def _attn_in(x, g, wq_t, wk, wv_t, wf_t, bf_col, pk, *, tm=ROW_TILE):
    b, s, d = x.shape
    const = lambda bi, i: (0, 0)
    return pl.pallas_call(
        functools.partial(_attn_in_kernel, tm=tm),
        grid=(b, s // tm),
        in_specs=[
            pl.BlockSpec((None, tm, d), lambda bi, i: (bi, i, 0)),
            pl.BlockSpec((1, d), const),
            pl.BlockSpec(wq_t.shape, const),
            pl.BlockSpec(wk.shape, const),
            pl.BlockSpec(wv_t.shape, const),
            pl.BlockSpec(wf_t.shape, const),
            pl.BlockSpec(bf_col.shape, const),
            pl.BlockSpec(pk.shape, const),
        ],
        out_specs=[
            pl.BlockSpec((None, N_HEADS, HEAD_PAD, tm), lambda bi, i: (bi, 0, 0, i)),
            pl.BlockSpec((None, N_HEADS, tm, HEAD_PAD), lambda bi, i: (bi, 0, i, 0)),
            pl.BlockSpec((None, N_HEADS, None, V_ROWS, tm), lambda bi, i: (bi, 0, i, 0, 0)),
        ],
        out_shape=[
            jax.ShapeDtypeStruct((b, N_HEADS, HEAD_PAD, s), BF16),
            jax.ShapeDtypeStruct((b, N_HEADS, s, HEAD_PAD), BF16),
            jax.ShapeDtypeStruct((b, N_HEADS, s // tm, V_ROWS, tm), BF16),
        ],
        scratch_shapes=[pltpu.VMEM((HEAD_PAD, 1), F32)],
        compiler_params=pltpu.CompilerParams(
            dimension_semantics=("arbitrary", "arbitrary"),
            vmem_limit_bytes=VMEM_LIMIT_BYTES),
        name="attn_in",
    )(x, g, wq_t, wk, wv_t, wf_t, bf_col, pk)


def _flash_kernel(q_ref, qnext_ref, k_ref, v_ref, o_ref, s_refs, smax_refs, m_ref, acc_ref,
                  *, bk):
    i = pl.program_id(2)
    bq = 2 * bk
    sub = bk // V_BLOCK
    n_tiles = bq // QUERY_TILE
    m_ref[...] = jnp.full_like(m_ref, -jnp.inf)
    acc_ref[...] = jnp.zeros_like(acc_ref)

    def scores_tile(t, slot, c, rows=bk, queries=q_ref):
        cols = slice(c * QUERY_TILE, (c + 1) * QUERY_TILE)
        start = pl.multiple_of(t * bk, bk)
        s = jnp.dot(k_ref[pl.ds(start, rows), :], queries[:, cols],
                    preferred_element_type=F32)
        s_refs[slot][0:rows, cols] = s
        smax_refs[slot][:, cols] = jnp.max(s, axis=0, keepdims=True)

    def consume_tile(t, slot, c, rows=bk, key_offset=None):
        cols = slice(c * QUERY_TILE, (c + 1) * QUERY_TILE)
        s = s_refs[slot][0:rows, cols]
        s_max = smax_refs[slot][:, cols]
        if key_offset is not None:
            key = lax.broadcasted_iota(jnp.int32, (rows, QUERY_TILE), 0) + key_offset
            qry = lax.broadcasted_iota(jnp.int32, (rows, QUERY_TILE), 1) + c * QUERY_TILE
            s = jnp.where(key <= qry, s, NEG_INF)
            s_max = jnp.max(s, axis=0, keepdims=True)
        m_prev = m_ref[:, cols]
        m_new = jnp.maximum(m_prev, s_max)
        alpha = jnp.exp2(m_prev - m_new)
        p = jnp.exp2(s - m_new).astype(BF16)
        pv = None
        for u in range(pl.cdiv(rows, V_BLOCK)):
            n = min(V_BLOCK, rows - u * V_BLOCK)
            part = jnp.dot(v_ref[t * sub + u, :, 0:n], p[u * V_BLOCK:u * V_BLOCK + n, :],
                           preferred_element_type=F32)
            pv = part if pv is None else pv + part
        acc_ref[:, cols] = alpha * acc_ref[:, cols] + pv
        m_ref[:, cols] = m_new

    diag = []
    for r in range(2):
        for c in range(n_tiles):
            rows = min(bk, (c + 1) * QUERY_TILE - r * bk)
            if rows > 0:
                diag.append((r, c, rows, rows < bk or c * QUERY_TILE < (r + 1) * bk))

    la = SCORE_LOOKAHEAD

    @pl.when(i == 0)
    def _():
        for r, c, rows, _ in diag[:la]:
            scores_tile(r, r, c, rows)

    def body(g, carry):
        t = 2 * g
        for half in range(2):
            for c in range(n_tiles):
                consume_tile(t + half, half, c)
                ahead = half + (c + la) // n_tiles
                scores_tile(t + ahead, ahead % 2, (c + la) % n_tiles)
        return carry

    lax.fori_loop(0, i, body, 0)
    t = 2 * i
    for idx, (r, c, rows, masked) in enumerate(diag):
        consume_tile(t + r, r, c, rows, key_offset=r * bk if masked else None)
        if idx + la < len(diag):
            r2, c2, rows2, _ = diag[idx + la]
            scores_tile(t + r2, r2, c2, rows2)
        else:
            scores_tile(0, 0, idx + la - len(diag), queries=qnext_ref)

    acc = acc_ref[...]
    o_ref[...] = (acc[0:HEAD_DIM] / acc[V_ONES_ROW:V_ONES_ROW + 1]).astype(o_ref.dtype)


def _flash(q_t, k, v_t, *, bk=ATTN_BLOCK):
    b, nh, s, hp = k.shape
    bq = 2 * bk
    assert v_t.shape == (b, nh, s // V_BLOCK, V_ROWS, V_BLOCK)
    score_bufs = [pltpu.VMEM((bk, bq), F32) for _ in range(2)]
    smax_bufs = [pltpu.VMEM((1, bq), F32) for _ in range(2)]
    n_steps = s // bq
    return pl.pallas_call(
        functools.partial(_flash_kernel, bk=bk),
        grid=(b, nh, s // bq),
        in_specs=[
            pl.BlockSpec((None, None, hp, bq), lambda bi, h, i: (bi, h, 0, i)),
            pl.BlockSpec((None, None, hp, bq),
                         lambda bi, h, i: (bi, h, 0, jnp.minimum(i + 1, n_steps - 1))),
            pl.BlockSpec((None, None, s, hp), lambda bi, h, i: (bi, h, 0, 0)),
            pl.BlockSpec((None, None, s // V_BLOCK, V_ROWS, V_BLOCK),
                         lambda bi, h, i: (bi, h, 0, 0, 0)),
        ],
        out_specs=pl.BlockSpec((None, None, HEAD_DIM, bq), lambda bi, h, i: (bi, h, 0, i)),
        out_shape=jax.ShapeDtypeStruct((b, nh, HEAD_DIM, s), BF16),
        scratch_shapes=[score_bufs, smax_bufs,
                        pltpu.VMEM((1, bq), F32), pltpu.VMEM((V_ROWS, bq), F32)],
        compiler_params=pltpu.CompilerParams(
            dimension_semantics=("arbitrary", "arbitrary", "arbitrary"),
            vmem_limit_bytes=VMEM_LIMIT_BYTES),
        name="fox_flash",
    )(q_t, q_t, k, v_t)


def _attn_out_kernel(o_ref, x_ref, w_ref, g_ref, y_ref):
    for r0 in range(0, x_ref.shape[0], ROW_TILE):
        rows = slice(r0, r0 + ROW_TILE)
        o_t = jnp.concatenate([o_ref[hh, :, rows] for hh in range(N_HEADS)], axis=0)
        m_t = jnp.dot(w_ref[...], o_t, preferred_element_type=F32)
        y_ref[rows, :] = x_ref[rows, :] + _rms(m_t.T, g_ref[...])


def _attn_out(o_t, x, w_out_t, g, *, tm=2 * ROW_TILE):
    b, s, d = x.shape
    const = lambda bi, i: (0, 0)
    return pl.pallas_call(
        _attn_out_kernel,
        grid=(b, s // tm),
        in_specs=[
            pl.BlockSpec((None, N_HEADS, HEAD_DIM, tm), lambda bi, i: (bi, 0, 0, i)),
            pl.BlockSpec((None, tm, d), lambda bi, i: (bi, i, 0)),
            pl.BlockSpec(w_out_t.shape, const),
            pl.BlockSpec((1, d), const),
        ],
        out_specs=pl.BlockSpec((None, tm, d), lambda bi, i: (bi, i, 0)),
        out_shape=jax.ShapeDtypeStruct(x.shape, F32),
        compiler_params=pltpu.CompilerParams(
            dimension_semantics=("arbitrary", "arbitrary"),
            vmem_limit_bytes=VMEM_LIMIT_BYTES),
        name="attn_out",
    )(o_t, x, w_out_t, g)


HALO = 8


def _conv_kernel(x_ref, g0_ref, win_ref, cw_ref, wout_ref, g1_ref, y_ref, z_ref, *, tm):
    d = D_MODEL

    @pl.when(pl.program_id(1) == 0)
    def _():
        z_ref[0:HALO, :] = jnp.zeros((HALO, d), F32)

    @pl.when(pl.program_id(1) > 0)
    def _():
        z_ref[0:HALO, :] = z_ref[tm:tm + HALO, :]

    x = x_ref[...]
    h = _rms(x, g0_ref[...]).astype(BF16)
    c_gate = jnp.dot(h, win_ref[:, d:2 * d], preferred_element_type=F32)
    u = jnp.dot(h, win_ref[:, 2 * d:3 * d], preferred_element_type=F32)
    z = c_gate * u
    z_ref[HALO:HALO + tm, :] = z
    cw = cw_ref[...]
    zc = (cw[2:3, :] * z
          + cw[1:2, :] * z_ref[HALO - 1:HALO - 1 + tm, :]
          + cw[0:1, :] * z_ref[HALO - 2:HALO - 2 + tm, :])
    b_gate = jnp.dot(h, win_ref[:, 0:d], preferred_element_type=F32)
    y = (b_gate * zc).astype(BF16)
    m = jnp.dot(y, wout_ref[...], preferred_element_type=F32)
    y_ref[...] = x + _rms(m, g1_ref[...])


def _conv_layer(x, g0, w_in, conv_w, w_out, g1, *, tm=ROW_TILE):
    b, s, d = x.shape
    const = lambda bi, i: (0, 0)
    return pl.pallas_call(
        functools.partial(_conv_kernel, tm=tm),
        grid=(b, s // tm),
        in_specs=[
            pl.BlockSpec((None, tm, d), lambda bi, i: (bi, i, 0)),
            pl.BlockSpec((1, d), const),
            pl.BlockSpec(w_in.shape, const),
            pl.BlockSpec(conv_w.shape, const),
            pl.BlockSpec(w_out.shape, const),
            pl.BlockSpec((1, d), const),
        ],
        out_specs=pl.BlockSpec((None, tm, d), lambda bi, i: (bi, i, 0)),
        out_shape=jax.ShapeDtypeStruct(x.shape, F32),
        scratch_shapes=[pltpu.VMEM((tm + HALO, d), F32)],
        compiler_params=pltpu.CompilerParams(
            dimension_semantics=("arbitrary", "arbitrary"),
            vmem_limit_bytes=VMEM_LIMIT_BYTES),
        name="conv_mixer",
    )(x, g0, w_in, conv_w, w_out, g1)


FF_CHUNK = 1024


def _mlp_kernel(x_ref, g2_ref, wup_ref, wdn_ref, g3_ref, y_ref):
    x = x_ref[...]
    h = _rms(x, g2_ref[...]).astype(BF16)
    d_ff = wup_ref.shape[1]
    f = None
    for c in range(d_ff // FF_CHUNK):
        cols = slice(c * FF_CHUNK, (c + 1) * FF_CHUNK)
        u = jnp.dot(h, wup_ref[:, cols], preferred_element_type=F32)
        a = jnp.square(jnp.maximum(u, 0.0)).astype(BF16)
        part = jnp.dot(a, wdn_ref[cols, :], preferred_element_type=F32)
        f = part if f is None else f + part
    y_ref[...] = x + _rms(f, g3_ref[...])


def _mlp(x, g2, w_up, w_down, g3, *, tm=ROW_TILE):
    b, s, d = x.shape
    const = lambda bi, i: (0, 0)
    return pl.pallas_call(
        _mlp_kernel,
        grid=(b, s // tm),
        in_specs=[
            pl.BlockSpec((None, tm, d), lambda bi, i: (bi, i, 0)),
            pl.BlockSpec((1, d), const),
            pl.BlockSpec(w_up.shape, const),
            pl.BlockSpec(w_down.shape, const),
            pl.BlockSpec((1, d), const),
        ],
        out_specs=pl.BlockSpec((None, tm, d), lambda bi, i: (bi, i, 0)),
        out_shape=jax.ShapeDtypeStruct(x.shape, F32),
        compiler_params=pltpu.CompilerParams(
            dimension_semantics=("arbitrary", "arbitrary"),
            vmem_limit_bytes=VMEM_LIMIT_BYTES),
        name="sq_relu_mlp",
    )(x, g2, w_up, w_down, g3)


def _ple_kernel(x_ref, p_ref, g4_ref, wg_ref, wp_ref, g5_ref, y_ref):
    for r0 in range(0, x_ref.shape[0], ROW_TILE):
        rows = slice(r0, r0 + ROW_TILE)
        x = x_ref[rows, :]
        h = _rms(x, g4_ref[...]).astype(BF16)
        gate = jax.nn.sigmoid(jnp.dot(h, wg_ref[...], preferred_element_type=F32))
        e = jnp.dot(p_ref[rows, :].astype(BF16), wp_ref[...],
                    preferred_element_type=F32) * gate
        y_ref[rows, :] = x + _rms(e, g5_ref[...])


def _ple(x, p, layer, g4, w_gate, w_proj, g5, *, tm=2 * ROW_TILE):
    b, s, d = x.shape
    const = lambda bi, i: (0, 0)
    return pl.pallas_call(
        _ple_kernel,
        grid=(b, s // tm),
        in_specs=[
            pl.BlockSpec((None, tm, d), lambda bi, i: (bi, i, 0)),
            pl.BlockSpec((None, None, tm, p.shape[-1]), lambda bi, i: (layer, bi, i, 0)),
            pl.BlockSpec((1, d), const),
            pl.BlockSpec(w_gate.shape, const),
            pl.BlockSpec(w_proj.shape, const),
            pl.BlockSpec((1, d), const),
        ],
        out_specs=pl.BlockSpec((None, tm, d), lambda bi, i: (bi, i, 0)),
        out_shape=jax.ShapeDtypeStruct(x.shape, F32),
        compiler_params=pltpu.CompilerParams(
            dimension_semantics=("arbitrary", "arbitrary"),
            vmem_limit_bytes=VMEM_LIMIT_BYTES),
        name="ple_gate",
    )(x, p, g4, w_gate, w_proj, g5)


def _prep_attn_weights(w_in, b_f, w_out):
    d = D_MODEL
    wq_t = w_in[:, 0:d].T.astype(BF16)
    wk = w_in[:, d:2 * d].astype(BF16)
    wv_t = w_in[:, 2 * d:3 * d].T.astype(BF16)
    wf = w_in[:, 3 * d:]
    wf_t = jnp.pad(jnp.concatenate([wf, wf, wf], axis=1),
                   ((0, 0), (0, HEAD_PAD - 3 * N_HEADS))).T.astype(BF16)
    bf_col = jnp.pad(jnp.concatenate([b_f, b_f, b_f]), (0, HEAD_PAD - 3 * N_HEADS))
    bf_col = bf_col.reshape(HEAD_PAD, 1).astype(F32)
    return wq_t, wk, wv_t, wf_t, bf_col, w_out.T.astype(BF16)


def kernel(x, p, norm_g, w_attn_in, b_forget, w_attn_out, w_conv_in, conv_w, w_conv_out,
           w_mlp_up, w_mlp_down, w_ple_proj, w_ple_gate):
    depth = norm_g.shape[0]
    pk = jnp.asarray(_k_placement_matrix(), dtype=BF16)
    for i in range(depth):
        g = norm_g[i].astype(F32)
        gi = lambda n: g[n:n + 1]
        j = i // 2
        if i % 2 == 0:
            wq_t, wk, wv_t, wf_t, bf_col, w_out_t = _prep_attn_weights(
                w_attn_in[j], b_forget[j], w_attn_out[j])
            q_t, k, v_t = _attn_in(x, gi(0), wq_t, wk, wv_t, wf_t, bf_col, pk)
            o_t = _flash(q_t, k, v_t)
            x = _attn_out(o_t, x, w_out_t, gi(1))
        else:
            x = _conv_layer(x, gi(0), w_conv_in[j].astype(BF16), conv_w[j].astype(F32),
                            w_conv_out[j].astype(BF16), gi(1))
        x = _mlp(x, gi(2), w_mlp_up[i].astype(BF16), w_mlp_down[i].astype(BF16), gi(3))
        x = _ple(x, p, i, gi(4), w_ple_gate[i].astype(BF16), w_ple_proj[i].astype(BF16),
                 gi(5))
    return x
```

```python
import functools

import numpy as np
import jax
import jax.numpy as jnp
from jax import lax
from jax.experimental import pallas as pl
from jax.experimental.pallas import tpu as pltpu

F32 = jnp.float32
BF16 = jnp.bfloat16

D_MODEL = 1024
N_HEADS = 16
HEAD_DIM = D_MODEL // N_HEADS
HEAD_PAD = 128
QK_PAD = N_HEADS * HEAD_PAD
V_ROWS = 128
V_ONES_ROW = HEAD_DIM
CONV_WIDTH = 3
RMS_EPS = 1e-6
NEG_INF = -1e30
LOG2E = 1.4426950408889634
Q_SCALE = (HEAD_DIM ** -0.5) * LOG2E

VMEM_LIMIT_BYTES = 56 * 1024 * 1024

ROW_TILE = 512
V_BLOCK = ROW_TILE
ATTN_BLOCK = 1024
QUERY_TILE = 256
SCORE_LOOKAHEAD = 4

SLOT_CQ = (64, 65, 66)
SLOT_CK = (67, 68, 69)
ROW_ONES = 48


def _rms(xf, g):
    ms = jnp.mean(xf * xf, axis=-1, keepdims=True)
    return xf * lax.rsqrt(ms + RMS_EPS) * g


def _split3(x):
    hi = x.astype(BF16).astype(F32)
    r = x - hi
    mid = r.astype(BF16).astype(F32)
    lo = (r - mid).astype(BF16).astype(F32)
    return hi, mid, lo


def _k_placement_matrix():
    pk = np.zeros((HEAD_PAD, QK_PAD), np.float32)
    for h in range(N_HEADS):
        base = h * HEAD_PAD
        for part in range(3):
            pk[part * N_HEADS + h, base + SLOT_CK[part]] = -1.0
            pk[ROW_ONES, base + SLOT_CQ[part]] = 1.0
    return pk


def _attn_in_kernel(x_ref, g_ref, wq_ref, wk_ref, wv_ref, wf_ref, bf_ref, pk_ref,
                    q_ref, k_ref, v_ref, carry_ref, *, tm):
    @pl.when(pl.program_id(1) == 0)
    def _():
        carry_ref[...] = jnp.zeros_like(carry_ref)

    hn = _rms(x_ref[...], g_ref[...])
    h = hn.astype(BF16)
    ht = hn.T.astype(BF16)

    f = jnp.dot(wf_ref[...], ht, preferred_element_type=F32) + bf_ref[...]

    vt = jnp.dot(wv_ref[...], ht, preferred_element_type=F32)
    pad_rows = jnp.where(
        lax.broadcasted_iota(jnp.int32, (V_ROWS - HEAD_DIM, tm), 0) == 0, 1.0, 0.0)
    for hh in range(N_HEADS):
        v_ref[hh] = jnp.concatenate(
            [vt[hh * HEAD_DIM:(hh + 1) * HEAD_DIM], pad_rows], axis=0).astype(BF16)

    kk = jnp.dot(h, wk_ref[...], preferred_element_type=F32)
    qt = jnp.dot(wq_ref[...], ht, preferred_element_type=F32) * Q_SCALE

    logf = (jnp.minimum(f, 0.0) - jnp.log1p(jnp.exp(-jnp.abs(f)))) * LOG2E
    row = lax.broadcasted_iota(jnp.int32, (tm, tm), 0)
    col = lax.broadcasted_iota(jnp.int32, (tm, tm), 1)
    triu = (row <= col).astype(BF16)
    c = carry_ref[...]
    for part in _split3(logf):
        c = c + jnp.dot(part.astype(BF16), triu, preferred_element_type=F32)
    carry_ref[...] = c[:, tm - 1:tm]
    chi, cmid, clo = _split3(c)

    sub = lax.broadcasted_iota(jnp.int32, (HEAD_PAD, tm), 0)
    bias_t = jnp.where(sub < N_HEADS, chi,
                       jnp.where(sub < 2 * N_HEADS, cmid,
                                 jnp.where(sub < 3 * N_HEADS, clo,
                                           jnp.where(sub == ROW_ONES, 1.0, 0.0))))
    bias = bias_t.T.astype(BF16)
    kb = jnp.dot(bias, pk_ref[...], preferred_element_type=F32)
    low = lax.broadcasted_iota(jnp.int32, (tm, HEAD_PAD), 1) < HEAD_DIM
    for pr in range(N_HEADS // 2):
        pair = kk[:, pr * HEAD_PAD:(pr + 1) * HEAD_PAD]
        for odd in range(2):
            hh = 2 * pr + odd
            feats = pltpu.roll(pair, HEAD_DIM, axis=1) if odd else pair
            k_ref[hh] = (jnp.where(low, feats, 0.0)
                         + kb[:, hh * HEAD_PAD:(hh + 1) * HEAD_PAD]).astype(BF16)

    slot = lax.broadcasted_iota(jnp.int32, (HEAD_PAD - HEAD_DIM, tm), 0) + HEAD_DIM
    ones_slots = functools.reduce(jnp.logical_or, [slot == s for s in SLOT_CK])
    for hh in range(N_HEADS):
        bias_rows = jnp.where(ones_slots, 1.0, 0.0)
        for part, c_part in zip(SLOT_CQ, (chi, cmid, clo)):
            bias_rows = jnp.where(slot == part, c_part[hh:hh + 1], bias_rows)
        q_ref[hh] = jnp.concatenate(
            [qt[hh * HEAD_DIM:(hh + 1) * HEAD_DIM], bias_rows], axis=0).astype(BF16)


def _attn_in(x, g, wq_t, wk, wv_t, wf_t, bf_col, pk, *, tm=ROW_TILE):
    b, s, d = x.shape
    const = lambda bi, i: (0, 0)
    return pl.pallas_call(
        functools.partial(_attn_in_kernel, tm=tm),
        grid=(b, s // tm),
        in_specs=[
            pl.BlockSpec((None, tm, d), lambda bi, i: (bi, i, 0)),
            pl.BlockSpec((1, d), const),
            pl.BlockSpec(wq_t.shape, const),
            pl.BlockSpec(wk.shape, const),
            pl.BlockSpec(wv_t.shape, const),
            pl.BlockSpec(wf_t.shape, const),
            pl.BlockSpec(bf_col.shape, const),
            pl.BlockSpec(pk.shape, const),
        ],
        out_specs=[
            pl.BlockSpec((None, N_HEADS, HEAD_PAD, tm), lambda bi, i: (bi, 0, 0, i)),
            pl.BlockSpec((None, N_HEADS, tm, HEAD_PAD), lambda bi, i: (bi, 0, i, 0)),
            pl.BlockSpec((None, N_HEADS, None, V_ROWS, tm), lambda bi, i: (bi, 0, i, 0, 0)),
        ],
        out_shape=[
            jax.ShapeDtypeStruct((b, N_HEADS, HEAD_PAD, s), BF16),
            jax.ShapeDtypeStruct((b, N_HEADS, s, HEAD_PAD), BF16),
            jax.ShapeDtypeStruct((b, N_HEADS, s // tm, V_ROWS, tm), BF16),
        ],
        scratch_shapes=[pltpu.VMEM((HEAD_PAD, 1), F32)],
        compiler_params=pltpu.CompilerParams(
            dimension_semantics=("arbitrary", "arbitrary"),
            vmem_limit_bytes=VMEM_LIMIT_BYTES),
        name="attn_in",
    )(x, g, wq_t, wk, wv_t, wf_t, bf_col, pk)


def _flash_kernel(q_ref, qnext_ref, k_ref, v_ref, o_ref, s_refs, smax_refs, m_ref, acc_ref,
                  *, bk):
    i = pl.program_id(2)
    bq = 2 * bk
    sub = bk // V_BLOCK
    n_tiles = bq // QUERY_TILE
    m_ref[...] = jnp.full_like(m_ref, -jnp.inf)
    acc_ref[...] = jnp.zeros_like(acc_ref)

    def scores_tile(t, slot, c, rows=bk, queries=q_ref):
        cols = slice(c * QUERY_TILE, (c + 1) * QUERY_TILE)
        start = pl.multiple_of(t * bk, bk)
        s = jnp.dot(k_ref[pl.ds(start, rows), :], queries[:, cols],
                    preferred_element_type=F32)
        s_refs[slot][0:rows, cols] = s
        smax_refs[slot][:, cols] = jnp.max(s, axis=0, keepdims=True)

    def consume_tile(t, slot, c, rows=bk, key_offset=None):
        cols = slice(c * QUERY_TILE, (c + 1) * QUERY_TILE)
        s = s_refs[slot][0:rows, cols]
        s_max = smax_refs[slot][:, cols]
        if key_offset is not None:
            key = lax.broadcasted_iota(jnp.int32, (rows, QUERY_TILE), 0) + key_offset
            qry = lax.broadcasted_iota(jnp.int32, (rows, QUERY_TILE), 1) + c * QUERY_TILE
            s = jnp.where(key <= qry, s, NEG_INF)
            s_max = jnp.max(s, axis=0, keepdims=True)
        m_prev = m_ref[:, cols]
        m_new = jnp.maximum(m_prev, s_max)
        alpha = jnp.exp2(m_prev - m_new)
        p = jnp.exp2(s - m_new).astype(BF16)
        pv = None
        for u in range(pl.cdiv(rows, V_BLOCK)):
            n = min(V_BLOCK, rows - u * V_BLOCK)
            part = jnp.dot(v_ref[t * sub + u, :, 0:n], p[u * V_BLOCK:u * V_BLOCK + n, :],
                           preferred_element_type=F32)
            pv = part if pv is None else pv + part
        acc_ref[:, cols] = alpha * acc_ref[:, cols] + pv
        m_ref[:, cols] = m_new

    diag = []
    for r in range(2):
        for c in range(n_tiles):
            rows = min(bk, (c + 1) * QUERY_TILE - r * bk)
            if rows > 0:
                diag.append((r, c, rows, rows < bk or c * QUERY_TILE < (r + 1) * bk))

    la = SCORE_LOOKAHEAD

    @pl.when(i == 0)
    def _():
        for r, c, rows, _ in diag[:la]:
            scores_tile(r, r, c, rows)

    def body(g, carry):
        t = 2 * g
        for half in range(2):
            for c in range(n_tiles):
                consume_tile(t + half, half, c)
                ahead = half + (c + la) // n_tiles
                scores_tile(t + ahead, ahead % 2, (c + la) % n_tiles)
        return carry

    lax.fori_loop(0, i, body, 0)
    t = 2 * i
    for idx, (r, c, rows, masked) in enumerate(diag):
        consume_tile(t + r, r, c, rows, key_offset=r * bk if masked else None)
        if idx + la < len(diag):
            r2, c2, rows2, _ = diag[idx + la]
            scores_tile(t + r2, r2, c2, rows2)
        else:
            scores_tile(0, 0, idx + la - len(diag), queries=qnext_ref)

    acc = acc_ref[...]
    o_ref[...] = (acc[0:HEAD_DIM] / acc[V_ONES_ROW:V_ONES_ROW + 1]).astype(o_ref.dtype)


def _flash(q_t, k, v_t, *, bk=ATTN_BLOCK):
    b, nh, s, hp = k.shape
    bq = 2 * bk
    assert v_t.shape == (b, nh, s // V_BLOCK, V_ROWS, V_BLOCK)
    score_bufs = [pltpu.VMEM((bk, bq), F32) for _ in range(2)]
    smax_bufs = [pltpu.VMEM((1, bq), F32) for _ in range(2)]
    n_steps = s // bq
    return pl.pallas_call(
        functools.partial(_flash_kernel, bk=bk),
        grid=(b, nh, s // bq),
        in_specs=[
            pl.BlockSpec((None, None, hp, bq), lambda bi, h, i: (bi, h, 0, i)),
            pl.BlockSpec((None, None, hp, bq),
                         lambda bi, h, i: (bi, h, 0, jnp.minimum(i + 1, n_steps - 1))),
            pl.BlockSpec((None, None, s, hp), lambda bi, h, i: (bi, h, 0, 0)),
            pl.BlockSpec((None, None, s // V_BLOCK, V_ROWS, V_BLOCK),
                         lambda bi, h, i: (bi, h, 0, 0, 0)),
        ],
        out_specs=pl.BlockSpec((None, None, HEAD_DIM, bq), lambda bi, h, i: (bi, h, 0, i)),
        out_shape=jax.ShapeDtypeStruct((b, nh, HEAD_DIM, s), BF16),
        scratch_shapes=[score_bufs, smax_bufs,
                        pltpu.VMEM((1, bq), F32), pltpu.VMEM((V_ROWS, bq), F32)],
        compiler_params=pltpu.CompilerParams(
            dimension_semantics=("arbitrary", "arbitrary", "arbitrary"),
            vmem_limit_bytes=VMEM_LIMIT_BYTES),
        name="fox_flash",
    )(q_t, q_t, k, v_t)


def _attn_out_kernel(o_ref, x_ref, w_ref, g_ref, y_ref):
    for r0 in range(0, x_ref.shape[0], ROW_TILE):
        rows = slice(r0, r0 + ROW_TILE)
        o_t = jnp.concatenate([o_ref[hh, :, rows] for hh in range(N_HEADS)], axis=0)
        m_t = jnp.dot(w_ref[...], o_t, preferred_element_type=F32)
        y_ref[rows, :] = x_ref[rows, :] + _rms(m_t.T, g_ref[...])


def _attn_out(o_t, x, w_out_t, g, *, tm=2 * ROW_TILE):
    b, s, d = x.shape
    const = lambda bi, i: (0, 0)
    return pl.pallas_call(
        _attn_out_kernel,
        grid=(b, s // tm),
        in_specs=[
            pl.BlockSpec((None, N_HEADS, HEAD_DIM, tm), lambda bi, i: (bi, 0, 0, i)),
            pl.BlockSpec((None, tm, d), lambda bi, i: (bi, i, 0)),
            pl.BlockSpec(w_out_t.shape, const),
            pl.BlockSpec((1, d), const),
        ],
        out_specs=pl.BlockSpec((None, tm, d), lambda bi, i: (bi, i, 0)),
        out_shape=jax.ShapeDtypeStruct(x.shape, F32),
        compiler_params=pltpu.CompilerParams(
            dimension_semantics=("arbitrary", "arbitrary"),
            vmem_limit_bytes=VMEM_LIMIT_BYTES),
        name="attn_out",
    )(o_t, x, w_out_t, g)


HALO = 8


def _conv_kernel(x_ref, g0_ref, win_ref, cw_ref, wout_ref, g1_ref, y_ref, z_ref, *, tm):
    d = D_MODEL

    @pl.when(pl.program_id(1) == 0)
    def _():
        z_ref[0:HALO, :] = jnp.zeros((HALO, d), F32)

    @pl.when(pl.program_id(1) > 0)
    def _():
        z_ref[0:HALO, :] = z_ref[tm:tm + HALO, :]

    x = x_ref[...]
    h = _rms(x, g0_ref[...]).astype(BF16)
    c_gate = jnp.dot(h, win_ref[:, d:2 * d], preferred_element_type=F32)
    u = jnp.dot(h, win_ref[:, 2 * d:3 * d], preferred_element_type=F32)
    z = c_gate * u
    z_ref[HALO:HALO + tm, :] = z
    cw = cw_ref[...]
    zc = (cw[2:3, :] * z
          + cw[1:2, :] * z_ref[HALO - 1:HALO - 1 + tm, :]
          + cw[0:1, :] * z_ref[HALO - 2:HALO - 2 + tm, :])
    b_gate = jnp.dot(h, win_ref[:, 0:d], preferred_element_type=F32)
    y = (b_gate * zc).astype(BF16)
    m = jnp.dot(y, wout_ref[...], preferred_element_type=F32)
    y_ref[...] = x + _rms(m, g1_ref[...])


def _conv_layer(x, g0, w_in, conv_w, w_out, g1, *, tm=ROW_TILE):
    b, s, d = x.shape
    const = lambda bi, i: (0, 0)
    return pl.pallas_call(
        functools.partial(_conv_kernel, tm=tm),
        grid=(b, s // tm),
        in_specs=[
            pl.BlockSpec((None, tm, d), lambda bi, i: (bi, i, 0)),
            pl.BlockSpec((1, d), const),
            pl.BlockSpec(w_in.shape, const),
            pl.BlockSpec(conv_w.shape, const),
            pl.BlockSpec(w_out.shape, const),
            pl.BlockSpec((1, d), const),
        ],
        out_specs=pl.BlockSpec((None, tm, d), lambda bi, i: (bi, i, 0)),
        out_shape=jax.ShapeDtypeStruct(x.shape, F32),
        scratch_shapes=[pltpu.VMEM((tm + HALO, d), F32)],
        compiler_params=pltpu.CompilerParams(
            dimension_semantics=("arbitrary", "arbitrary"),
            vmem_limit_bytes=VMEM_LIMIT_BYTES),
        name="conv_mixer",
    )(x, g0, w_in, conv_w, w_out, g1)


FF_CHUNK = 1024


def _mlp_ple_kernel(x_ref, p_ref, g_ref, wup_ref, wdn_ref, wg_ref, wp_ref, y_ref):
    d_ff = wup_ref.shape[1]
    tiles = [slice(r0, r0 + ROW_TILE) for r0 in range(0, x_ref.shape[0], ROW_TILE)]

    def mlp(rows):
        x = x_ref[rows, :]
        h = _rms(x, g_ref[0:1, :]).astype(BF16)
        f = None
        for c in range(d_ff // FF_CHUNK):
            cols = slice(c * FF_CHUNK, (c + 1) * FF_CHUNK)
            u = jnp.dot(h, wup_ref[:, cols], preferred_element_type=F32)
            a = jnp.square(jnp.maximum(u, 0.0)).astype(BF16)
            part = jnp.dot(a, wdn_ref[cols, :], preferred_element_type=F32)
            f = part if f is None else f + part
        return x + _rms(f, g_ref[1:2, :])

    def ple(rows, x):
        h = _rms(x, g_ref[2:3, :]).astype(BF16)
        gate = jax.nn.sigmoid(jnp.dot(h, wg_ref[...], preferred_element_type=F32))
        e = jnp.dot(p_ref[rows, :].astype(BF16), wp_ref[...],
                    preferred_element_type=F32) * gate
        y_ref[rows, :] = x + _rms(e, g_ref[3:4, :])

    xs = [mlp(rows) for rows in tiles]
    for rows, x in zip(tiles, xs):
        ple(rows, x)


def _mlp_ple(x, p, layer, gains, w_up, w_down, w_gate, w_proj, *, tm=2 * ROW_TILE):
    b, s, d = x.shape
    const = lambda bi, i: (0, 0)
    resident = lambda a: pl.BlockSpec(a.shape, const, pipeline_mode=pl.Buffered(1))
    return pl.pallas_call(
        _mlp_ple_kernel,
        grid=(b, s // tm),
        in_specs=[
            pl.BlockSpec((None, tm, d), lambda bi, i: (bi, i, 0)),
            pl.BlockSpec((None, None, tm, p.shape[-1]), lambda bi, i: (layer, bi, i, 0)),
            pl.BlockSpec(gains.shape, const),
            resident(w_up), resident(w_down), resident(w_gate), resident(w_proj),
        ],
        out_specs=pl.BlockSpec((None, tm, d), lambda bi, i: (bi, i, 0)),
        out_shape=jax.ShapeDtypeStruct(x.shape, F32),
        compiler_params=pltpu.CompilerParams(
            dimension_semantics=("arbitrary", "arbitrary"),
            vmem_limit_bytes=VMEM_LIMIT_BYTES),
        name="mlp_ple",
    )(x, p, gains, w_up, w_down, w_gate, w_proj)


def _prep_attn_weights(w_in, b_f, w_out):
    d = D_MODEL
    wq_t = w_in[:, 0:d].T.astype(BF16)
    wk = w_in[:, d:2 * d].astype(BF16)
    wv_t = w_in[:, 2 * d:3 * d].T.astype(BF16)
    wf = w_in[:, 3 * d:]
    wf_t = jnp.pad(jnp.concatenate([wf, wf, wf], axis=1),
                   ((0, 0), (0, HEAD_PAD - 3 * N_HEADS))).T.astype(BF16)
    bf_col = jnp.pad(jnp.concatenate([b_f, b_f, b_f]), (0, HEAD_PAD - 3 * N_HEADS))
    bf_col = bf_col.reshape(HEAD_PAD, 1).astype(F32)
    return wq_t, wk, wv_t, wf_t, bf_col, w_out.T.astype(BF16)


def kernel(x, p, norm_g, w_attn_in, b_forget, w_attn_out, w_conv_in, conv_w, w_conv_out,
           w_mlp_up, w_mlp_down, w_ple_proj, w_ple_gate):
    depth = norm_g.shape[0]
    pk = jnp.asarray(_k_placement_matrix(), dtype=BF16)
    for i in range(depth):
        g = norm_g[i].astype(F32)
        gi = lambda n: g[n:n + 1]
        j = i // 2
        if i % 2 == 0:
            wq_t, wk, wv_t, wf_t, bf_col, w_out_t = _prep_attn_weights(
                w_attn_in[j], b_forget[j], w_attn_out[j])
            q_t, k, v_t = _attn_in(x, gi(0), wq_t, wk, wv_t, wf_t, bf_col, pk)
            o_t = _flash(q_t, k, v_t)
            x = _attn_out(o_t, x, w_out_t, gi(1))
        else:
            x = _conv_layer(x, gi(0), w_conv_in[j].astype(BF16), conv_w[j].astype(F32),
                            w_conv_out[j].astype(BF16), gi(1))
        x = _mlp_ple(x, p, i, g[2:6], w_mlp_up[i].astype(BF16), w_mlp_down[i].astype(BF16),
                     w_ple_gate[i].astype(BF16), w_ple_proj[i].astype(BF16))
    return x
```

```python
import functools

import numpy as np
import jax
import jax.numpy as jnp
from jax import lax
from jax.experimental import pallas as pl
from jax.experimental.pallas import tpu as pltpu

F32 = jnp.float32
BF16 = jnp.bfloat16

D_MODEL = 1024
N_HEADS = 16
HEAD_DIM = D_MODEL // N_HEADS
HEAD_PAD = 128
QK_PAD = N_HEADS * HEAD_PAD
V_ROWS = 128
V_ONES_ROW = HEAD_DIM
CONV_WIDTH = 3
RMS_EPS = 1e-6
NEG_INF = -1e30
LOG2E = 1.4426950408889634
Q_SCALE = (HEAD_DIM ** -0.5) * LOG2E

VMEM_LIMIT_BYTES = 56 * 1024 * 1024

ROW_TILE = 512
V_BLOCK = ROW_TILE
ATTN_BLOCK = 1024
QUERY_TILE = 256
SCORE_LOOKAHEAD = 4

SLOT_CQ = (64, 65, 66)
SLOT_CK = (67, 68, 69)
ROW_ONES = 48


def _rms(xf, g):
    ms = jnp.mean(xf * xf, axis=-1, keepdims=True)
    return xf * lax.rsqrt(ms + RMS_EPS) * g


def _split3(x):
    hi = x.astype(BF16).astype(F32)
    r = x - hi
    mid = r.astype(BF16).astype(F32)
    lo = (r - mid).astype(BF16).astype(F32)
    return hi, mid, lo


def _k_placement_matrix():
    pk = np.zeros((HEAD_PAD, QK_PAD), np.float32)
    for h in range(N_HEADS):
        base = h * HEAD_PAD
        for part in range(3):
            pk[part * N_HEADS + h, base + SLOT_CK[part]] = -1.0
            pk[ROW_ONES, base + SLOT_CQ[part]] = 1.0
    return pk


def _attn_in_kernel(x_ref, g_ref, wq_ref, wk_ref, wv_ref, wf_ref, bf_ref, pk_ref,
                    q_ref, k_ref, v_ref, carry_ref, *, tm):
    @pl.when(pl.program_id(1) == 0)
    def _():
        carry_ref[...] = jnp.zeros_like(carry_ref)

    hn = _rms(x_ref[...], g_ref[...])
    h = hn.astype(BF16)
    ht = hn.T.astype(BF16)

    f = jnp.dot(wf_ref[...], ht, preferred_element_type=F32) + bf_ref[...]

    vt = jnp.dot(wv_ref[...], ht, preferred_element_type=F32)
    pad_rows = jnp.where(
        lax.broadcasted_iota(jnp.int32, (V_ROWS - HEAD_DIM, tm), 0) == 0, 1.0, 0.0)
    for hh in range(N_HEADS):
        v_ref[hh] = jnp.concatenate(
            [vt[hh * HEAD_DIM:(hh + 1) * HEAD_DIM], pad_rows], axis=0).astype(BF16)

    kk = jnp.dot(h, wk_ref[...], preferred_element_type=F32)
    qt = jnp.dot(wq_ref[...], ht, preferred_element_type=F32) * Q_SCALE

    logf = (jnp.minimum(f, 0.0) - jnp.log1p(jnp.exp(-jnp.abs(f)))) * LOG2E
    row = lax.broadcasted_iota(jnp.int32, (tm, tm), 0)
    col = lax.broadcasted_iota(jnp.int32, (tm, tm), 1)
    triu = (row <= col).astype(BF16)
    c = carry_ref[...]
    for part in _split3(logf):
        c = c + jnp.dot(part.astype(BF16), triu, preferred_element_type=F32)
    carry_ref[...] = c[:, tm - 1:tm]
    chi, cmid, clo = _split3(c)

    sub = lax.broadcasted_iota(jnp.int32, (HEAD_PAD, tm), 0)
    bias_t = jnp.where(sub < N_HEADS, chi,
                       jnp.where(sub < 2 * N_HEADS, cmid,
                                 jnp.where(sub < 3 * N_HEADS, clo,
                                           jnp.where(sub == ROW_ONES, 1.0, 0.0))))
    bias = bias_t.T.astype(BF16)
    kb = jnp.dot(bias, pk_ref[...], preferred_element_type=F32)
    low = lax.broadcasted_iota(jnp.int32, (tm, HEAD_PAD), 1) < HEAD_DIM
    for pr in range(N_HEADS // 2):
        pair = kk[:, pr * HEAD_PAD:(pr + 1) * HEAD_PAD]
        for odd in range(2):
            hh = 2 * pr + odd
            feats = pltpu.roll(pair, HEAD_DIM, axis=1) if odd else pair
            k_ref[hh] = (jnp.where(low, feats, 0.0)
                         + kb[:, hh * HEAD_PAD:(hh + 1) * HEAD_PAD]).astype(BF16)

    slot = lax.broadcasted_iota(jnp.int32, (HEAD_PAD - HEAD_DIM, tm), 0) + HEAD_DIM
    ones_slots = functools.reduce(jnp.logical_or, [slot == s for s in SLOT_CK])
    for hh in range(N_HEADS):
        bias_rows = jnp.where(ones_slots, 1.0, 0.0)
        for part, c_part in zip(SLOT_CQ, (chi, cmid, clo)):
            bias_rows = jnp.where(slot == part, c_part[hh:hh + 1], bias_rows)
        q_ref[hh] = jnp.concatenate(
            [qt[hh * HEAD_DIM:(hh + 1) * HEAD_DIM], bias_rows], axis=0).astype(BF16)


def _attn_in(x, g, wq_t, wk, wv_t, wf_t, bf_col, pk, *, tm=ROW_TILE):
    b, s, d = x.shape
    const = lambda bi, i: (0, 0)
    return pl.pallas_call(
        functools.partial(_attn_in_kernel, tm=tm),
        grid=(b, s // tm),
        in_specs=[
            pl.BlockSpec((None, tm, d), lambda bi, i: (bi, i, 0)),
            pl.BlockSpec((1, d), const),
            pl.BlockSpec(wq_t.shape, const),
            pl.BlockSpec(wk.shape, const),
            pl.BlockSpec(wv_t.shape, const),
            pl.BlockSpec(wf_t.shape, const),
            pl.BlockSpec(bf_col.shape, const),
            pl.BlockSpec(pk.shape, const),
        ],
        out_specs=[
            pl.BlockSpec((None, N_HEADS, HEAD_PAD, tm), lambda bi, i: (bi, 0, 0, i)),
            pl.BlockSpec((None, N_HEADS, tm, HEAD_PAD), lambda bi, i: (bi, 0, i, 0)),
            pl.BlockSpec((None, N_HEADS, None, V_ROWS, tm), lambda bi, i: (bi, 0, i, 0, 0)),
        ],
        out_shape=[
            jax.ShapeDtypeStruct((b, N_HEADS, HEAD_PAD, s), BF16),
            jax.ShapeDtypeStruct((b, N_HEADS, s, HEAD_PAD), BF16),
            jax.ShapeDtypeStruct((b, N_HEADS, s // tm, V_ROWS, tm), BF16),
        ],
        scratch_shapes=[pltpu.VMEM((HEAD_PAD, 1), F32)],
        compiler_params=pltpu.CompilerParams(
            dimension_semantics=("arbitrary", "arbitrary"),
            vmem_limit_bytes=VMEM_LIMIT_BYTES),
        name="attn_in",
    )(x, g, wq_t, wk, wv_t, wf_t, bf_col, pk)


def _flash_kernel(q_ref, qnext_ref, k_ref, v_ref, o_ref, s_refs, smax_refs, m_ref, acc_ref,
                  *, bk):
    i = pl.program_id(2)
    bq = 2 * bk
    sub = bk // V_BLOCK
    n_tiles = bq // QUERY_TILE
    m_ref[...] = jnp.full_like(m_ref, -jnp.inf)
    acc_ref[...] = jnp.zeros_like(acc_ref)

    def scores_tile(t, slot, c, rows=bk, queries=q_ref):
        cols = slice(c * QUERY_TILE, (c + 1) * QUERY_TILE)
        start = pl.multiple_of(t * bk, bk)
        s = jnp.dot(k_ref[pl.ds(start, rows), :], queries[:, cols],
                    preferred_element_type=F32)
        s_refs[slot][0:rows, cols] = s
        smax_refs[slot][:, cols] = jnp.max(s, axis=0, keepdims=True)

    def consume_tile(t, slot, c, rows=bk, key_offset=None):
        cols = slice(c * QUERY_TILE, (c + 1) * QUERY_TILE)
        s = s_refs[slot][0:rows, cols]
        s_max = smax_refs[slot][:, cols]
        if key_offset is not None:
            key = lax.broadcasted_iota(jnp.int32, (rows, QUERY_TILE), 0) + key_offset
            qry = lax.broadcasted_iota(jnp.int32, (rows, QUERY_TILE), 1) + c * QUERY_TILE
            s = jnp.where(key <= qry, s, NEG_INF)
            s_max = jnp.max(s, axis=0, keepdims=True)
        m_prev = m_ref[:, cols]
        m_new = jnp.maximum(m_prev, s_max)
        alpha = jnp.exp2(m_prev - m_new)
        p = jnp.exp2(s - m_new).astype(BF16)
        pv = None
        for u in range(pl.cdiv(rows, V_BLOCK)):
            n = min(V_BLOCK, rows - u * V_BLOCK)
            part = jnp.dot(v_ref[t * sub + u, :, 0:n], p[u * V_BLOCK:u * V_BLOCK + n, :],
                           preferred_element_type=F32)
            pv = part if pv is None else pv + part
        acc_ref[:, cols] = alpha * acc_ref[:, cols] + pv
        m_ref[:, cols] = m_new

    diag = []
    for r in range(2):
        for c in range(n_tiles):
            rows = min(bk, (c + 1) * QUERY_TILE - r * bk)
            if rows > 0:
                diag.append((r, c, rows, rows < bk or c * QUERY_TILE < (r + 1) * bk))

    la = SCORE_LOOKAHEAD

    @pl.when(i == 0)
    def _():
        for r, c, rows, _ in diag[:la]:
            scores_tile(r, r, c, rows)

    def body(g, carry):
        t = 2 * g
        for half in range(2):
            for c in range(n_tiles):
                consume_tile(t + half, half, c)
                ahead = half + (c + la) // n_tiles
                scores_tile(t + ahead, ahead % 2, (c + la) % n_tiles)
        return carry

    lax.fori_loop(0, i, body, 0)
    t = 2 * i
    for idx, (r, c, rows, masked) in enumerate(diag):
        consume_tile(t + r, r, c, rows, key_offset=r * bk if masked else None)
        if idx + la < len(diag):
            r2, c2, rows2, _ = diag[idx + la]
            scores_tile(t + r2, r2, c2, rows2)
        else:
            scores_tile(0, 0, idx + la - len(diag), queries=qnext_ref)

    acc = acc_ref[...]
    o_ref[...] = (acc[0:HEAD_DIM] / acc[V_ONES_ROW:V_ONES_ROW + 1]).astype(o_ref.dtype)


def _flash(q_t, k, v_t, *, bk=ATTN_BLOCK):
    b, nh, s, hp = k.shape
    bq = 2 * bk
    assert v_t.shape == (b, nh, s // V_BLOCK, V_ROWS, V_BLOCK)
    score_bufs = [pltpu.VMEM((bk, bq), F32) for _ in range(2)]
    smax_bufs = [pltpu.VMEM((1, bq), F32) for _ in range(2)]
    n_steps = s // bq
    return pl.pallas_call(
        functools.partial(_flash_kernel, bk=bk),
        grid=(b, nh, s // bq),
        in_specs=[
            pl.BlockSpec((None, None, hp, bq), lambda bi, h, i: (bi, h, 0, i)),
            pl.BlockSpec((None, None, hp, bq),
                         lambda bi, h, i: (bi, h, 0, jnp.minimum(i + 1, n_steps - 1))),
            pl.BlockSpec((None, None, s, hp), lambda bi, h, i: (bi, h, 0, 0)),
            pl.BlockSpec((None, None, s // V_BLOCK, V_ROWS, V_BLOCK),
                         lambda bi, h, i: (bi, h, 0, 0, 0)),
        ],
        out_specs=pl.BlockSpec((None, None, HEAD_DIM, bq), lambda bi, h, i: (bi, h, 0, i)),
        out_shape=jax.ShapeDtypeStruct((b, nh, HEAD_DIM, s), BF16),
        scratch_shapes=[score_bufs, smax_bufs,
                        pltpu.VMEM((1, bq), F32), pltpu.VMEM((V_ROWS, bq), F32)],
        compiler_params=pltpu.CompilerParams(
            dimension_semantics=("arbitrary", "arbitrary", "arbitrary"),
            vmem_limit_bytes=VMEM_LIMIT_BYTES),
        name="fox_flash",
    )(q_t, q_t, k, v_t)


def _attn_out_kernel(o_ref, x_ref, w_ref, g_ref, y_ref):
    for r0 in range(0, x_ref.shape[0], ROW_TILE):
        rows = slice(r0, r0 + ROW_TILE)
        o_t = jnp.concatenate([o_ref[hh, :, rows] for hh in range(N_HEADS)], axis=0)
        m_t = jnp.dot(w_ref[...], o_t, preferred_element_type=F32)
        y_ref[rows, :] = x_ref[rows, :] + _rms(m_t.T, g_ref[...])


def _attn_out(o_t, x, w_out_t, g, *, tm=2 * ROW_TILE):
    b, s, d = x.shape
    const = lambda bi, i: (0, 0)
    return pl.pallas_call(
        _attn_out_kernel,
        grid=(b, s // tm),
        in_specs=[
            pl.BlockSpec((None, N_HEADS, HEAD_DIM, tm), lambda bi, i: (bi, 0, 0, i)),
            pl.BlockSpec((None, tm, d), lambda bi, i: (bi, i, 0)),
            pl.BlockSpec(w_out_t.shape, const),
            pl.BlockSpec((1, d), const),
        ],
        out_specs=pl.BlockSpec((None, tm, d), lambda bi, i: (bi, i, 0)),
        out_shape=jax.ShapeDtypeStruct(x.shape, F32),
        compiler_params=pltpu.CompilerParams(
            dimension_semantics=("arbitrary", "arbitrary"),
            vmem_limit_bytes=VMEM_LIMIT_BYTES),
        name="attn_out",
    )(o_t, x, w_out_t, g)


HALO = 8


def _conv_kernel(x_ref, g0_ref, win_ref, cw_ref, wout_ref, g1_ref, y_ref, z_ref, *, tm):
    d = D_MODEL

    @pl.when(pl.program_id(1) == 0)
    def _():
        z_ref[0:HALO, :] = jnp.zeros((HALO, d), F32)

    @pl.when(pl.program_id(1) > 0)
    def _():
        z_ref[0:HALO, :] = z_ref[tm:tm + HALO, :]

    def in_proj(r0):
        x = x_ref[r0:r0 + ROW_TILE, :]
        h = _rms(x, g0_ref[...]).astype(BF16)
        c_gate = jnp.dot(h, win_ref[:, d:2 * d], preferred_element_type=F32)
        u = jnp.dot(h, win_ref[:, 2 * d:3 * d], preferred_element_type=F32)
        z_ref[HALO + r0:HALO + r0 + ROW_TILE, :] = c_gate * u
        return x, jnp.dot(h, win_ref[:, 0:d], preferred_element_type=F32)

    def out_proj(r0, x, b_gate):
        cw = cw_ref[...]
        zc = sum(cw[k:k + 1, :] * z_ref[pl.ds(HALO + r0 - (CONV_WIDTH - 1 - k), ROW_TILE), :]
                 for k in range(CONV_WIDTH))
        y = (b_gate * zc).astype(BF16)
        m = jnp.dot(y, wout_ref[...], preferred_element_type=F32)
        y_ref[r0:r0 + ROW_TILE, :] = x + _rms(m, g1_ref[...])

    starts = range(0, tm, ROW_TILE)
    staged = [in_proj(r0) for r0 in starts]
    for r0, (x, b_gate) in zip(starts, staged):
        out_proj(r0, x, b_gate)


def _conv_layer(x, g0, w_in, conv_w, w_out, g1, *, tm=2 * ROW_TILE):
    b, s, d = x.shape
    const = lambda bi, i: (0, 0)
    resident = lambda a: pl.BlockSpec(a.shape, const, pipeline_mode=pl.Buffered(1))
    return pl.pallas_call(
        functools.partial(_conv_kernel, tm=tm),
        grid=(b, s // tm),
        in_specs=[
            pl.BlockSpec((None, tm, d), lambda bi, i: (bi, i, 0)),
            pl.BlockSpec((1, d), const),
            resident(w_in),
            pl.BlockSpec(conv_w.shape, const),
            resident(w_out),
            pl.BlockSpec((1, d), const),
        ],
        out_specs=pl.BlockSpec((None, tm, d), lambda bi, i: (bi, i, 0)),
        out_shape=jax.ShapeDtypeStruct(x.shape, F32),
        scratch_shapes=[pltpu.VMEM((tm + HALO, d), F32)],
        compiler_params=pltpu.CompilerParams(
            dimension_semantics=("arbitrary", "arbitrary"),
            vmem_limit_bytes=VMEM_LIMIT_BYTES),
        name="conv_mixer",
    )(x, g0, w_in, conv_w, w_out, g1)


FF_CHUNK = 1024


def _mlp_ple_kernel(x_ref, p_ref, g_ref, wup_ref, wdn_ref, wg_ref, wp_ref, y_ref):
    d_ff = wup_ref.shape[1]
    tiles = [slice(r0, r0 + ROW_TILE) for r0 in range(0, x_ref.shape[0], ROW_TILE)]

    def mlp(rows):
        x = x_ref[rows, :]
        h = _rms(x, g_ref[0:1, :]).astype(BF16)
        f = None
        for c in range(d_ff // FF_CHUNK):
            cols = slice(c * FF_CHUNK, (c + 1) * FF_CHUNK)
            u = jnp.dot(h, wup_ref[:, cols], preferred_element_type=F32)
            a = jnp.square(jnp.maximum(u, 0.0)).astype(BF16)
            part = jnp.dot(a, wdn_ref[cols, :], preferred_element_type=F32)
            f = part if f is None else f + part
        return x + _rms(f, g_ref[1:2, :])

    def ple(rows, x):
        h = _rms(x, g_ref[2:3, :]).astype(BF16)
        gate = jax.nn.sigmoid(jnp.dot(h, wg_ref[...], preferred_element_type=F32))
        e = jnp.dot(p_ref[rows, :].astype(BF16), wp_ref[...],
                    preferred_element_type=F32) * gate
        y_ref[rows, :] = x + _rms(e, g_ref[3:4, :])

    xs = [mlp(rows) for rows in tiles]
    for rows, x in zip(tiles, xs):
        ple(rows, x)


def _mlp_ple(x, p, layer, gains, w_up, w_down, w_gate, w_proj, *, tm=2 * ROW_TILE):
    b, s, d = x.shape
    const = lambda bi, i: (0, 0)
    resident = lambda a: pl.BlockSpec(a.shape, const, pipeline_mode=pl.Buffered(1))
    return pl.pallas_call(
        _mlp_ple_kernel,
        grid=(b, s // tm),
        in_specs=[
            pl.BlockSpec((None, tm, d), lambda bi, i: (bi, i, 0)),
            pl.BlockSpec((None, None, tm, p.shape[-1]), lambda bi, i: (layer, bi, i, 0)),
            pl.BlockSpec(gains.shape, const),
            resident(w_up), resident(w_down), resident(w_gate), resident(w_proj),
        ],
        out_specs=pl.BlockSpec((None, tm, d), lambda bi, i: (bi, i, 0)),
        out_shape=jax.ShapeDtypeStruct(x.shape, F32),
        compiler_params=pltpu.CompilerParams(
            dimension_semantics=("arbitrary", "arbitrary"),
            vmem_limit_bytes=VMEM_LIMIT_BYTES),
        name="mlp_ple",
    )(x, p, gains, w_up, w_down, w_gate, w_proj)


def _prep_attn_weights(w_in, b_f, w_out):
    d = D_MODEL
    wq_t = w_in[:, 0:d].T.astype(BF16)
    wk = w_in[:, d:2 * d].astype(BF16)
    wv_t = w_in[:, 2 * d:3 * d].T.astype(BF16)
    wf = w_in[:, 3 * d:]
    wf_t = jnp.pad(jnp.concatenate([wf, wf, wf], axis=1),
                   ((0, 0), (0, HEAD_PAD - 3 * N_HEADS))).T.astype(BF16)
    bf_col = jnp.pad(jnp.concatenate([b_f, b_f, b_f]), (0, HEAD_PAD - 3 * N_HEADS))
    bf_col = bf_col.reshape(HEAD_PAD, 1).astype(F32)
    return wq_t, wk, wv_t, wf_t, bf_col, w_out.T.astype(BF16)


def kernel(x, p, norm_g, w_attn_in, b_forget, w_attn_out, w_conv_in, conv_w, w_conv_out,
           w_mlp_up, w_mlp_down, w_ple_proj, w_ple_gate):
    depth = norm_g.shape[0]
    pk = jnp.asarray(_k_placement_matrix(), dtype=BF16)
    for i in range(depth):
        g = norm_g[i].astype(F32)
        gi = lambda n: g[n:n + 1]
        j = i // 2
        if i % 2 == 0:
            wq_t, wk, wv_t, wf_t, bf_col, w_out_t = _prep_attn_weights(
                w_attn_in[j], b_forget[j], w_attn_out[j])
            q_t, k, v_t = _attn_in(x, gi(0), wq_t, wk, wv_t, wf_t, bf_col, pk)
            o_t = _flash(q_t, k, v_t)
            x = _attn_out(o_t, x, w_out_t, gi(1))
        else:
            x = _conv_layer(x, gi(0), w_conv_in[j].astype(BF16), conv_w[j].astype(F32),
                            w_conv_out[j].astype(BF16), gi(1))
        x = _mlp_ple(x, p, i, g[2:6], w_mlp_up[i].astype(BF16), w_mlp_down[i].astype(BF16),
                     w_ple_gate[i].astype(BF16), w_ple_proj[i].astype(BF16))
    return x
```

```python
import functools

import numpy as np
import jax
import jax.numpy as jnp
from jax import lax
from jax.experimental import pallas as pl
from jax.experimental.pallas import tpu as pltpu

F32 = jnp.float32
BF16 = jnp.bfloat16

D_MODEL = 1024
N_HEADS = 16
HEAD_DIM = D_MODEL // N_HEADS
HEAD_PAD = 128
QK_PAD = N_HEADS * HEAD_PAD
V_ROWS = 128
V_ONES_ROW = HEAD_DIM
CONV_WIDTH = 3
RMS_EPS = 1e-6
NEG_INF = -1e30
LOG2E = 1.4426950408889634
Q_SCALE = (HEAD_DIM ** -0.5) * LOG2E

VMEM_LIMIT_BYTES = 56 * 1024 * 1024

ROW_TILE = 512
ATTN_BLOCK = 1024
V_BLOCK = ROW_TILE
QUERY_TILE = 256
SCORE_LOOKAHEAD = 4

SLOT_CQ = (64, 65, 66)
SLOT_CK = (67, 68, 69)
ROW_ONES = 48


def _rms(xf, g):
    ms = jnp.mean(xf * xf, axis=-1, keepdims=True)
    return xf * lax.rsqrt(ms + RMS_EPS) * g


def _split3(x):
    hi = x.astype(BF16).astype(F32)
    r = x - hi
    mid = r.astype(BF16).astype(F32)
    lo = (r - mid).astype(BF16).astype(F32)
    return hi, mid, lo


def _k_placement_matrix():
    pk = np.zeros((HEAD_PAD, QK_PAD), np.float32)
    for h in range(N_HEADS):
        base = h * HEAD_PAD
        for part in range(3):
            pk[part * N_HEADS + h, base + SLOT_CK[part]] = -1.0
            pk[ROW_ONES, base + SLOT_CQ[part]] = 1.0
    return pk


def _attn_in_kernel(x_ref, g_ref, wq_ref, wk_ref, wv_ref, wf_ref, bf_ref, pk_ref,
                    q_ref, k_ref, v_ref, carry_ref, *, tm):
    @pl.when(pl.program_id(1) == 0)
    def _():
        carry_ref[...] = jnp.zeros_like(carry_ref)

    hn = _rms(x_ref[...], g_ref[...])
    h = hn.astype(BF16)
    ht = hn.T.astype(BF16)

    f = jnp.dot(wf_ref[...], ht, preferred_element_type=F32) + bf_ref[...]

    vt = jnp.dot(wv_ref[...], ht, preferred_element_type=F32)
    pad_rows = jnp.where(
        lax.broadcasted_iota(jnp.int32, (V_ROWS - HEAD_DIM, tm), 0) == 0, 1.0, 0.0)
    for hh in range(N_HEADS):
        v_ref[hh] = jnp.concatenate(
            [vt[hh * HEAD_DIM:(hh + 1) * HEAD_DIM], pad_rows], axis=0).astype(BF16)

    kk = jnp.dot(h, wk_ref[...], preferred_element_type=F32)
    qt = jnp.dot(wq_ref[...], ht, preferred_element_type=F32) * Q_SCALE

    logf = (jnp.minimum(f, 0.0) - jnp.log1p(jnp.exp(-jnp.abs(f)))) * LOG2E
    row = lax.broadcasted_iota(jnp.int32, (tm, tm), 0)
    col = lax.broadcasted_iota(jnp.int32, (tm, tm), 1)
    triu = (row <= col).astype(BF16)
    c = carry_ref[...]
    for part in _split3(logf):
        c = c + jnp.dot(part.astype(BF16), triu, preferred_element_type=F32)
    carry_ref[...] = c[:, tm - 1:tm]
    chi, cmid, clo = _split3(c)

    sub = lax.broadcasted_iota(jnp.int32, (HEAD_PAD, tm), 0)
    bias_t = jnp.where(sub < N_HEADS, chi,
                       jnp.where(sub < 2 * N_HEADS, cmid,
                                 jnp.where(sub < 3 * N_HEADS, clo,
                                           jnp.where(sub == ROW_ONES, 1.0, 0.0))))
    bias = bias_t.T.astype(BF16)
    kb = jnp.dot(bias, pk_ref[...], preferred_element_type=F32)
    low = lax.broadcasted_iota(jnp.int32, (tm, HEAD_PAD), 1) < HEAD_DIM
    for pr in range(N_HEADS // 2):
        pair = kk[:, pr * HEAD_PAD:(pr + 1) * HEAD_PAD]
        for odd in range(2):
            hh = 2 * pr + odd
            feats = pltpu.roll(pair, HEAD_DIM, axis=1) if odd else pair
            k_ref[hh] = (jnp.where(low, feats, 0.0)
                         + kb[:, hh * HEAD_PAD:(hh + 1) * HEAD_PAD]).astype(BF16)

    slot = lax.broadcasted_iota(jnp.int32, (HEAD_PAD - HEAD_DIM, tm), 0) + HEAD_DIM
    ones_slots = functools.reduce(jnp.logical_or, [slot == s for s in SLOT_CK])
    for hh in range(N_HEADS):
        bias_rows = jnp.where(ones_slots, 1.0, 0.0)
        for part, c_part in zip(SLOT_CQ, (chi, cmid, clo)):
            bias_rows = jnp.where(slot == part, c_part[hh:hh + 1], bias_rows)
        q_ref[hh] = jnp.concatenate(
            [qt[hh * HEAD_DIM:(hh + 1) * HEAD_DIM], bias_rows], axis=0).astype(BF16)


def _attn_in(x, g, wq_t, wk, wv_t, wf_t, bf_col, pk, *, tm=ROW_TILE):
    b, s, d = x.shape
    const = lambda bi, i: (0, 0)
    return pl.pallas_call(
        functools.partial(_attn_in_kernel, tm=tm),
        grid=(b, s // tm),
        in_specs=[
            pl.BlockSpec((None, tm, d), lambda bi, i: (bi, i, 0)),
            pl.BlockSpec((1, d), const),
            pl.BlockSpec(wq_t.shape, const),
            pl.BlockSpec(wk.shape, const),
            pl.BlockSpec(wv_t.shape, const),
            pl.BlockSpec(wf_t.shape, const),
            pl.BlockSpec(bf_col.shape, const),
            pl.BlockSpec(pk.shape, const),
        ],
        out_specs=[
            pl.BlockSpec((None, N_HEADS, HEAD_PAD, tm), lambda bi, i: (bi, 0, 0, i)),
            pl.BlockSpec((None, N_HEADS, tm, HEAD_PAD), lambda bi, i: (bi, 0, i, 0)),
            pl.BlockSpec((None, N_HEADS, None, V_ROWS, tm),
                         lambda bi, i: (bi, 0, i // (V_BLOCK // tm), 0, i % (V_BLOCK // tm))),
        ],
        out_shape=[
            jax.ShapeDtypeStruct((b, N_HEADS, HEAD_PAD, s), BF16),
            jax.ShapeDtypeStruct((b, N_HEADS, s, HEAD_PAD), BF16),
            jax.ShapeDtypeStruct((b, N_HEADS, s // V_BLOCK, V_ROWS, V_BLOCK), BF16),
        ],
        scratch_shapes=[pltpu.VMEM((HEAD_PAD, 1), F32)],
        compiler_params=pltpu.CompilerParams(
            dimension_semantics=("arbitrary", "arbitrary"),
            vmem_limit_bytes=VMEM_LIMIT_BYTES),
        name="attn_in",
    )(x, g, wq_t, wk, wv_t, wf_t, bf_col, pk)


def _flash_kernel(q_ref, qnext_ref, k_ref, v_ref, o_ref, s_refs, smax_refs, m_ref, acc_ref,
                  *, bk):
    i = pl.program_id(2)
    bq = 2 * bk
    sub = bk // V_BLOCK
    n_tiles = bq // QUERY_TILE
    m_ref[...] = jnp.full_like(m_ref, -jnp.inf)
    acc_ref[...] = jnp.zeros_like(acc_ref)

    def scores_tile(t, slot, c, rows=bk, queries=q_ref):
        cols = slice(c * QUERY_TILE, (c + 1) * QUERY_TILE)
        start = pl.multiple_of(t * bk, bk)
        s = jnp.dot(k_ref[pl.ds(start, rows), :], queries[:, cols],
                    preferred_element_type=F32)
        s_refs[slot][0:rows, cols] = s
        smax_refs[slot][:, cols] = jnp.max(s, axis=0, keepdims=True)

    def consume_tile(t, slot, c, rows=bk, key_offset=None):
        cols = slice(c * QUERY_TILE, (c + 1) * QUERY_TILE)
        s = s_refs[slot][0:rows, cols]
        s_max = smax_refs[slot][:, cols]
        if key_offset is not None:
            key = lax.broadcasted_iota(jnp.int32, (rows, QUERY_TILE), 0) + key_offset
            qry = lax.broadcasted_iota(jnp.int32, (rows, QUERY_TILE), 1) + c * QUERY_TILE
            s = jnp.where(key <= qry, s, NEG_INF)
            s_max = jnp.max(s, axis=0, keepdims=True)
        m_prev = m_ref[:, cols]
        m_new = jnp.maximum(m_prev, s_max)
        alpha = jnp.exp2(m_prev - m_new)
        p = jnp.exp2(s - m_new).astype(BF16)
        pv = None
        for u in range(pl.cdiv(rows, V_BLOCK)):
            n = min(V_BLOCK, rows - u * V_BLOCK)
            part = jnp.dot(v_ref[t * sub + u, :, 0:n], p[u * V_BLOCK:u * V_BLOCK + n, :],
                           preferred_element_type=F32)
            pv = part if pv is None else pv + part
        acc_ref[:, cols] = alpha * acc_ref[:, cols] + pv
        m_ref[:, cols] = m_new

    diag = []
    for r in range(2):
        for c in range(n_tiles):
            rows = min(bk, (c + 1) * QUERY_TILE - r * bk)
            if rows > 0:
                diag.append((r, c, rows, rows < bk or c * QUERY_TILE < (r + 1) * bk))

    la = SCORE_LOOKAHEAD

    @pl.when(i == 0)
    def _():
        for r, c, rows, _ in diag[:la]:
            scores_tile(r, r, c, rows)

    def body(g, carry):
        t = 2 * g
        for half in range(2):
            for c in range(n_tiles):
                consume_tile(t + half, half, c)
                ahead = half + (c + la) // n_tiles
                scores_tile(t + ahead, ahead % 2, (c + la) % n_tiles)
        return carry

    lax.fori_loop(0, i, body, 0)
    t = 2 * i
    for idx, (r, c, rows, masked) in enumerate(diag):
        consume_tile(t + r, r, c, rows, key_offset=r * bk if masked else None)
        if idx + la < len(diag):
            r2, c2, rows2, _ = diag[idx + la]
            scores_tile(t + r2, r2, c2, rows2)
        else:
            scores_tile(0, 0, idx + la - len(diag), queries=qnext_ref)

    acc = acc_ref[...]
    o_ref[...] = (acc[0:HEAD_DIM] / acc[V_ONES_ROW:V_ONES_ROW + 1]).astype(o_ref.dtype)


def _flash(q_t, k, v_t, *, bk=ATTN_BLOCK):
    b, nh, s, hp = k.shape
    bq = 2 * bk
    assert v_t.shape == (b, nh, s // V_BLOCK, V_ROWS, V_BLOCK)
    score_bufs = [pltpu.VMEM((bk, bq), F32) for _ in range(2)]
    smax_bufs = [pltpu.VMEM((1, bq), F32) for _ in range(2)]
    n_steps = s // bq
    return pl.pallas_call(
        functools.partial(_flash_kernel, bk=bk),
        grid=(b, nh, s // bq),
        in_specs=[
            pl.BlockSpec((None, None, hp, bq), lambda bi, h, i: (bi, h, 0, i)),
            pl.BlockSpec((None, None, hp, bq),
                         lambda bi, h, i: (bi, h, 0, jnp.minimum(i + 1, n_steps - 1))),
            pl.BlockSpec((None, None, s, hp), lambda bi, h, i: (bi, h, 0, 0)),
            pl.BlockSpec((None, None, s // V_BLOCK, V_ROWS, V_BLOCK),
                         lambda bi, h, i: (bi, h, 0, 0, 0)),
        ],
        out_specs=pl.BlockSpec((None, None, HEAD_DIM, bq), lambda bi, h, i: (bi, h, 0, i)),
        out_shape=jax.ShapeDtypeStruct((b, nh, HEAD_DIM, s), BF16),
        scratch_shapes=[score_bufs, smax_bufs,
                        pltpu.VMEM((1, bq), F32), pltpu.VMEM((V_ROWS, bq), F32)],
        compiler_params=pltpu.CompilerParams(
            dimension_semantics=("arbitrary", "arbitrary", "arbitrary"),
            vmem_limit_bytes=VMEM_LIMIT_BYTES),
        name="fox_flash",
    )(q_t, q_t, k, v_t)


def _attn_out_kernel(o_ref, x_ref, w_ref, g_ref, y_ref):
    for r0 in range(0, x_ref.shape[0], ROW_TILE):
        rows = slice(r0, r0 + ROW_TILE)
        o_t = jnp.concatenate([o_ref[hh, :, rows] for hh in range(N_HEADS)], axis=0)
        m_t = jnp.dot(w_ref[...], o_t, preferred_element_type=F32)
        y_ref[rows, :] = x_ref[rows, :] + _rms(m_t.T, g_ref[...])


def _attn_out(o_t, x, w_out_t, g, *, tm=2 * ROW_TILE):
    b, s, d = x.shape
    const = lambda bi, i: (0, 0)
    return pl.pallas_call(
        _attn_out_kernel,
        grid=(b, s // tm),
        in_specs=[
            pl.BlockSpec((None, N_HEADS, HEAD_DIM, tm), lambda bi, i: (bi, 0, 0, i)),
            pl.BlockSpec((None, tm, d), lambda bi, i: (bi, i, 0)),
            pl.BlockSpec(w_out_t.shape, const),
            pl.BlockSpec((1, d), const),
        ],
        out_specs=pl.BlockSpec((None, tm, d), lambda bi, i: (bi, i, 0)),
        out_shape=jax.ShapeDtypeStruct(x.shape, F32),
        compiler_params=pltpu.CompilerParams(
            dimension_semantics=("arbitrary", "arbitrary"),
            vmem_limit_bytes=VMEM_LIMIT_BYTES),
        name="attn_out",
    )(o_t, x, w_out_t, g)


HALO = 8


def _conv_kernel(x_ref, g0_ref, win_ref, cw_ref, wout_ref, g1_ref, y_ref, z_ref, *, tm):
    d = D_MODEL

    @pl.when(pl.program_id(1) == 0)
    def _():
        z_ref[0:HALO, :] = jnp.zeros((HALO, d), F32)

    @pl.when(pl.program_id(1) > 0)
    def _():
        z_ref[0:HALO, :] = z_ref[tm:tm + HALO, :]

    def in_proj(r0):
        x = x_ref[r0:r0 + ROW_TILE, :]
        h = _rms(x, g0_ref[...]).astype(BF16)
        c_gate = jnp.dot(h, win_ref[:, d:2 * d], preferred_element_type=F32)
        u = jnp.dot(h, win_ref[:, 2 * d:3 * d], preferred_element_type=F32)
        z_ref[HALO + r0:HALO + r0 + ROW_TILE, :] = c_gate * u
        return x, jnp.dot(h, win_ref[:, 0:d], preferred_element_type=F32)

    def out_proj(r0, x, b_gate):
        cw = cw_ref[...]
        zc = sum(cw[k:k + 1, :] * z_ref[pl.ds(HALO + r0 - (CONV_WIDTH - 1 - k), ROW_TILE), :]
                 for k in range(CONV_WIDTH))
        y = (b_gate * zc).astype(BF16)
        m = jnp.dot(y, wout_ref[...], preferred_element_type=F32)
        y_ref[r0:r0 + ROW_TILE, :] = x + _rms(m, g1_ref[...])

    starts = range(0, tm, ROW_TILE)
    staged = [in_proj(r0) for r0 in starts]
    for r0, (x, b_gate) in zip(starts, staged):
        out_proj(r0, x, b_gate)


def _conv_layer(x, g0, w_in, conv_w, w_out, g1, *, tm=2 * ROW_TILE):
    b, s, d = x.shape
    const = lambda bi, i: (0, 0)
    resident = lambda a: pl.BlockSpec(a.shape, const, pipeline_mode=pl.Buffered(1))
    return pl.pallas_call(
        functools.partial(_conv_kernel, tm=tm),
        grid=(b, s // tm),
        in_specs=[
            pl.BlockSpec((None, tm, d), lambda bi, i: (bi, i, 0)),
            pl.BlockSpec((1, d), const),
            resident(w_in),
            pl.BlockSpec(conv_w.shape, const),
            resident(w_out),
            pl.BlockSpec((1, d), const),
        ],
        out_specs=pl.BlockSpec((None, tm, d), lambda bi, i: (bi, i, 0)),
        out_shape=jax.ShapeDtypeStruct(x.shape, F32),
        scratch_shapes=[pltpu.VMEM((tm + HALO, d), F32)],
        compiler_params=pltpu.CompilerParams(
            dimension_semantics=("arbitrary", "arbitrary"),
            vmem_limit_bytes=VMEM_LIMIT_BYTES),
        name="conv_mixer",
    )(x, g0, w_in, conv_w, w_out, g1)


FF_CHUNK = 1024


def _mlp_ple_kernel(x_ref, p_ref, g_ref, wup_ref, wdn_ref, wg_ref, wp_ref, y_ref):
    d_ff = wup_ref.shape[1]
    tiles = [slice(r0, r0 + ROW_TILE) for r0 in range(0, x_ref.shape[0], ROW_TILE)]

    def mlp(rows):
        x = x_ref[rows, :]
        h = _rms(x, g_ref[0:1, :]).astype(BF16)
        f = None
        for c in range(d_ff // FF_CHUNK):
            cols = slice(c * FF_CHUNK, (c + 1) * FF_CHUNK)
            u = jnp.dot(h, wup_ref[:, cols], preferred_element_type=F32)
            a = jnp.square(jnp.maximum(u, 0.0)).astype(BF16)
            part = jnp.dot(a, wdn_ref[cols, :], preferred_element_type=F32)
            f = part if f is None else f + part
        return x + _rms(f, g_ref[1:2, :])

    def ple(rows, x):
        h = _rms(x, g_ref[2:3, :]).astype(BF16)
        gate = jax.nn.sigmoid(jnp.dot(h, wg_ref[...], preferred_element_type=F32))
        e = jnp.dot(p_ref[rows, :].astype(BF16), wp_ref[...],
                    preferred_element_type=F32) * gate
        y_ref[rows, :] = x + _rms(e, g_ref[3:4, :])

    xs = [mlp(rows) for rows in tiles]
    for rows, x in zip(tiles, xs):
        ple(rows, x)


def _mlp_ple(x, p, layer, gains, w_up, w_down, w_gate, w_proj, *, tm=2 * ROW_TILE):
    b, s, d = x.shape
    const = lambda bi, i: (0, 0)
    resident = lambda a: pl.BlockSpec(a.shape, const, pipeline_mode=pl.Buffered(1))
    return pl.pallas_call(
        _mlp_ple_kernel,
        grid=(b, s // tm),
        in_specs=[
            pl.BlockSpec((None, tm, d), lambda bi, i: (bi, i, 0)),
            pl.BlockSpec((None, None, tm, p.shape[-1]), lambda bi, i: (layer, bi, i, 0)),
            pl.BlockSpec(gains.shape, const),
            resident(w_up), resident(w_down), resident(w_gate), resident(w_proj),
        ],
        out_specs=pl.BlockSpec((None, tm, d), lambda bi, i: (bi, i, 0)),
        out_shape=jax.ShapeDtypeStruct(x.shape, F32),
        compiler_params=pltpu.CompilerParams(
            dimension_semantics=("arbitrary", "arbitrary"),
            vmem_limit_bytes=VMEM_LIMIT_BYTES),
        name="mlp_ple",
    )(x, p, gains, w_up, w_down, w_gate, w_proj)


def _prep_attn_weights(w_in, b_f, w_out):
    d = D_MODEL
    wq_t = w_in[:, 0:d].T.astype(BF16)
    wk = w_in[:, d:2 * d].astype(BF16)
    wv_t = w_in[:, 2 * d:3 * d].T.astype(BF16)
    wf = w_in[:, 3 * d:]
    wf_t = jnp.pad(jnp.concatenate([wf, wf, wf], axis=1),
                   ((0, 0), (0, HEAD_PAD - 3 * N_HEADS))).T.astype(BF16)
    bf_col = jnp.pad(jnp.concatenate([b_f, b_f, b_f]), (0, HEAD_PAD - 3 * N_HEADS))
    bf_col = bf_col.reshape(HEAD_PAD, 1).astype(F32)
    return wq_t, wk, wv_t, wf_t, bf_col, w_out.T.astype(BF16)


def kernel(x, p, norm_g, w_attn_in, b_forget, w_attn_out, w_conv_in, conv_w, w_conv_out,
           w_mlp_up, w_mlp_down, w_ple_proj, w_ple_gate):
    depth = norm_g.shape[0]
    pk = jnp.asarray(_k_placement_matrix(), dtype=BF16)
    for i in range(depth):
        g = norm_g[i].astype(F32)
        gi = lambda n: g[n:n + 1]
        j = i // 2
        if i % 2 == 0:
            wq_t, wk, wv_t, wf_t, bf_col, w_out_t = _prep_attn_weights(
                w_attn_in[j], b_forget[j], w_attn_out[j])
            q_t, k, v_t = _attn_in(x, gi(0), wq_t, wk, wv_t, wf_t, bf_col, pk)
            o_t = _flash(q_t, k, v_t)
            x = _attn_out(o_t, x, w_out_t, gi(1))
        else:
            x = _conv_layer(x, gi(0), w_conv_in[j].astype(BF16), conv_w[j].astype(F32),
                            w_conv_out[j].astype(BF16), gi(1))
        x = _mlp_ple(x, p, i, g[2:6], w_mlp_up[i].astype(BF16), w_mlp_down[i].astype(BF16),
                     w_ple_gate[i].astype(BF16), w_ple_proj[i].astype(BF16))
    return x
```

```python
import functools

import numpy as np
import jax
import jax.numpy as jnp
from jax import lax
from jax.experimental import pallas as pl
from jax.experimental.pallas import tpu as pltpu

F32 = jnp.float32
BF16 = jnp.bfloat16

D_MODEL = 1024
N_HEADS = 16
HEAD_DIM = D_MODEL // N_HEADS
HEAD_PAD = 128
QK_PAD = N_HEADS * HEAD_PAD
V_ROWS = 128
V_ONES_ROW = HEAD_DIM
CONV_WIDTH = 3
RMS_EPS = 1e-6
NEG_INF = -1e30
LOG2E = 1.4426950408889634
Q_SCALE = (HEAD_DIM ** -0.5) * LOG2E

VMEM_LIMIT_BYTES = 56 * 1024 * 1024

ROW_TILE = 512
ATTN_BLOCK = 1024
V_BLOCK = ROW_TILE
QUERY_TILE = 256
SCORE_LOOKAHEAD = 4

SLOT_CQ = (64, 65, 66)
SLOT_CK = (67, 68, 69)
ROW_ONES = 48


def _rms(xf, g):
    ms = jnp.mean(xf * xf, axis=-1, keepdims=True)
    return xf * lax.rsqrt(ms + RMS_EPS) * g


def _split3(x):
    hi = x.astype(BF16).astype(F32)
    r = x - hi
    mid = r.astype(BF16).astype(F32)
    lo = (r - mid).astype(BF16).astype(F32)
    return hi, mid, lo


def _k_placement_matrix():
    pk = np.zeros((HEAD_PAD, QK_PAD), np.float32)
    for h in range(N_HEADS):
        base = h * HEAD_PAD
        for part in range(3):
            pk[part * N_HEADS + h, base + SLOT_CK[part]] = -1.0
            pk[ROW_ONES, base + SLOT_CQ[part]] = 1.0
    return pk


def _attn_in_kernel(x_ref, g_ref, wq_ref, wk_ref, wv_ref, wf_ref, bf_ref, pk_ref,
                    q_ref, k_ref, v_ref, carry_ref, *, tm):
    @pl.when(pl.program_id(1) == 0)
    def _():
        carry_ref[...] = jnp.zeros_like(carry_ref)

    hn = _rms(x_ref[...], g_ref[...])
    h = hn.astype(BF16)
    ht = hn.T.astype(BF16)

    f = jnp.dot(wf_ref[...], ht, preferred_element_type=F32) + bf_ref[...]

    vt = jnp.dot(wv_ref[...], ht, preferred_element_type=F32)
    pad_rows = jnp.where(
        lax.broadcasted_iota(jnp.int32, (V_ROWS - HEAD_DIM, tm), 0) == 0, 1.0, 0.0)
    for hh in range(N_HEADS):
        v_ref[hh] = jnp.concatenate(
            [vt[hh * HEAD_DIM:(hh + 1) * HEAD_DIM], pad_rows], axis=0).astype(BF16)

    kk = jnp.dot(h, wk_ref[...], preferred_element_type=F32)
    qt = jnp.dot(wq_ref[...], ht, preferred_element_type=F32) * Q_SCALE

    logf = (jnp.minimum(f, 0.0) - jnp.log1p(jnp.exp(-jnp.abs(f)))) * LOG2E
    row = lax.broadcasted_iota(jnp.int32, (tm, tm), 0)
    col = lax.broadcasted_iota(jnp.int32, (tm, tm), 1)
    triu = (row <= col).astype(BF16)
    c = carry_ref[...]
    for part in _split3(logf):
        c = c + jnp.dot(part.astype(BF16), triu, preferred_element_type=F32)
    carry_ref[...] = c[:, tm - 1:tm]
    chi, cmid, clo = _split3(c)

    sub = lax.broadcasted_iota(jnp.int32, (HEAD_PAD, tm), 0)
    bias_t = jnp.where(sub < N_HEADS, chi,
                       jnp.where(sub < 2 * N_HEADS, cmid,
                                 jnp.where(sub < 3 * N_HEADS, clo,
                                           jnp.where(sub == ROW_ONES, 1.0, 0.0))))
    bias = bias_t.T.astype(BF16)
    kb = jnp.dot(bias, pk_ref[...], preferred_element_type=F32)
    low = lax.broadcasted_iota(jnp.int32, (tm, HEAD_PAD), 1) < HEAD_DIM
    for pr in range(N_HEADS // 2):
        pair = kk[:, pr * HEAD_PAD:(pr + 1) * HEAD_PAD]
        for odd in range(2):
            hh = 2 * pr + odd
            feats = pltpu.roll(pair, HEAD_DIM, axis=1) if odd else pair
            k_ref[hh] = (jnp.where(low, feats, 0.0)
                         + kb[:, hh * HEAD_PAD:(hh + 1) * HEAD_PAD]).astype(BF16)

    slot = lax.broadcasted_iota(jnp.int32, (HEAD_PAD - HEAD_DIM, tm), 0) + HEAD_DIM
    ones_slots = functools.reduce(jnp.logical_or, [slot == s for s in SLOT_CK])
    for hh in range(N_HEADS):
        bias_rows = jnp.where(ones_slots, 1.0, 0.0)
        for part, c_part in zip(SLOT_CQ, (chi, cmid, clo)):
            bias_rows = jnp.where(slot == part, c_part[hh:hh + 1], bias_rows)
        q_ref[hh] = jnp.concatenate(
            [qt[hh * HEAD_DIM:(hh + 1) * HEAD_DIM], bias_rows], axis=0).astype(BF16)


def _attn_in(x, g, wq_t, wk, wv_t, wf_t, bf_col, pk, *, tm=ROW_TILE):
    b, s, d = x.shape
    const = lambda bi, i: (0, 0)
    return pl.pallas_call(
        functools.partial(_attn_in_kernel, tm=tm),
        grid=(b, s // tm),
        in_specs=[
            pl.BlockSpec((None, tm, d), lambda bi, i: (bi, i, 0)),
            pl.BlockSpec((1, d), const),
            pl.BlockSpec(wq_t.shape, const),
            pl.BlockSpec(wk.shape, const),
            pl.BlockSpec(wv_t.shape, const),
            pl.BlockSpec(wf_t.shape, const),
            pl.BlockSpec(bf_col.shape, const),
            pl.BlockSpec(pk.shape, const),
        ],
        out_specs=[
            pl.BlockSpec((None, N_HEADS, HEAD_PAD, tm), lambda bi, i: (bi, 0, 0, i)),
            pl.BlockSpec((None, N_HEADS, tm, HEAD_PAD), lambda bi, i: (bi, 0, i, 0)),
            pl.BlockSpec((None, N_HEADS, None, V_ROWS, tm),
                         lambda bi, i: (bi, 0, i // (V_BLOCK // tm), 0, i % (V_BLOCK // tm))),
        ],
        out_shape=[
            jax.ShapeDtypeStruct((b, N_HEADS, HEAD_PAD, s), BF16),
            jax.ShapeDtypeStruct((b, N_HEADS, s, HEAD_PAD), BF16),
            jax.ShapeDtypeStruct((b, N_HEADS, s // V_BLOCK, V_ROWS, V_BLOCK), BF16),
        ],
        scratch_shapes=[pltpu.VMEM((HEAD_PAD, 1), F32)],
        compiler_params=pltpu.CompilerParams(
            dimension_semantics=("arbitrary", "arbitrary"),
            vmem_limit_bytes=VMEM_LIMIT_BYTES),
        name="attn_in",
    )(x, g, wq_t, wk, wv_t, wf_t, bf_col, pk)


def _flash_kernel(q_ref, qnext_ref, k_ref, v_ref, o_ref, s_refs, smax_refs, m_ref, acc_ref,
                  *, bk):
    i = pl.program_id(2)
    bq = 2 * bk
    sub = bk // V_BLOCK
    n_tiles = bq // QUERY_TILE
    m_ref[...] = jnp.full_like(m_ref, -jnp.inf)
    acc_ref[...] = jnp.zeros_like(acc_ref)

    def scores_tile(t, slot, c, rows=bk, queries=q_ref):
        cols = slice(c * QUERY_TILE, (c + 1) * QUERY_TILE)
        start = pl.multiple_of(t * bk, bk)
        s = jnp.dot(k_ref[pl.ds(start, rows), :], queries[:, cols],
                    preferred_element_type=F32)
        s_refs[slot][0:rows, cols] = s
        smax_refs[slot][:, cols] = jnp.max(s, axis=0, keepdims=True)

    def consume_tile(t, slot, c, rows=bk, key_offset=None):
        cols = slice(c * QUERY_TILE, (c + 1) * QUERY_TILE)
        s = s_refs[slot][0:rows, cols]
        s_max = smax_refs[slot][:, cols]
        if key_offset is not None:
            key = lax.broadcasted_iota(jnp.int32, (rows, QUERY_TILE), 0) + key_offset
            qry = lax.broadcasted_iota(jnp.int32, (rows, QUERY_TILE), 1) + c * QUERY_TILE
            s = jnp.where(key <= qry, s, NEG_INF)
            s_max = jnp.max(s, axis=0, keepdims=True)
        m_prev = m_ref[:, cols]
        m_new = jnp.maximum(m_prev, s_max)
        alpha = jnp.exp2(m_prev - m_new)
        p = jnp.exp2(s - m_new).astype(BF16)
        pv = None
        for u in range(pl.cdiv(rows, V_BLOCK)):
            n = min(V_BLOCK, rows - u * V_BLOCK)
            part = jnp.dot(v_ref[t * sub + u, :, 0:n], p[u * V_BLOCK:u * V_BLOCK + n, :],
                           preferred_element_type=F32)
            pv = part if pv is None else pv + part
        acc_ref[:, cols] = alpha * acc_ref[:, cols] + pv
        m_ref[:, cols] = m_new

    diag = []
    for r in range(2):
        for c in range(n_tiles):
            rows = min(bk, (c + 1) * QUERY_TILE - r * bk)
            if rows > 0:
                diag.append((r, c, rows, rows < bk or c * QUERY_TILE < (r + 1) * bk))

    la = SCORE_LOOKAHEAD

    @pl.when(i == 0)
    def _():
        for r, c, rows, _ in diag[:la]:
            scores_tile(r, r, c, rows)

    def body(g, carry):
        t = 2 * g
        for half in range(2):
            for c in range(n_tiles):
                consume_tile(t + half, half, c)
                ahead = half + (c + la) // n_tiles
                scores_tile(t + ahead, ahead % 2, (c + la) % n_tiles)
        return carry

    lax.fori_loop(0, i, body, 0)
    t = 2 * i
    for idx, (r, c, rows, masked) in enumerate(diag):
        consume_tile(t + r, r, c, rows, key_offset=r * bk if masked else None)
        if idx + la < len(diag):
            r2, c2, rows2, _ = diag[idx + la]
            scores_tile(t + r2, r2, c2, rows2)
        else:
            scores_tile(0, 0, idx + la - len(diag), queries=qnext_ref)

    acc = acc_ref[...]
    o_ref[...] = (acc[0:HEAD_DIM] / acc[V_ONES_ROW:V_ONES_ROW + 1]).astype(o_ref.dtype)


def _flash(q_t, k, v_t, *, bk=ATTN_BLOCK):
    b, nh, s, hp = k.shape
    bq = 2 * bk
    assert v_t.shape == (b, nh, s // V_BLOCK, V_ROWS, V_BLOCK)
    score_bufs = [pltpu.VMEM((bk, bq), F32) for _ in range(2)]
    smax_bufs = [pltpu.VMEM((1, bq), F32) for _ in range(2)]
    n_steps = s // bq
    return pl.pallas_call(
        functools.partial(_flash_kernel, bk=bk),
        grid=(b, nh, s // bq),
        in_specs=[
            pl.BlockSpec((None, None, hp, bq), lambda bi, h, i: (bi, h, 0, i)),
            pl.BlockSpec((None, None, hp, bq),
                         lambda bi, h, i: (bi, h, 0, jnp.minimum(i + 1, n_steps - 1))),
            pl.BlockSpec((None, None, s, hp), lambda bi, h, i: (bi, h, 0, 0)),
            pl.BlockSpec((None, None, s // V_BLOCK, V_ROWS, V_BLOCK),
                         lambda bi, h, i: (bi, h, 0, 0, 0)),
        ],
        out_specs=pl.BlockSpec((None, None, HEAD_DIM, bq), lambda bi, h, i: (bi, h, 0, i)),
        out_shape=jax.ShapeDtypeStruct((b, nh, HEAD_DIM, s), BF16),
        scratch_shapes=[score_bufs, smax_bufs,
                        pltpu.VMEM((1, bq), F32), pltpu.VMEM((V_ROWS, bq), F32)],
        compiler_params=pltpu.CompilerParams(
            dimension_semantics=("arbitrary", "arbitrary", "arbitrary"),
            vmem_limit_bytes=VMEM_LIMIT_BYTES),
        name="fox_flash",
    )(q_t, q_t, k, v_t)


HALO = 8


def _conv_kernel(x_ref, g0_ref, win_ref, cw_ref, wout_ref, g1_ref, y_ref, z_ref, *, tm):
    d = D_MODEL

    @pl.when(pl.program_id(1) == 0)
    def _():
        z_ref[0:HALO, :] = jnp.zeros((HALO, d), F32)

    @pl.when(pl.program_id(1) > 0)
    def _():
        z_ref[0:HALO, :] = z_ref[tm:tm + HALO, :]

    def in_proj(r0):
        x = x_ref[r0:r0 + ROW_TILE, :]
        h = _rms(x, g0_ref[...]).astype(BF16)
        c_gate = jnp.dot(h, win_ref[:, d:2 * d], preferred_element_type=F32)
        u = jnp.dot(h, win_ref[:, 2 * d:3 * d], preferred_element_type=F32)
        z_ref[HALO + r0:HALO + r0 + ROW_TILE, :] = c_gate * u
        return x, jnp.dot(h, win_ref[:, 0:d], preferred_element_type=F32)

    def out_proj(r0, x, b_gate):
        cw = cw_ref[...]
        zc = sum(cw[k:k + 1, :] * z_ref[pl.ds(HALO + r0 - (CONV_WIDTH - 1 - k), ROW_TILE), :]
                 for k in range(CONV_WIDTH))
        y = (b_gate * zc).astype(BF16)
        m = jnp.dot(y, wout_ref[...], preferred_element_type=F32)
        y_ref[r0:r0 + ROW_TILE, :] = x + _rms(m, g1_ref[...])

    starts = range(0, tm, ROW_TILE)
    staged = [in_proj(r0) for r0 in starts]
    for r0, (x, b_gate) in zip(starts, staged):
        out_proj(r0, x, b_gate)


def _conv_layer(x, g0, w_in, conv_w, w_out, g1, *, tm=2 * ROW_TILE):
    b, s, d = x.shape
    const = lambda bi, i: (0, 0)
    resident = lambda a: pl.BlockSpec(a.shape, const, pipeline_mode=pl.Buffered(1))
    return pl.pallas_call(
        functools.partial(_conv_kernel, tm=tm),
        grid=(b, s // tm),
        in_specs=[
            pl.BlockSpec((None, tm, d), lambda bi, i: (bi, i, 0)),
            pl.BlockSpec((1, d), const),
            resident(w_in),
            pl.BlockSpec(conv_w.shape, const),
            resident(w_out),
            pl.BlockSpec((1, d), const),
        ],
        out_specs=pl.BlockSpec((None, tm, d), lambda bi, i: (bi, i, 0)),
        out_shape=jax.ShapeDtypeStruct(x.shape, F32),
        scratch_shapes=[pltpu.VMEM((tm + HALO, d), F32)],
        compiler_params=pltpu.CompilerParams(
            dimension_semantics=("arbitrary", "arbitrary"),
            vmem_limit_bytes=VMEM_LIMIT_BYTES),
        name="conv_mixer",
    )(x, g0, w_in, conv_w, w_out, g1)


FF_CHUNK = 1024
TAIL_SUB_TILE = 256


def _tail_kernel(*refs, attn_out):
    if attn_out:
        o_ref, wo_ref, *refs = refs
    x_ref, p_ref, g_ref, wup_ref, wdn_ref, wg_ref, wp_ref, y_ref = refs
    d_ff = wup_ref.shape[1]
    tiles = [slice(r0, r0 + TAIL_SUB_TILE) for r0 in range(0, x_ref.shape[0], TAIL_SUB_TILE)]

    def mixer_out(rows):
        x = x_ref[rows, :]
        if not attn_out:
            return x
        o_t = jnp.concatenate([o_ref[hh, :, rows] for hh in range(N_HEADS)], axis=0)
        m_t = jnp.dot(wo_ref[...], o_t, preferred_element_type=F32)
        return x + _rms(m_t.T, g_ref[0:1, :])

    def mlp(x):
        h = _rms(x, g_ref[1:2, :]).astype(BF16)
        f = None
        for c in range(d_ff // FF_CHUNK):
            cols = slice(c * FF_CHUNK, (c + 1) * FF_CHUNK)
            u = jnp.dot(h, wup_ref[:, cols], preferred_element_type=F32)
            a = jnp.square(jnp.maximum(u, 0.0)).astype(BF16)
            part = jnp.dot(a, wdn_ref[cols, :], preferred_element_type=F32)
            f = part if f is None else f + part
        return x + _rms(f, g_ref[2:3, :])

    def ple(rows, x):
        h = _rms(x, g_ref[3:4, :]).astype(BF16)
        gate = jax.nn.sigmoid(jnp.dot(h, wg_ref[...], preferred_element_type=F32))
        e = jnp.dot(p_ref[rows, :].astype(BF16), wp_ref[...],
                    preferred_element_type=F32) * gate
        y_ref[rows, :] = x + _rms(e, g_ref[4:5, :])

    xs = [mixer_out(rows) for rows in tiles]
    xs = [mlp(x) for x in xs]
    for rows, x in zip(tiles, xs):
        ple(rows, x)


def _layer_tail(x, p, layer, gains, w_up, w_down, w_gate, w_proj, o_t=None, w_out_t=None,
                *, tm=2 * TAIL_SUB_TILE):
    b, s, d = x.shape
    const = lambda bi, i: (0, 0)
    resident = lambda a: pl.BlockSpec(a.shape, const, pipeline_mode=pl.Buffered(1))
    attn_out = o_t is not None
    operands, specs = [], []
    if attn_out:
        operands += [o_t, w_out_t]
        specs += [pl.BlockSpec((None, N_HEADS, HEAD_DIM, tm), lambda bi, i: (bi, 0, 0, i)),
                  resident(w_out_t)]
    operands += [x, p, gains, w_up, w_down, w_gate, w_proj]
    specs += [
        pl.BlockSpec((None, tm, d), lambda bi, i: (bi, i, 0)),
        pl.BlockSpec((None, None, tm, p.shape[-1]), lambda bi, i: (layer, bi, i, 0)),
        pl.BlockSpec(gains.shape, const),
        resident(w_up), resident(w_down), resident(w_gate), resident(w_proj),
    ]
    return pl.pallas_call(
        functools.partial(_tail_kernel, attn_out=attn_out),
        grid=(b, s // tm),
        in_specs=specs,
        out_specs=pl.BlockSpec((None, tm, d), lambda bi, i: (bi, i, 0)),
        out_shape=jax.ShapeDtypeStruct(x.shape, F32),
        compiler_params=pltpu.CompilerParams(
            dimension_semantics=("arbitrary", "arbitrary"),
            vmem_limit_bytes=VMEM_LIMIT_BYTES),
        name="layer_tail",
    )(*operands)


def _prep_attn_weights(w_in, b_f, w_out):
    d = D_MODEL
    wq_t = w_in[:, 0:d].T.astype(BF16)
    wk = w_in[:, d:2 * d].astype(BF16)
    wv_t = w_in[:, 2 * d:3 * d].T.astype(BF16)
    wf = w_in[:, 3 * d:]
    wf_t = jnp.pad(jnp.concatenate([wf, wf, wf], axis=1),
                   ((0, 0), (0, HEAD_PAD - 3 * N_HEADS))).T.astype(BF16)
    bf_col = jnp.pad(jnp.concatenate([b_f, b_f, b_f]), (0, HEAD_PAD - 3 * N_HEADS))
    bf_col = bf_col.reshape(HEAD_PAD, 1).astype(F32)
    return wq_t, wk, wv_t, wf_t, bf_col, w_out.T.astype(BF16)


def kernel(x, p, norm_g, w_attn_in, b_forget, w_attn_out, w_conv_in, conv_w, w_conv_out,
           w_mlp_up, w_mlp_down, w_ple_proj, w_ple_gate):
    depth = norm_g.shape[0]
    pk = jnp.asarray(_k_placement_matrix(), dtype=BF16)
    for i in range(depth):
        g = norm_g[i].astype(F32)
        gi = lambda n: g[n:n + 1]
        j = i // 2
        if i % 2 == 0:
            wq_t, wk, wv_t, wf_t, bf_col, w_out_t = _prep_attn_weights(
                w_attn_in[j], b_forget[j], w_attn_out[j])
            q_t, k, v_t = _attn_in(x, gi(0), wq_t, wk, wv_t, wf_t, bf_col, pk)
            mixer = dict(o_t=_flash(q_t, k, v_t), w_out_t=w_out_t)
        else:
            x = _conv_layer(x, gi(0), w_conv_in[j].astype(BF16), conv_w[j].astype(F32),
                            w_conv_out[j].astype(BF16), gi(1))
            mixer = {}
        x = _layer_tail(x, p, i, g[1:6], w_mlp_up[i].astype(BF16),
                        w_mlp_down[i].astype(BF16), w_ple_gate[i].astype(BF16),
                        w_ple_proj[i].astype(BF16), **mixer)
    return x
```

```python
import functools

import numpy as np
import jax
import jax.numpy as jnp
from jax import lax
from jax.experimental import pallas as pl
from jax.experimental.pallas import tpu as pltpu

F32 = jnp.float32
BF16 = jnp.bfloat16

D_MODEL = 1024
N_HEADS = 16
HEAD_DIM = D_MODEL // N_HEADS
HEAD_PAD = 128
QK_PAD = N_HEADS * HEAD_PAD
V_ROWS = 128
V_ONES_ROW = HEAD_DIM
CONV_WIDTH = 3
RMS_EPS = 1e-6
NEG_INF = -1e30
LOG2E = 1.4426950408889634
Q_SCALE = (HEAD_DIM ** -0.5) * LOG2E

VMEM_LIMIT_BYTES = 56 * 1024 * 1024

ROW_TILE = 512
ATTN_BLOCK = 1024
V_BLOCK = ROW_TILE
QUERY_TILE = 256
SCORE_LOOKAHEAD = 4
SCORE_ROWS = 512

SLOT_CQ = (64, 65, 66)
SLOT_CK = (67, 68, 69)
ROW_ONES = 48


def _rms(xf, g):
    ms = jnp.mean(xf * xf, axis=-1, keepdims=True)
    return xf * lax.rsqrt(ms + RMS_EPS) * g


def _split3(x):
    hi = x.astype(BF16).astype(F32)
    r = x - hi
    mid = r.astype(BF16).astype(F32)
    lo = (r - mid).astype(BF16).astype(F32)
    return hi, mid, lo


def _k_placement_matrix():
    pk = np.zeros((HEAD_PAD, QK_PAD), np.float32)
    for h in range(N_HEADS):
        base = h * HEAD_PAD
        for part in range(3):
            pk[part * N_HEADS + h, base + SLOT_CK[part]] = -1.0
            pk[ROW_ONES, base + SLOT_CQ[part]] = 1.0
    return pk


def _attn_in_kernel(x_ref, g_ref, wq_ref, wk_ref, wv_ref, wf_ref, bf_ref, pk_ref,
                    q_ref, k_ref, v_ref, carry_ref, *, tm):
    @pl.when(pl.program_id(1) == 0)
    def _():
        carry_ref[...] = jnp.zeros_like(carry_ref)

    hn = _rms(x_ref[...], g_ref[...])
    h = hn.astype(BF16)
    ht = hn.T.astype(BF16)

    f = jnp.dot(wf_ref[...], ht, preferred_element_type=F32) + bf_ref[...]

    vt = jnp.dot(wv_ref[...], ht, preferred_element_type=F32)
    pad_rows = jnp.where(
        lax.broadcasted_iota(jnp.int32, (V_ROWS - HEAD_DIM, tm), 0) == 0, 1.0, 0.0)
    for hh in range(N_HEADS):
        v_ref[hh] = jnp.concatenate(
            [vt[hh * HEAD_DIM:(hh + 1) * HEAD_DIM], pad_rows], axis=0).astype(BF16)

    kk = jnp.dot(h, wk_ref[...], preferred_element_type=F32)
    qt = jnp.dot(wq_ref[...], ht, preferred_element_type=F32) * Q_SCALE

    logf = (jnp.minimum(f, 0.0) - jnp.log1p(jnp.exp(-jnp.abs(f)))) * LOG2E
    row = lax.broadcasted_iota(jnp.int32, (tm, tm), 0)
    col = lax.broadcasted_iota(jnp.int32, (tm, tm), 1)
    triu = (row <= col).astype(BF16)
    c = carry_ref[...]
    for part in _split3(logf):
        c = c + jnp.dot(part.astype(BF16), triu, preferred_element_type=F32)
    carry_ref[...] = c[:, tm - 1:tm]
    chi, cmid, clo = _split3(c)

    sub = lax.broadcasted_iota(jnp.int32, (HEAD_PAD, tm), 0)
    bias_t = jnp.where(sub < N_HEADS, chi,
                       jnp.where(sub < 2 * N_HEADS, cmid,
                                 jnp.where(sub < 3 * N_HEADS, clo,
                                           jnp.where(sub == ROW_ONES, 1.0, 0.0))))
    bias = bias_t.T.astype(BF16)
    kb = jnp.dot(bias, pk_ref[...], preferred_element_type=F32)
    low = lax.broadcasted_iota(jnp.int32, (tm, HEAD_PAD), 1) < HEAD_DIM
    for pr in range(N_HEADS // 2):
        pair = kk[:, pr * HEAD_PAD:(pr + 1) * HEAD_PAD]
        for odd in range(2):
            hh = 2 * pr + odd
            feats = pltpu.roll(pair, HEAD_DIM, axis=1) if odd else pair
            k_ref[hh] = (jnp.where(low, feats, 0.0)
                         + kb[:, hh * HEAD_PAD:(hh + 1) * HEAD_PAD]).astype(BF16)

    slot = lax.broadcasted_iota(jnp.int32, (HEAD_PAD - HEAD_DIM, tm), 0) + HEAD_DIM
    ones_slots = functools.reduce(jnp.logical_or, [slot == s for s in SLOT_CK])
    for hh in range(N_HEADS):
        bias_rows = jnp.where(ones_slots, 1.0, 0.0)
        for part, c_part in zip(SLOT_CQ, (chi, cmid, clo)):
            bias_rows = jnp.where(slot == part, c_part[hh:hh + 1], bias_rows)
        q_ref[hh] = jnp.concatenate(
            [qt[hh * HEAD_DIM:(hh + 1) * HEAD_DIM], bias_rows], axis=0).astype(BF16)


def _attn_in(x, g, wq_t, wk, wv_t, wf_t, bf_col, pk, *, tm=ROW_TILE):
    b, s, d = x.shape
    const = lambda bi, i: (0, 0)
    return pl.pallas_call(
        functools.partial(_attn_in_kernel, tm=tm),
        grid=(b, s // tm),
        in_specs=[
            pl.BlockSpec((None, tm, d), lambda bi, i: (bi, i, 0)),
            pl.BlockSpec((1, d), const),
            pl.BlockSpec(wq_t.shape, const),
            pl.BlockSpec(wk.shape, const),
            pl.BlockSpec(wv_t.shape, const),
            pl.BlockSpec(wf_t.shape, const),
            pl.BlockSpec(bf_col.shape, const),
            pl.BlockSpec(pk.shape, const),
        ],
        out_specs=[
            pl.BlockSpec((None, N_HEADS, HEAD_PAD, tm), lambda bi, i: (bi, 0, 0, i)),
            pl.BlockSpec((None, N_HEADS, tm, HEAD_PAD), lambda bi, i: (bi, 0, i, 0)),
            pl.BlockSpec((None, N_HEADS, None, V_ROWS, tm),
                         lambda bi, i: (bi, 0, i // (V_BLOCK // tm), 0, i % (V_BLOCK // tm))),
        ],
        out_shape=[
            jax.ShapeDtypeStruct((b, N_HEADS, HEAD_PAD, s), BF16),
            jax.ShapeDtypeStruct((b, N_HEADS, s, HEAD_PAD), BF16),
            jax.ShapeDtypeStruct((b, N_HEADS, s // V_BLOCK, V_ROWS, V_BLOCK), BF16),
        ],
        scratch_shapes=[pltpu.VMEM((HEAD_PAD, 1), F32)],
        compiler_params=pltpu.CompilerParams(
            dimension_semantics=("arbitrary", "arbitrary"),
            vmem_limit_bytes=VMEM_LIMIT_BYTES),
        name="attn_in",
    )(x, g, wq_t, wk, wv_t, wf_t, bf_col, pk)


def _flash_kernel(q_ref, qnext_ref, k_ref, v_ref, o_ref, s_refs, smax_refs, m_ref, acc_ref,
                  *, bk):
    i = pl.program_id(2)
    bq = 2 * bk
    sub = bk // V_BLOCK
    n_tiles = bq // QUERY_TILE
    m_ref[...] = jnp.full_like(m_ref, -jnp.inf)
    acc_ref[...] = jnp.zeros_like(acc_ref)

    def scores_tile(t, slot, c, rows=bk, queries=q_ref):
        cols = slice(c * QUERY_TILE, (c + 1) * QUERY_TILE)
        start = pl.multiple_of(t * bk, bk)
        s_max = None
        for r0 in range(0, rows, SCORE_ROWS):
            n = min(SCORE_ROWS, rows - r0)
            s = jnp.dot(k_ref[pl.ds(start + r0, n), :], queries[:, cols],
                        preferred_element_type=F32)
            s_refs[slot][r0:r0 + n, cols] = s
            part = jnp.max(s, axis=0, keepdims=True)
            s_max = part if s_max is None else jnp.maximum(s_max, part)
        smax_refs[slot][:, cols] = s_max

    def consume_tile(t, slot, c, rows=bk, key_offset=None):
        cols = slice(c * QUERY_TILE, (c + 1) * QUERY_TILE)
        s = s_refs[slot][0:rows, cols]
        s_max = smax_refs[slot][:, cols]
        if key_offset is not None:
            key = lax.broadcasted_iota(jnp.int32, (rows, QUERY_TILE), 0) + key_offset
            qry = lax.broadcasted_iota(jnp.int32, (rows, QUERY_TILE), 1) + c * QUERY_TILE
            s = jnp.where(key <= qry, s, NEG_INF)
            s_max = jnp.max(s, axis=0, keepdims=True)
        m_prev = m_ref[:, cols]
        m_new = jnp.maximum(m_prev, s_max)
        alpha = jnp.exp2(m_prev - m_new)
        p = jnp.exp2(s - m_new).astype(BF16)
        pv = None
        for u in range(pl.cdiv(rows, V_BLOCK)):
            n = min(V_BLOCK, rows - u * V_BLOCK)
            part = jnp.dot(v_ref[t * sub + u, :, 0:n], p[u * V_BLOCK:u * V_BLOCK + n, :],
                           preferred_element_type=F32)
            pv = part if pv is None else pv + part
        acc_ref[:, cols] = alpha * acc_ref[:, cols] + pv
        m_ref[:, cols] = m_new

    diag = []
    for r in range(2):
        for c in range(n_tiles):
            rows = min(bk, (c + 1) * QUERY_TILE - r * bk)
            if rows > 0:
                diag.append((r, c, rows, rows < bk or c * QUERY_TILE < (r + 1) * bk))

    la = SCORE_LOOKAHEAD

    @pl.when(i == 0)
    def _():
        for r, c, rows, _ in diag[:la]:
            scores_tile(r, r, c, rows)

    def body(g, carry):
        t = 2 * g
        for half in range(2):
            for c in range(n_tiles):
                consume_tile(t + half, half, c)
                ahead = half + (c + la) // n_tiles
                scores_tile(t + ahead, ahead % 2, (c + la) % n_tiles)
        return carry

    lax.fori_loop(0, i, body, 0)
    t = 2 * i
    for idx, (r, c, rows, masked) in enumerate(diag):
        consume_tile(t + r, r, c, rows, key_offset=r * bk if masked else None)
        if idx + la < len(diag):
            r2, c2, rows2, _ = diag[idx + la]
            scores_tile(t + r2, r2, c2, rows2)
        else:
            scores_tile(0, 0, idx + la - len(diag), queries=qnext_ref)

    acc = acc_ref[...]
    o_ref[...] = (acc[0:HEAD_DIM] / acc[V_ONES_ROW:V_ONES_ROW + 1]).astype(o_ref.dtype)


def _flash(q_t, k, v_t, *, bk=ATTN_BLOCK):
    b, nh, s, hp = k.shape
    bq = 2 * bk
    assert v_t.shape == (b, nh, s // V_BLOCK, V_ROWS, V_BLOCK)
    score_bufs = [pltpu.VMEM((bk, bq), F32) for _ in range(2)]
    smax_bufs = [pltpu.VMEM((1, bq), F32) for _ in range(2)]
    n_steps = s // bq
    return pl.pallas_call(
        functools.partial(_flash_kernel, bk=bk),
        grid=(b, nh, s // bq),
        in_specs=[
            pl.BlockSpec((None, None, hp, bq), lambda bi, h, i: (bi, h, 0, i)),
            pl.BlockSpec((None, None, hp, bq),
                         lambda bi, h, i: (bi, h, 0, jnp.minimum(i + 1, n_steps - 1))),
            pl.BlockSpec((None, None, s, hp), lambda bi, h, i: (bi, h, 0, 0)),
            pl.BlockSpec((None, None, s // V_BLOCK, V_ROWS, V_BLOCK),
                         lambda bi, h, i: (bi, h, 0, 0, 0)),
        ],
        out_specs=pl.BlockSpec((None, None, HEAD_DIM, bq), lambda bi, h, i: (bi, h, 0, i)),
        out_shape=jax.ShapeDtypeStruct((b, nh, HEAD_DIM, s), BF16),
        scratch_shapes=[score_bufs, smax_bufs,
                        pltpu.VMEM((1, bq), F32), pltpu.VMEM((V_ROWS, bq), F32)],
        compiler_params=pltpu.CompilerParams(
            dimension_semantics=("arbitrary", "arbitrary", "arbitrary"),
            vmem_limit_bytes=VMEM_LIMIT_BYTES),
        name="fox_flash",
    )(q_t, q_t, k, v_t)


HALO = 8


def _conv_kernel(x_ref, g0_ref, win_ref, cw_ref, wout_ref, g1_ref, y_ref, z_ref, *, tm):
    d = D_MODEL

    @pl.when(pl.program_id(1) == 0)
    def _():
        z_ref[0:HALO, :] = jnp.zeros((HALO, d), F32)

    @pl.when(pl.program_id(1) > 0)
    def _():
        z_ref[0:HALO, :] = z_ref[tm:tm + HALO, :]

    def in_proj(r0):
        x = x_ref[r0:r0 + ROW_TILE, :]
        h = _rms(x, g0_ref[...]).astype(BF16)
        c_gate = jnp.dot(h, win_ref[:, d:2 * d], preferred_element_type=F32)
        u = jnp.dot(h, win_ref[:, 2 * d:3 * d], preferred_element_type=F32)
        z_ref[HALO + r0:HALO + r0 + ROW_TILE, :] = c_gate * u
        return x, jnp.dot(h, win_ref[:, 0:d], preferred_element_type=F32)

    def out_proj(r0, x, b_gate):
        cw = cw_ref[...]
        zc = sum(cw[k:k + 1, :] * z_ref[pl.ds(HALO + r0 - (CONV_WIDTH - 1 - k), ROW_TILE), :]
                 for k in range(CONV_WIDTH))
        y = (b_gate * zc).astype(BF16)
        m = jnp.dot(y, wout_ref[...], preferred_element_type=F32)
        y_ref[r0:r0 + ROW_TILE, :] = x + _rms(m, g1_ref[...])

    starts = range(0, tm, ROW_TILE)
    staged = [in_proj(r0) for r0 in starts]
    for r0, (x, b_gate) in zip(starts, staged):
        out_proj(r0, x, b_gate)


def _conv_layer(x, g0, w_in, conv_w, w_out, g1, *, tm=2 * ROW_TILE):
    b, s, d = x.shape
    const = lambda bi, i: (0, 0)
    resident = lambda a: pl.BlockSpec(a.shape, const, pipeline_mode=pl.Buffered(1))
    return pl.pallas_call(
        functools.partial(_conv_kernel, tm=tm),
        grid=(b, s // tm),
        in_specs=[
            pl.BlockSpec((None, tm, d), lambda bi, i: (bi, i, 0)),
            pl.BlockSpec((1, d), const),
            resident(w_in),
            pl.BlockSpec(conv_w.shape, const),
            resident(w_out),
            pl.BlockSpec((1, d), const),
        ],
        out_specs=pl.BlockSpec((None, tm, d), lambda bi, i: (bi, i, 0)),
        out_shape=jax.ShapeDtypeStruct(x.shape, F32),
        scratch_shapes=[pltpu.VMEM((tm + HALO, d), F32)],
        compiler_params=pltpu.CompilerParams(
            dimension_semantics=("arbitrary", "arbitrary"),
            vmem_limit_bytes=VMEM_LIMIT_BYTES),
        name="conv_mixer",
    )(x, g0, w_in, conv_w, w_out, g1)


FF_CHUNK = 1024
TAIL_SUB_TILE = 256


def _tail_kernel(*refs, attn_out):
    if attn_out:
        o_ref, wo_ref, *refs = refs
    x_ref, p_ref, g_ref, wup_ref, wdn_ref, wg_ref, wp_ref, y_ref = refs
    d_ff = wup_ref.shape[1]
    tiles = [slice(r0, r0 + TAIL_SUB_TILE) for r0 in range(0, x_ref.shape[0], TAIL_SUB_TILE)]

    def mixer_out(rows):
        x = x_ref[rows, :]
        if not attn_out:
            return x
        o_t = jnp.concatenate([o_ref[hh, :, rows] for hh in range(N_HEADS)], axis=0)
        m_t = jnp.dot(wo_ref[...], o_t, preferred_element_type=F32)
        return x + _rms(m_t.T, g_ref[0:1, :])

    def mlp(x):
        h = _rms(x, g_ref[1:2, :]).astype(BF16)
        f = None
        for c in range(d_ff // FF_CHUNK):
            cols = slice(c * FF_CHUNK, (c + 1) * FF_CHUNK)
            u = jnp.dot(h, wup_ref[:, cols], preferred_element_type=F32)
            a = jnp.square(jnp.maximum(u, 0.0)).astype(BF16)
            part = jnp.dot(a, wdn_ref[cols, :], preferred_element_type=F32)
            f = part if f is None else f + part
        return x + _rms(f, g_ref[2:3, :])

    def ple(rows, x):
        h = _rms(x, g_ref[3:4, :]).astype(BF16)
        gate = jax.nn.sigmoid(jnp.dot(h, wg_ref[...], preferred_element_type=F32))
        e = jnp.dot(p_ref[rows, :].astype(BF16), wp_ref[...],
                    preferred_element_type=F32) * gate
        y_ref[rows, :] = x + _rms(e, g_ref[4:5, :])

    xs = [mixer_out(rows) for rows in tiles]
    xs = [mlp(x) for x in xs]
    for rows, x in zip(tiles, xs):
        ple(rows, x)


def _layer_tail(x, p, layer, gains, w_up, w_down, w_gate, w_proj, o_t=None, w_out_t=None,
                *, tm=2 * TAIL_SUB_TILE):
    b, s, d = x.shape
    const = lambda bi, i: (0, 0)
    resident = lambda a: pl.BlockSpec(a.shape, const, pipeline_mode=pl.Buffered(1))
    attn_out = o_t is not None
    operands, specs = [], []
    if attn_out:
        operands += [o_t, w_out_t]
        specs += [pl.BlockSpec((None, N_HEADS, HEAD_DIM, tm), lambda bi, i: (bi, 0, 0, i)),
                  resident(w_out_t)]
    operands += [x, p, gains, w_up, w_down, w_gate, w_proj]
    specs += [
        pl.BlockSpec((None, tm, d), lambda bi, i: (bi, i, 0)),
        pl.BlockSpec((None, None, tm, p.shape[-1]), lambda bi, i: (layer, bi, i, 0)),
        pl.BlockSpec(gains.shape, const),
        resident(w_up), resident(w_down), resident(w_gate), resident(w_proj),
    ]
    return pl.pallas_call(
        functools.partial(_tail_kernel, attn_out=attn_out),
        grid=(b, s // tm),
        in_specs=specs,
        out_specs=pl.BlockSpec((None, tm, d), lambda bi, i: (bi, i, 0)),
        out_shape=jax.ShapeDtypeStruct(x.shape, F32),
        compiler_params=pltpu.CompilerParams(
            dimension_semantics=("arbitrary", "arbitrary"),
            vmem_limit_bytes=VMEM_LIMIT_BYTES),
        name="layer_tail",
    )(*operands)


def _prep_attn_weights(w_in, b_f, w_out):
    d = D_MODEL
    wq_t = w_in[:, 0:d].T.astype(BF16)
    wk = w_in[:, d:2 * d].astype(BF16)
    wv_t = w_in[:, 2 * d:3 * d].T.astype(BF16)
    wf = w_in[:, 3 * d:]
    wf_t = jnp.pad(jnp.concatenate([wf, wf, wf], axis=1),
                   ((0, 0), (0, HEAD_PAD - 3 * N_HEADS))).T.astype(BF16)
    bf_col = jnp.pad(jnp.concatenate([b_f, b_f, b_f]), (0, HEAD_PAD - 3 * N_HEADS))
    bf_col = bf_col.reshape(HEAD_PAD, 1).astype(F32)
    return wq_t, wk, wv_t, wf_t, bf_col, w_out.T.astype(BF16)


def kernel(x, p, norm_g, w_attn_in, b_forget, w_attn_out, w_conv_in, conv_w, w_conv_out,
           w_mlp_up, w_mlp_down, w_ple_proj, w_ple_gate):
    depth = norm_g.shape[0]
    pk = jnp.asarray(_k_placement_matrix(), dtype=BF16)
    for i in range(depth):
        g = norm_g[i].astype(F32)
        gi = lambda n: g[n:n + 1]
        j = i // 2
        if i % 2 == 0:
            wq_t, wk, wv_t, wf_t, bf_col, w_out_t = _prep_attn_weights(
                w_attn_in[j], b_forget[j], w_attn_out[j])
            q_t, k, v_t = _attn_in(x, gi(0), wq_t, wk, wv_t, wf_t, bf_col, pk)
            mixer = dict(o_t=_flash(q_t, k, v_t), w_out_t=w_out_t)
        else:
            x = _conv_layer(x, gi(0), w_conv_in[j].astype(BF16), conv_w[j].astype(F32),
                            w_conv_out[j].astype(BF16), gi(1))
            mixer = {}
        x = _layer_tail(x, p, i, g[1:6], w_mlp_up[i].astype(BF16),
                        w_mlp_down[i].astype(BF16), w_ple_gate[i].astype(BF16),
                        w_ple_proj[i].astype(BF16), **mixer)
    return x
```

```python
import functools

import numpy as np
import jax
import jax.numpy as jnp
from jax import lax
from jax.experimental import pallas as pl
from jax.experimental.pallas import tpu as pltpu

F32 = jnp.float32
BF16 = jnp.bfloat16

D_MODEL = 1024
N_HEADS = 16
HEAD_DIM = D_MODEL // N_HEADS
HEAD_PAD = 128
QK_PAD = N_HEADS * HEAD_PAD
V_ROWS = 128
V_ONES_ROW = HEAD_DIM
CONV_WIDTH = 3
RMS_EPS = 1e-6
NEG_INF = -1e30
LOG2E = 1.4426950408889634
Q_SCALE = (HEAD_DIM ** -0.5) * LOG2E

VMEM_LIMIT_BYTES = 56 * 1024 * 1024

ROW_TILE = 512
ATTN_BLOCK = 1024
V_BLOCK = ROW_TILE
QUERY_TILE = 256
SCORE_LOOKAHEAD = 3
SCORE_ROWS = 512

SLOT_CQ = (64, 65, 66)
SLOT_CK = (67, 68, 69)
ROW_ONES = 48


def _rms(xf, g):
    ms = jnp.mean(xf * xf, axis=-1, keepdims=True)
    return xf * lax.rsqrt(ms + RMS_EPS) * g


def _split3(x):
    hi = x.astype(BF16).astype(F32)
    r = x - hi
    mid = r.astype(BF16).astype(F32)
    lo = (r - mid).astype(BF16).astype(F32)
    return hi, mid, lo


def _k_placement_matrix():
    pk = np.zeros((HEAD_PAD, QK_PAD), np.float32)
    for h in range(N_HEADS):
        base = h * HEAD_PAD
        for part in range(3):
            pk[part * N_HEADS + h, base + SLOT_CK[part]] = -1.0
            pk[ROW_ONES, base + SLOT_CQ[part]] = 1.0
    return pk


def _attn_in_kernel(x_ref, g_ref, wq_ref, wk_ref, wv_ref, wf_ref, bf_ref, pk_ref,
                    q_ref, k_ref, v_ref, carry_ref, *, tm):
    @pl.when(pl.program_id(1) == 0)
    def _():
        carry_ref[...] = jnp.zeros_like(carry_ref)

    hn = _rms(x_ref[...], g_ref[...])
    h = hn.astype(BF16)
    ht = hn.T.astype(BF16)

    f = jnp.dot(wf_ref[...], ht, preferred_element_type=F32) + bf_ref[...]

    vt = jnp.dot(wv_ref[...], ht, preferred_element_type=F32)
    pad_rows = jnp.where(
        lax.broadcasted_iota(jnp.int32, (V_ROWS - HEAD_DIM, tm), 0) == 0, 1.0, 0.0)
    for hh in range(N_HEADS):
        v_ref[hh] = jnp.concatenate(
            [vt[hh * HEAD_DIM:(hh + 1) * HEAD_DIM], pad_rows], axis=0).astype(BF16)

    kk = jnp.dot(h, wk_ref[...], preferred_element_type=F32)
    qt = jnp.dot(wq_ref[...], ht, preferred_element_type=F32) * Q_SCALE

    logf = (jnp.minimum(f, 0.0) - jnp.log1p(jnp.exp(-jnp.abs(f)))) * LOG2E
    row = lax.broadcasted_iota(jnp.int32, (tm, tm), 0)
    col = lax.broadcasted_iota(jnp.int32, (tm, tm), 1)
    triu = (row <= col).astype(BF16)
    c = carry_ref[...]
    for part in _split3(logf):
        c = c + jnp.dot(part.astype(BF16), triu, preferred_element_type=F32)
    carry_ref[...] = c[:, tm - 1:tm]
    chi, cmid, clo = _split3(c)

    sub = lax.broadcasted_iota(jnp.int32, (HEAD_PAD, tm), 0)
    bias_t = jnp.where(sub < N_HEADS, chi,
                       jnp.where(sub < 2 * N_HEADS, cmid,
                                 jnp.where(sub < 3 * N_HEADS, clo,
                                           jnp.where(sub == ROW_ONES, 1.0, 0.0))))
    bias = bias_t.T.astype(BF16)
    kb = jnp.dot(bias, pk_ref[...], preferred_element_type=F32)
    low = lax.broadcasted_iota(jnp.int32, (tm, HEAD_PAD), 1) < HEAD_DIM
    for pr in range(N_HEADS // 2):
        pair = kk[:, pr * HEAD_PAD:(pr + 1) * HEAD_PAD]
        for odd in range(2):
            hh = 2 * pr + odd
            feats = pltpu.roll(pair, HEAD_DIM, axis=1) if odd else pair
            k_ref[hh] = (jnp.where(low, feats, 0.0)
                         + kb[:, hh * HEAD_PAD:(hh + 1) * HEAD_PAD]).astype(BF16)

    slot = lax.broadcasted_iota(jnp.int32, (HEAD_PAD - HEAD_DIM, tm), 0) + HEAD_DIM
    ones_slots = functools.reduce(jnp.logical_or, [slot == s for s in SLOT_CK])
    for hh in range(N_HEADS):
        bias_rows = jnp.where(ones_slots, 1.0, 0.0)
        for part, c_part in zip(SLOT_CQ, (chi, cmid, clo)):
            bias_rows = jnp.where(slot == part, c_part[hh:hh + 1], bias_rows)
        q_ref[hh] = jnp.concatenate(
            [qt[hh * HEAD_DIM:(hh + 1) * HEAD_DIM], bias_rows], axis=0).astype(BF16)


def _attn_in(x, g, wq_t, wk, wv_t, wf_t, bf_col, pk, *, tm=ROW_TILE):
    b, s, d = x.shape
    const = lambda bi, i: (0, 0)
    return pl.pallas_call(
        functools.partial(_attn_in_kernel, tm=tm),
        grid=(b, s // tm),
        in_specs=[
            pl.BlockSpec((None, tm, d), lambda bi, i: (bi, i, 0)),
            pl.BlockSpec((1, d), const),
            pl.BlockSpec(wq_t.shape, const),
            pl.BlockSpec(wk.shape, const),
            pl.BlockSpec(wv_t.shape, const),
            pl.BlockSpec(wf_t.shape, const),
            pl.BlockSpec(bf_col.shape, const),
            pl.BlockSpec(pk.shape, const),
        ],
        out_specs=[
            pl.BlockSpec((None, N_HEADS, HEAD_PAD, tm), lambda bi, i: (bi, 0, 0, i)),
            pl.BlockSpec((None, N_HEADS, tm, HEAD_PAD), lambda bi, i: (bi, 0, i, 0)),
            pl.BlockSpec((None, N_HEADS, None, V_ROWS, tm),
                         lambda bi, i: (bi, 0, i // (V_BLOCK // tm), 0, i % (V_BLOCK // tm))),
        ],
        out_shape=[
            jax.ShapeDtypeStruct((b, N_HEADS, HEAD_PAD, s), BF16),
            jax.ShapeDtypeStruct((b, N_HEADS, s, HEAD_PAD), BF16),
            jax.ShapeDtypeStruct((b, N_HEADS, s // V_BLOCK, V_ROWS, V_BLOCK), BF16),
        ],
        scratch_shapes=[pltpu.VMEM((HEAD_PAD, 1), F32)],
        compiler_params=pltpu.CompilerParams(
            dimension_semantics=("arbitrary", "arbitrary"),
            vmem_limit_bytes=VMEM_LIMIT_BYTES),
        name="attn_in",
    )(x, g, wq_t, wk, wv_t, wf_t, bf_col, pk)


def _flash_kernel(q_ref, qnext_ref, k_ref, v_ref, o_ref, s_refs, smax_refs, m_ref, acc_ref,
                  *, bk):
    i = pl.program_id(2)
    bq = 2 * bk
    sub = bk // V_BLOCK
    n_tiles = bq // QUERY_TILE
    m_ref[...] = jnp.full_like(m_ref, -jnp.inf)
    acc_ref[...] = jnp.zeros_like(acc_ref)

    def scores_tile(t, slot, c, rows=bk, queries=q_ref):
        cols = slice(c * QUERY_TILE, (c + 1) * QUERY_TILE)
        start = pl.multiple_of(t * bk, bk)
        s_max = None
        for r0 in range(0, rows, SCORE_ROWS):
            n = min(SCORE_ROWS, rows - r0)
            s = jnp.dot(k_ref[pl.ds(start + r0, n), :], queries[:, cols],
                        preferred_element_type=F32)
            s_refs[slot][r0:r0 + n, cols] = s
            part = jnp.max(s, axis=0, keepdims=True)
            s_max = part if s_max is None else jnp.maximum(s_max, part)
        smax_refs[slot][:, cols] = s_max

    def consume_tile(t, slot, c, rows=bk, key_offset=None):
        cols = slice(c * QUERY_TILE, (c + 1) * QUERY_TILE)
        s = s_refs[slot][0:rows, cols]
        s_max = smax_refs[slot][:, cols]
        if key_offset is not None:
            key = lax.broadcasted_iota(jnp.int32, (rows, QUERY_TILE), 0) + key_offset
            qry = lax.broadcasted_iota(jnp.int32, (rows, QUERY_TILE), 1) + c * QUERY_TILE
            s = jnp.where(key <= qry, s, NEG_INF)
            s_max = jnp.max(s, axis=0, keepdims=True)
        m_prev = m_ref[:, cols]
        m_new = jnp.maximum(m_prev, s_max)
        alpha = jnp.exp2(m_prev - m_new)
        p = jnp.exp2(s - m_new).astype(BF16)
        pv = None
        for u in range(pl.cdiv(rows, V_BLOCK)):
            n = min(V_BLOCK, rows - u * V_BLOCK)
            part = jnp.dot(v_ref[t * sub + u, :, 0:n], p[u * V_BLOCK:u * V_BLOCK + n, :],
                           preferred_element_type=F32)
            pv = part if pv is None else pv + part
        acc_ref[:, cols] = alpha * acc_ref[:, cols] + pv
        m_ref[:, cols] = m_new

    diag = []
    for r in range(2):
        for c in range(n_tiles):
            rows = min(bk, (c + 1) * QUERY_TILE - r * bk)
            if rows > 0:
                diag.append((r, c, rows, rows < bk or c * QUERY_TILE < (r + 1) * bk))

    la = SCORE_LOOKAHEAD

    @pl.when(i == 0)
    def _():
        for r, c, rows, _ in diag[:la]:
            scores_tile(r, r, c, rows)

    def body(g, carry):
        t = 2 * g
        for half in range(2):
            for c in range(n_tiles):
                consume_tile(t + half, half, c)
                ahead = half + (c + la) // n_tiles
                scores_tile(t + ahead, ahead % 2, (c + la) % n_tiles)
        return carry

    lax.fori_loop(0, i, body, 0)
    t = 2 * i
    for idx, (r, c, rows, masked) in enumerate(diag):
        consume_tile(t + r, r, c, rows, key_offset=r * bk if masked else None)
        if idx + la < len(diag):
            r2, c2, rows2, _ = diag[idx + la]
            scores_tile(t + r2, r2, c2, rows2)
        else:
            scores_tile(0, 0, idx + la - len(diag), queries=qnext_ref)

    acc = acc_ref[...]
    o_ref[...] = (acc[0:HEAD_DIM] / acc[V_ONES_ROW:V_ONES_ROW + 1]).astype(o_ref.dtype)


def _flash(q_t, k, v_t, *, bk=ATTN_BLOCK):
    b, nh, s, hp = k.shape
    bq = 2 * bk
    assert v_t.shape == (b, nh, s // V_BLOCK, V_ROWS, V_BLOCK)
    score_bufs = [pltpu.VMEM((bk, bq), F32) for _ in range(2)]
    smax_bufs = [pltpu.VMEM((1, bq), F32) for _ in range(2)]
    n_steps = s // bq
    return pl.pallas_call(
        functools.partial(_flash_kernel, bk=bk),
        grid=(b, nh, s // bq),
        in_specs=[
            pl.BlockSpec((None, None, hp, bq), lambda bi, h, i: (bi, h, 0, i)),
            pl.BlockSpec((None, None, hp, bq),
                         lambda bi, h, i: (bi, h, 0, jnp.minimum(i + 1, n_steps - 1))),
            pl.BlockSpec((None, None, s, hp), lambda bi, h, i: (bi, h, 0, 0)),
            pl.BlockSpec((None, None, s // V_BLOCK, V_ROWS, V_BLOCK),
                         lambda bi, h, i: (bi, h, 0, 0, 0)),
        ],
        out_specs=pl.BlockSpec((None, None, HEAD_DIM, bq), lambda bi, h, i: (bi, h, 0, i)),
        out_shape=jax.ShapeDtypeStruct((b, nh, HEAD_DIM, s), BF16),
        scratch_shapes=[score_bufs, smax_bufs,
                        pltpu.VMEM((1, bq), F32), pltpu.VMEM((V_ROWS, bq), F32)],
        compiler_params=pltpu.CompilerParams(
            dimension_semantics=("arbitrary", "arbitrary", "arbitrary"),
            vmem_limit_bytes=VMEM_LIMIT_BYTES),
        name="fox_flash",
    )(q_t, q_t, k, v_t)


HALO = 8
CONV_SUB_TILE = 256


def _conv_kernel(x_ref, g0_ref, win_ref, cw_ref, wout_ref, g1_ref, y_ref, z_ref, *, tm):
    d = D_MODEL

    @pl.when(pl.program_id(1) == 0)
    def _():
        z_ref[0:HALO, :] = jnp.zeros((HALO, d), F32)

    @pl.when(pl.program_id(1) > 0)
    def _():
        z_ref[0:HALO, :] = z_ref[tm:tm + HALO, :]

    ts = CONV_SUB_TILE

    def in_proj(r0):
        x = x_ref[r0:r0 + ts, :]
        h = _rms(x, g0_ref[...]).astype(BF16)
        c_gate = jnp.dot(h, win_ref[:, d:2 * d], preferred_element_type=F32)
        u = jnp.dot(h, win_ref[:, 2 * d:3 * d], preferred_element_type=F32)
        z_ref[HALO + r0:HALO + r0 + ts, :] = c_gate * u
        return x, jnp.dot(h, win_ref[:, 0:d], preferred_element_type=F32)

    def out_proj(r0, x, b_gate):
        cw = cw_ref[...]
        zc = sum(cw[k:k + 1, :] * z_ref[pl.ds(HALO + r0 - (CONV_WIDTH - 1 - k), ts), :]
                 for k in range(CONV_WIDTH))
        y = (b_gate * zc).astype(BF16)
        m = jnp.dot(y, wout_ref[...], preferred_element_type=F32)
        y_ref[r0:r0 + ts, :] = x + _rms(m, g1_ref[...])

    starts = range(0, tm, ts)
    staged = [in_proj(r0) for r0 in starts]
    for r0, (x, b_gate) in zip(starts, staged):
        out_proj(r0, x, b_gate)


def _conv_layer(x, g0, w_in, conv_w, w_out, g1, *, tm=2 * ROW_TILE):
    b, s, d = x.shape
    const = lambda bi, i: (0, 0)
    resident = lambda a: pl.BlockSpec(a.shape, const, pipeline_mode=pl.Buffered(1))
    return pl.pallas_call(
        functools.partial(_conv_kernel, tm=tm),
        grid=(b, s // tm),
        in_specs=[
            pl.BlockSpec((None, tm, d), lambda bi, i: (bi, i, 0)),
            pl.BlockSpec((1, d), const),
            resident(w_in),
            pl.BlockSpec(conv_w.shape, const),
            resident(w_out),
            pl.BlockSpec((1, d), const),
        ],
        out_specs=pl.BlockSpec((None, tm, d), lambda bi, i: (bi, i, 0)),
        out_shape=jax.ShapeDtypeStruct(x.shape, F32),
        scratch_shapes=[pltpu.VMEM((tm + HALO, d), F32)],
        compiler_params=pltpu.CompilerParams(
            dimension_semantics=("arbitrary", "arbitrary"),
            vmem_limit_bytes=VMEM_LIMIT_BYTES),
        name="conv_mixer",
    )(x, g0, w_in, conv_w, w_out, g1)


FF_CHUNK = 1024
TAIL_SUB_TILE = 256


def _tail_kernel(*refs, attn_out):
    if attn_out:
        o_ref, wo_ref, *refs = refs
    x_ref, p_ref, g_ref, wup_ref, wdn_ref, wg_ref, wp_ref, y_ref = refs
    d_ff = wup_ref.shape[1]
    tiles = [slice(r0, r0 + TAIL_SUB_TILE) for r0 in range(0, x_ref.shape[0], TAIL_SUB_TILE)]

    def mixer_out(rows):
        x = x_ref[rows, :]
        if not attn_out:
            return x
        o_t = jnp.concatenate([o_ref[hh, :, rows] for hh in range(N_HEADS)], axis=0)
        m_t = jnp.dot(wo_ref[...], o_t, preferred_element_type=F32)
        return x + _rms(m_t.T, g_ref[0:1, :])

    def mlp(x):
        h = _rms(x, g_ref[1:2, :]).astype(BF16)
        f = None
        for c in range(d_ff // FF_CHUNK):
            cols = slice(c * FF_CHUNK, (c + 1) * FF_CHUNK)
            u = jnp.dot(h, wup_ref[:, cols], preferred_element_type=F32)
            a = jnp.square(jnp.maximum(u, 0.0)).astype(BF16)
            part = jnp.dot(a, wdn_ref[cols, :], preferred_element_type=F32)
            f = part if f is None else f + part
        return x + _rms(f, g_ref[2:3, :])

    def ple(rows, x):
        h = _rms(x, g_ref[3:4, :]).astype(BF16)
        gate = jax.nn.sigmoid(jnp.dot(h, wg_ref[...], preferred_element_type=F32))
        e = jnp.dot(p_ref[rows, :].astype(BF16), wp_ref[...],
                    preferred_element_type=F32) * gate
        y_ref[rows, :] = x + _rms(e, g_ref[4:5, :])

    xs = [mixer_out(rows) for rows in tiles]
    xs = [mlp(x) for x in xs]
    for rows, x in zip(tiles, xs):
        ple(rows, x)


def _layer_tail(x, p, layer, gains, w_up, w_down, w_gate, w_proj, o_t=None, w_out_t=None,
                *, tm=2 * TAIL_SUB_TILE):
    b, s, d = x.shape
    const = lambda bi, i: (0, 0)
    resident = lambda a: pl.BlockSpec(a.shape, const, pipeline_mode=pl.Buffered(1))
    attn_out = o_t is not None
    operands, specs = [], []
    if attn_out:
        operands += [o_t, w_out_t]
        specs += [pl.BlockSpec((None, N_HEADS, HEAD_DIM, tm), lambda bi, i: (bi, 0, 0, i)),
                  resident(w_out_t)]
    operands += [x, p, gains, w_up, w_down, w_gate, w_proj]
    specs += [
        pl.BlockSpec((None, tm, d), lambda bi, i: (bi, i, 0)),
        pl.BlockSpec((None, None, tm, p.shape[-1]), lambda bi, i: (layer, bi, i, 0)),
        pl.BlockSpec(gains.shape, const),
        resident(w_up), resident(w_down), resident(w_gate), resident(w_proj),
    ]
    return pl.pallas_call(
        functools.partial(_tail_kernel, attn_out=attn_out),
        grid=(b, s // tm),
        in_specs=specs,
        out_specs=pl.BlockSpec((None, tm, d), lambda bi, i: (bi, i, 0)),
        out_shape=jax.ShapeDtypeStruct(x.shape, F32),
        compiler_params=pltpu.CompilerParams(
            dimension_semantics=("arbitrary", "arbitrary"),
            vmem_limit_bytes=VMEM_LIMIT_BYTES),
        name="layer_tail",
    )(*operands)


def _prep_attn_weights(w_in, b_f, w_out):
    d = D_MODEL
    wq_t = w_in[:, 0:d].T.astype(BF16)
    wk = w_in[:, d:2 * d].astype(BF16)
    wv_t = w_in[:, 2 * d:3 * d].T.astype(BF16)
    wf = w_in[:, 3 * d:]
    wf_t = jnp.pad(jnp.concatenate([wf, wf, wf], axis=1),
                   ((0, 0), (0, HEAD_PAD - 3 * N_HEADS))).T.astype(BF16)
    bf_col = jnp.pad(jnp.concatenate([b_f, b_f, b_f]), (0, HEAD_PAD - 3 * N_HEADS))
    bf_col = bf_col.reshape(HEAD_PAD, 1).astype(F32)
    return wq_t, wk, wv_t, wf_t, bf_col, w_out.T.astype(BF16)


def kernel(x, p, norm_g, w_attn_in, b_forget, w_attn_out, w_conv_in, conv_w, w_conv_out,
           w_mlp_up, w_mlp_down, w_ple_proj, w_ple_gate):
    depth = norm_g.shape[0]
    pk = jnp.asarray(_k_placement_matrix(), dtype=BF16)
    for i in range(depth):
        g = norm_g[i].astype(F32)
        gi = lambda n: g[n:n + 1]
        j = i // 2
        if i % 2 == 0:
            wq_t, wk, wv_t, wf_t, bf_col, w_out_t = _prep_attn_weights(
                w_attn_in[j], b_forget[j], w_attn_out[j])
            q_t, k, v_t = _attn_in(x, gi(0), wq_t, wk, wv_t, wf_t, bf_col, pk)
            mixer = dict(o_t=_flash(q_t, k, v_t), w_out_t=w_out_t)
        else:
            x = _conv_layer(x, gi(0), w_conv_in[j].astype(BF16), conv_w[j].astype(F32),
                            w_conv_out[j].astype(BF16), gi(1))
            mixer = {}
        x = _layer_tail(x, p, i, g[1:6], w_mlp_up[i].astype(BF16),
                        w_mlp_down[i].astype(BF16), w_ple_gate[i].astype(BF16),
                        w_ple_proj[i].astype(BF16), **mixer)
    return x
```

```python
import functools

import numpy as np
import jax
import jax.numpy as jnp
from jax import lax
from jax.experimental import pallas as pl
from jax.experimental.pallas import tpu as pltpu

F32 = jnp.float32
BF16 = jnp.bfloat16

D_MODEL = 1024
N_HEADS = 16
HEAD_DIM = D_MODEL // N_HEADS
HEAD_PAD = 128
QK_PAD = N_HEADS * HEAD_PAD
V_ROWS = 128
V_ONES_ROW = HEAD_DIM
CONV_WIDTH = 3
RMS_EPS = 1e-6
NEG_INF = -1e30
LOG2E = 1.4426950408889634
Q_SCALE = (HEAD_DIM ** -0.5) * LOG2E

VMEM_LIMIT_BYTES = 56 * 1024 * 1024

ROW_TILE = 512
ATTN_BLOCK = 2048
V_BLOCK = ROW_TILE
QUERY_TILE = 256
SCORE_LOOKAHEAD = 4
SCORE_ROWS = 512
SCORE_RING = 8

SLOT_CQ = (64, 65, 66)
SLOT_CK = (67, 68, 69)
ROW_ONES = 48


def _rms(xf, g):
    ms = jnp.mean(xf * xf, axis=-1, keepdims=True)
    return xf * lax.rsqrt(ms + RMS_EPS) * g


def _split3(x):
    hi = x.astype(BF16).astype(F32)
    r = x - hi
    mid = r.astype(BF16).astype(F32)
    lo = (r - mid).astype(BF16).astype(F32)
    return hi, mid, lo


def _k_placement_matrix():
    pk = np.zeros((HEAD_PAD, QK_PAD), np.float32)
    for h in range(N_HEADS):
        base = h * HEAD_PAD
        for part in range(3):
            pk[part * N_HEADS + h, base + SLOT_CK[part]] = -1.0
            pk[ROW_ONES, base + SLOT_CQ[part]] = 1.0
    return pk


def _attn_in_kernel(x_ref, g_ref, wq_ref, wk_ref, wv_ref, wf_ref, bf_ref, pk_ref,
                    q_ref, k_ref, v_ref, carry_ref, *, tm):
    @pl.when(pl.program_id(1) == 0)
    def _():
        carry_ref[...] = jnp.zeros_like(carry_ref)

    hn = _rms(x_ref[...], g_ref[...])
    h = hn.astype(BF16)
    ht = hn.T.astype(BF16)

    f = jnp.dot(wf_ref[...], ht, preferred_element_type=F32) + bf_ref[...]

    vt = jnp.dot(wv_ref[...], ht, preferred_element_type=F32)
    pad_rows = jnp.where(
        lax.broadcasted_iota(jnp.int32, (V_ROWS - HEAD_DIM, tm), 0) == 0, 1.0, 0.0)
    for hh in range(N_HEADS):
        v_ref[hh] = jnp.concatenate(
            [vt[hh * HEAD_DIM:(hh + 1) * HEAD_DIM], pad_rows], axis=0).astype(BF16)

    kk = jnp.dot(h, wk_ref[...], preferred_element_type=F32)
    qt = jnp.dot(wq_ref[...], ht, preferred_element_type=F32) * Q_SCALE

    logf = (jnp.minimum(f, 0.0) - jnp.log1p(jnp.exp(-jnp.abs(f)))) * LOG2E
    row = lax.broadcasted_iota(jnp.int32, (tm, tm), 0)
    col = lax.broadcasted_iota(jnp.int32, (tm, tm), 1)
    triu = (row <= col).astype(BF16)
    c = carry_ref[...]
    for part in _split3(logf):
        c = c + jnp.dot(part.astype(BF16), triu, preferred_element_type=F32)
    carry_ref[...] = c[:, tm - 1:tm]
    chi, cmid, clo = _split3(c)

    sub = lax.broadcasted_iota(jnp.int32, (HEAD_PAD, tm), 0)
    bias_t = jnp.where(sub < N_HEADS, chi,
                       jnp.where(sub < 2 * N_HEADS, cmid,
                                 jnp.where(sub < 3 * N_HEADS, clo,
                                           jnp.where(sub == ROW_ONES, 1.0, 0.0))))
    bias = bias_t.T.astype(BF16)
    kb = jnp.dot(bias, pk_ref[...], preferred_element_type=F32)
    low = lax.broadcasted_iota(jnp.int32, (tm, HEAD_PAD), 1) < HEAD_DIM
    for pr in range(N_HEADS // 2):
        pair = kk[:, pr * HEAD_PAD:(pr + 1) * HEAD_PAD]
        for odd in range(2):
            hh = 2 * pr + odd
            feats = pltpu.roll(pair, HEAD_DIM, axis=1) if odd else pair
            k_ref[hh] = (jnp.where(low, feats, 0.0)
                         + kb[:, hh * HEAD_PAD:(hh + 1) * HEAD_PAD]).astype(BF16)

    slot = lax.broadcasted_iota(jnp.int32, (HEAD_PAD - HEAD_DIM, tm), 0) + HEAD_DIM
    ones_slots = functools.reduce(jnp.logical_or, [slot == s for s in SLOT_CK])
    for hh in range(N_HEADS):
        bias_rows = jnp.where(ones_slots, 1.0, 0.0)
        for part, c_part in zip(SLOT_CQ, (chi, cmid, clo)):
            bias_rows = jnp.where(slot == part, c_part[hh:hh + 1], bias_rows)
        q_ref[hh] = jnp.concatenate(
            [qt[hh * HEAD_DIM:(hh + 1) * HEAD_DIM], bias_rows], axis=0).astype(BF16)


def _attn_in(x, g, wq_t, wk, wv_t, wf_t, bf_col, pk, *, tm=ROW_TILE):
    b, s, d = x.shape
    const = lambda bi, i: (0, 0)
    return pl.pallas_call(
        functools.partial(_attn_in_kernel, tm=tm),
        grid=(b, s // tm),
        in_specs=[
            pl.BlockSpec((None, tm, d), lambda bi, i: (bi, i, 0)),
            pl.BlockSpec((1, d), const),
            pl.BlockSpec(wq_t.shape, const),
            pl.BlockSpec(wk.shape, const),
            pl.BlockSpec(wv_t.shape, const),
            pl.BlockSpec(wf_t.shape, const),
            pl.BlockSpec(bf_col.shape, const),
            pl.BlockSpec(pk.shape, const),
        ],
        out_specs=[
            pl.BlockSpec((None, N_HEADS, HEAD_PAD, tm), lambda bi, i: (bi, 0, 0, i)),
            pl.BlockSpec((None, N_HEADS, tm, HEAD_PAD), lambda bi, i: (bi, 0, i, 0)),
            pl.BlockSpec((None, N_HEADS, None, V_ROWS, tm),
                         lambda bi, i: (bi, 0, i // (V_BLOCK // tm), 0, i % (V_BLOCK // tm))),
        ],
        out_shape=[
            jax.ShapeDtypeStruct((b, N_HEADS, HEAD_PAD, s), BF16),
            jax.ShapeDtypeStruct((b, N_HEADS, s, HEAD_PAD), BF16),
            jax.ShapeDtypeStruct((b, N_HEADS, s // V_BLOCK, V_ROWS, V_BLOCK), BF16),
        ],
        scratch_shapes=[pltpu.VMEM((HEAD_PAD, 1), F32)],
        compiler_params=pltpu.CompilerParams(
            dimension_semantics=("arbitrary", "arbitrary"),
            vmem_limit_bytes=VMEM_LIMIT_BYTES),
        name="attn_in",
    )(x, g, wq_t, wk, wv_t, wf_t, bf_col, pk)


def _flash_kernel(q_ref, qnext_ref, k_ref, v_ref, o_ref, s_tiles, smax_tiles, m_ref, acc_ref,
                  *, bk):
    i = pl.program_id(2)
    bq = 2 * bk
    sub = bk // V_BLOCK
    n_tiles = bq // QUERY_TILE
    m_ref[...] = jnp.full_like(m_ref, -jnp.inf)
    acc_ref[...] = jnp.zeros_like(acc_ref)

    def scores_tile(t, pos, c, rows=bk, queries=q_ref):
        cols = slice(c * QUERY_TILE, (c + 1) * QUERY_TILE)
        start = pl.multiple_of(t * bk, bk)
        s_max = None
        for r0 in range(0, rows, SCORE_ROWS):
            n = min(SCORE_ROWS, rows - r0)
            s = jnp.dot(k_ref[pl.ds(start + r0, n), :], queries[:, cols],
                        preferred_element_type=F32)
            s_tiles[pos % SCORE_RING][r0:r0 + n, :] = s
            part = jnp.max(s, axis=0, keepdims=True)
            s_max = part if s_max is None else jnp.maximum(s_max, part)
        smax_tiles[pos % SCORE_RING][...] = s_max

    def consume_tile(t, pos, c, rows=bk, key_offset=None):
        cols = slice(c * QUERY_TILE, (c + 1) * QUERY_TILE)
        s = s_tiles[pos % SCORE_RING][0:rows, :]
        s_max = smax_tiles[pos % SCORE_RING][...]
        if key_offset is not None:
            key = lax.broadcasted_iota(jnp.int32, (rows, QUERY_TILE), 0) + key_offset
            qry = lax.broadcasted_iota(jnp.int32, (rows, QUERY_TILE), 1) + c * QUERY_TILE
            s = jnp.where(key <= qry, s, NEG_INF)
            s_max = jnp.max(s, axis=0, keepdims=True)
        m_prev = m_ref[:, cols]
        m_new = jnp.maximum(m_prev, s_max)
        alpha = jnp.exp2(m_prev - m_new)
        p = jnp.exp2(s - m_new).astype(BF16)
        pv = None
        for u in range(pl.cdiv(rows, V_BLOCK)):
            n = min(V_BLOCK, rows - u * V_BLOCK)
            part = jnp.dot(v_ref[t * sub + u, :, 0:n], p[u * V_BLOCK:u * V_BLOCK + n, :],
                           preferred_element_type=F32)
            pv = part if pv is None else pv + part
        acc_ref[:, cols] = alpha * acc_ref[:, cols] + pv
        m_ref[:, cols] = m_new

    diag = []
    for r in range(2):
        for c in range(n_tiles):
            rows = min(bk, (c + 1) * QUERY_TILE - r * bk)
            if rows > 0:
                diag.append((r, c, rows, rows < bk or c * QUERY_TILE < (r + 1) * bk))

    la = SCORE_LOOKAHEAD
    assert la < SCORE_RING and n_tiles % SCORE_RING == 0 and len(diag) % SCORE_RING == 0

    @pl.when(i == 0)
    def _():
        for pos, (r, c, rows, _) in enumerate(diag[:la]):
            scores_tile(r, pos, c, rows)

    def body(g, carry):
        t = 2 * g
        for pos in range(2 * n_tiles):
            consume_tile(t + pos // n_tiles, pos, pos % n_tiles)
            ahead = pos + la
            scores_tile(t + ahead // n_tiles, ahead, ahead % n_tiles)
        return carry

    lax.fori_loop(0, i, body, 0)
    t = 2 * i
    for pos, (r, c, rows, masked) in enumerate(diag):
        consume_tile(t + r, pos, c, rows, key_offset=r * bk if masked else None)
        if pos + la < len(diag):
            r2, c2, rows2, _ = diag[pos + la]
            scores_tile(t + r2, pos + la, c2, rows2)
        else:
            scores_tile(0, pos + la, pos + la - len(diag), queries=qnext_ref)

    acc = acc_ref[...]
    o_ref[...] = (acc[0:HEAD_DIM] / acc[V_ONES_ROW:V_ONES_ROW + 1]).astype(o_ref.dtype)


def _flash(q_t, k, v_t, *, bk=ATTN_BLOCK):
    b, nh, s, hp = k.shape
    bq = 2 * bk
    assert v_t.shape == (b, nh, s // V_BLOCK, V_ROWS, V_BLOCK)
    score_tiles = [pltpu.VMEM((bk, QUERY_TILE), F32) for _ in range(SCORE_RING)]
    smax_tiles = [pltpu.VMEM((1, QUERY_TILE), F32) for _ in range(SCORE_RING)]
    n_steps = s // bq
    return pl.pallas_call(
        functools.partial(_flash_kernel, bk=bk),
        grid=(b, nh, s // bq),
        in_specs=[
            pl.BlockSpec((None, None, hp, bq), lambda bi, h, i: (bi, h, 0, i)),
            pl.BlockSpec((None, None, hp, bq),
                         lambda bi, h, i: (bi, h, 0, jnp.minimum(i + 1, n_steps - 1))),
            pl.BlockSpec((None, None, s, hp), lambda bi, h, i: (bi, h, 0, 0)),
            pl.BlockSpec((None, None, s // V_BLOCK, V_ROWS, V_BLOCK),
                         lambda bi, h, i: (bi, h, 0, 0, 0)),
        ],
        out_specs=pl.BlockSpec((None, None, HEAD_DIM, bq), lambda bi, h, i: (bi, h, 0, i)),
        out_shape=jax.ShapeDtypeStruct((b, nh, HEAD_DIM, s), BF16),
        scratch_shapes=[score_tiles, smax_tiles,
                        pltpu.VMEM((1, bq), F32), pltpu.VMEM((V_ROWS, bq), F32)],
        compiler_params=pltpu.CompilerParams(
            dimension_semantics=("arbitrary", "arbitrary", "arbitrary"),
            vmem_limit_bytes=VMEM_LIMIT_BYTES),
        name="fox_flash",
    )(q_t, q_t, k, v_t)


HALO = 8


def _conv_kernel(x_ref, g0_ref, win_ref, cw_ref, wout_ref, g1_ref, y_ref, z_ref, *, tm):
    d = D_MODEL

    @pl.when(pl.program_id(1) == 0)
    def _():
        z_ref[0:HALO, :] = jnp.zeros((HALO, d), F32)

    @pl.when(pl.program_id(1) > 0)
    def _():
        z_ref[0:HALO, :] = z_ref[tm:tm + HALO, :]

    def in_proj(r0):
        x = x_ref[r0:r0 + ROW_TILE, :]
        h = _rms(x, g0_ref[...]).astype(BF16)
        c_gate = jnp.dot(h, win_ref[:, d:2 * d], preferred_element_type=F32)
        u = jnp.dot(h, win_ref[:, 2 * d:3 * d], preferred_element_type=F32)
        z_ref[HALO + r0:HALO + r0 + ROW_TILE, :] = c_gate * u
        return x, jnp.dot(h, win_ref[:, 0:d], preferred_element_type=F32)

    def out_proj(r0, x, b_gate):
        cw = cw_ref[...]
        zc = sum(cw[k:k + 1, :] * z_ref[pl.ds(HALO + r0 - (CONV_WIDTH - 1 - k), ROW_TILE), :]
                 for k in range(CONV_WIDTH))
        y = (b_gate * zc).astype(BF16)
        m = jnp.dot(y, wout_ref[...], preferred_element_type=F32)
        y_ref[r0:r0 + ROW_TILE, :] = x + _rms(m, g1_ref[...])

    starts = range(0, tm, ROW_TILE)
    staged = [in_proj(r0) for r0 in starts]
    for r0, (x, b_gate) in zip(starts, staged):
        out_proj(r0, x, b_gate)


def _conv_layer(x, g0, w_in, conv_w, w_out, g1, *, tm=2 * ROW_TILE):
    b, s, d = x.shape
    const = lambda bi, i: (0, 0)
    resident = lambda a: pl.BlockSpec(a.shape, const, pipeline_mode=pl.Buffered(1))
    return pl.pallas_call(
        functools.partial(_conv_kernel, tm=tm),
        grid=(b, s // tm),
        in_specs=[
            pl.BlockSpec((None, tm, d), lambda bi, i: (bi, i, 0)),
            pl.BlockSpec((1, d), const),
            resident(w_in),
            pl.BlockSpec(conv_w.shape, const),
            resident(w_out),
            pl.BlockSpec((1, d), const),
        ],
        out_specs=pl.BlockSpec((None, tm, d), lambda bi, i: (bi, i, 0)),
        out_shape=jax.ShapeDtypeStruct(x.shape, F32),
        scratch_shapes=[pltpu.VMEM((tm + HALO, d), F32)],
        compiler_params=pltpu.CompilerParams(
            dimension_semantics=("arbitrary", "arbitrary"),
            vmem_limit_bytes=VMEM_LIMIT_BYTES),
        name="conv_mixer",
    )(x, g0, w_in, conv_w, w_out, g1)


FF_CHUNK = 1024
TAIL_SUB_TILE = 256


def _tail_kernel(*refs, attn_out):
    if attn_out:
        o_ref, wo_ref, *refs = refs
    x_ref, p_ref, g_ref, wup_ref, wdn_ref, wg_ref, wp_ref, y_ref = refs
    d_ff = wup_ref.shape[1]
    tiles = [slice(r0, r0 + TAIL_SUB_TILE) for r0 in range(0, x_ref.shape[0], TAIL_SUB_TILE)]

    def mixer_out(rows):
        x = x_ref[rows, :]
        if not attn_out:
            return x
        o_t = jnp.concatenate([o_ref[hh, :, rows] for hh in range(N_HEADS)], axis=0)
        m_t = jnp.dot(wo_ref[...], o_t, preferred_element_type=F32)
        return x + _rms(m_t.T, g_ref[0:1, :])

    def mlp(x):
        h = _rms(x, g_ref[1:2, :]).astype(BF16)
        f = None
        for c in range(d_ff // FF_CHUNK):
            cols = slice(c * FF_CHUNK, (c + 1) * FF_CHUNK)
            u = jnp.dot(h, wup_ref[:, cols], preferred_element_type=F32)
            a = jnp.square(jnp.maximum(u, 0.0)).astype(BF16)
            part = jnp.dot(a, wdn_ref[cols, :], preferred_element_type=F32)
            f = part if f is None else f + part
        return x + _rms(f, g_ref[2:3, :])

    def ple(rows, x):
        h = _rms(x, g_ref[3:4, :]).astype(BF16)
        gate = jax.nn.sigmoid(jnp.dot(h, wg_ref[...], preferred_element_type=F32))
        e = jnp.dot(p_ref[rows, :].astype(BF16), wp_ref[...],
                    preferred_element_type=F32) * gate
        y_ref[rows, :] = x + _rms(e, g_ref[4:5, :])

    xs = [mixer_out(rows) for rows in tiles]
    xs = [mlp(x) for x in xs]
    for rows, x in zip(tiles, xs):
        ple(rows, x)


def _layer_tail(x, p, layer, gains, w_up, w_down, w_gate, w_proj, o_t=None, w_out_t=None,
                *, tm=2 * TAIL_SUB_TILE):
    b, s, d = x.shape
    const = lambda bi, i: (0, 0)
    resident = lambda a: pl.BlockSpec(a.shape, const, pipeline_mode=pl.Buffered(1))
    attn_out = o_t is not None
    operands, specs = [], []
    if attn_out:
        operands += [o_t, w_out_t]
        specs += [pl.BlockSpec((None, N_HEADS, HEAD_DIM, tm), lambda bi, i: (bi, 0, 0, i)),
                  resident(w_out_t)]
    operands += [x, p, gains, w_up, w_down, w_gate, w_proj]
    specs += [
        pl.BlockSpec((None, tm, d), lambda bi, i: (bi, i, 0)),
        pl.BlockSpec((None, None, tm, p.shape[-1]), lambda bi, i: (layer, bi, i, 0)),
        pl.BlockSpec(gains.shape, const),
        resident(w_up), resident(w_down), resident(w_gate), resident(w_proj),
    ]
    return pl.pallas_call(
        functools.partial(_tail_kernel, attn_out=attn_out),
        grid=(b, s // tm),
        in_specs=specs,
        out_specs=pl.BlockSpec((None, tm, d), lambda bi, i: (bi, i, 0)),
        out_shape=jax.ShapeDtypeStruct(x.shape, F32),
        compiler_params=pltpu.CompilerParams(
            dimension_semantics=("arbitrary", "arbitrary"),
            vmem_limit_bytes=VMEM_LIMIT_BYTES),
        name="layer_tail",
    )(*operands)


def _prep_attn_weights(w_in, b_f, w_out):
    d = D_MODEL
    wq_t = w_in[:, 0:d].T.astype(BF16)
    wk = w_in[:, d:2 * d].astype(BF16)
    wv_t = w_in[:, 2 * d:3 * d].T.astype(BF16)
    wf = w_in[:, 3 * d:]
    wf_t = jnp.pad(jnp.concatenate([wf, wf, wf], axis=1),
                   ((0, 0), (0, HEAD_PAD - 3 * N_HEADS))).T.astype(BF16)
    bf_col = jnp.pad(jnp.concatenate([b_f, b_f, b_f]), (0, HEAD_PAD - 3 * N_HEADS))
    bf_col = bf_col.reshape(HEAD_PAD, 1).astype(F32)
    return wq_t, wk, wv_t, wf_t, bf_col, w_out.T.astype(BF16)


def kernel(x, p, norm_g, w_attn_in, b_forget, w_attn_out, w_conv_in, conv_w, w_conv_out,
           w_mlp_up, w_mlp_down, w_ple_proj, w_ple_gate):
    depth = norm_g.shape[0]
    pk = jnp.asarray(_k_placement_matrix(), dtype=BF16)
    for i in range(depth):
        g = norm_g[i].astype(F32)
        gi = lambda n: g[n:n + 1]
        j = i // 2
        if i % 2 == 0:
            wq_t, wk, wv_t, wf_t, bf_col, w_out_t = _prep_attn_weights(
                w_attn_in[j], b_forget[j], w_attn_out[j])
            q_t, k, v_t = _attn_in(x, gi(0), wq_t, wk, wv_t, wf_t, bf_col, pk)
            mixer = dict(o_t=_flash(q_t, k, v_t), w_out_t=w_out_t)
        else:
            x = _conv_layer(x, gi(0), w_conv_in[j].astype(BF16), conv_w[j].astype(F32),
                            w_conv_out[j].astype(BF16), gi(1))
            mixer = {}
        x = _layer_tail(x, p, i, g[1:6], w_mlp_up[i].astype(BF16),
                        w_mlp_down[i].astype(BF16), w_ple_gate[i].astype(BF16),
                        w_ple_proj[i].astype(BF16), **mixer)
    return x
```

```python
import functools

import numpy as np
import jax
import jax.numpy as jnp
from jax import lax
from jax.experimental import pallas as pl
from jax.experimental.pallas import tpu as pltpu

F32 = jnp.float32
BF16 = jnp.bfloat16

D_MODEL = 1024
N_HEADS = 16
HEAD_DIM = D_MODEL // N_HEADS
HEAD_PAD = 128
QK_PAD = N_HEADS * HEAD_PAD
V_ROWS = 128
V_ONES_ROW = HEAD_DIM
CONV_WIDTH = 3
RMS_EPS = 1e-6
NEG_INF = -1e30
LOG2E = 1.4426950408889634
Q_SCALE = (HEAD_DIM ** -0.5) * LOG2E

VMEM_LIMIT_BYTES = 56 * 1024 * 1024

ROW_TILE = 512
ATTN_BLOCK = 2048
V_BLOCK = ROW_TILE
QUERY_TILE = 256
SCORE_LOOKAHEAD = 2
SCORE_ROWS = 512
SCORE_RING = 8

SLOT_CQ = (64, 65, 66)
SLOT_CK = (67, 68, 69)
ROW_ONES = 48


def _rms(xf, g):
    ms = jnp.mean(xf * xf, axis=-1, keepdims=True)
    return xf * lax.rsqrt(ms + RMS_EPS) * g


def _split3(x):
    hi = x.astype(BF16).astype(F32)
    r = x - hi
    mid = r.astype(BF16).astype(F32)
    lo = (r - mid).astype(BF16).astype(F32)
    return hi, mid, lo


def _k_placement_matrix():
    pk = np.zeros((HEAD_PAD, QK_PAD), np.float32)
    for h in range(N_HEADS):
        base = h * HEAD_PAD
        for part in range(3):
            pk[part * N_HEADS + h, base + SLOT_CK[part]] = -1.0
            pk[ROW_ONES, base + SLOT_CQ[part]] = 1.0
    return pk


def _attn_in_kernel(x_ref, g_ref, wq_ref, wk_ref, wv_ref, wf_ref, bf_ref, pk_ref,
                    q_ref, k_ref, v_ref, carry_ref, *, tm):
    @pl.when(pl.program_id(1) == 0)
    def _():
        carry_ref[...] = jnp.zeros_like(carry_ref)

    hn = _rms(x_ref[...], g_ref[...])
    h = hn.astype(BF16)
    ht = hn.T.astype(BF16)

    f = jnp.dot(wf_ref[...], ht, preferred_element_type=F32) + bf_ref[...]

    vt = jnp.dot(wv_ref[...], ht, preferred_element_type=F32)
    pad_rows = jnp.where(
        lax.broadcasted_iota(jnp.int32, (V_ROWS - HEAD_DIM, tm), 0) == 0, 1.0, 0.0)
    for hh in range(N_HEADS):
        v_ref[hh] = jnp.concatenate(
            [vt[hh * HEAD_DIM:(hh + 1) * HEAD_DIM], pad_rows], axis=0).astype(BF16)

    kk = jnp.dot(h, wk_ref[...], preferred_element_type=F32)
    qt = jnp.dot(wq_ref[...], ht, preferred_element_type=F32) * Q_SCALE

    logf = (jnp.minimum(f, 0.0) - jnp.log1p(jnp.exp(-jnp.abs(f)))) * LOG2E
    row = lax.broadcasted_iota(jnp.int32, (tm, tm), 0)
    col = lax.broadcasted_iota(jnp.int32, (tm, tm), 1)
    triu = (row <= col).astype(BF16)
    c = carry_ref[...]
    for part in _split3(logf):
        c = c + jnp.dot(part.astype(BF16), triu, preferred_element_type=F32)
    carry_ref[...] = c[:, tm - 1:tm]
    chi, cmid, clo = _split3(c)

    sub = lax.broadcasted_iota(jnp.int32, (HEAD_PAD, tm), 0)
    bias_t = jnp.where(sub < N_HEADS, chi,
                       jnp.where(sub < 2 * N_HEADS, cmid,
                                 jnp.where(sub < 3 * N_HEADS, clo,
                                           jnp.where(sub == ROW_ONES, 1.0, 0.0))))
    bias = bias_t.T.astype(BF16)
    kb = jnp.dot(bias, pk_ref[...], preferred_element_type=F32)
    low = lax.broadcasted_iota(jnp.int32, (tm, HEAD_PAD), 1) < HEAD_DIM
    for pr in range(N_HEADS // 2):
        pair = kk[:, pr * HEAD_PAD:(pr + 1) * HEAD_PAD]
        for odd in range(2):
            hh = 2 * pr + odd
            feats = pltpu.roll(pair, HEAD_DIM, axis=1) if odd else pair
            k_ref[hh] = (jnp.where(low, feats, 0.0)
                         + kb[:, hh * HEAD_PAD:(hh + 1) * HEAD_PAD]).astype(BF16)

    slot = lax.broadcasted_iota(jnp.int32, (HEAD_PAD - HEAD_DIM, tm), 0) + HEAD_DIM
    ones_slots = functools.reduce(jnp.logical_or, [slot == s for s in SLOT_CK])
    for hh in range(N_HEADS):
        bias_rows = jnp.where(ones_slots, 1.0, 0.0)
        for part, c_part in zip(SLOT_CQ, (chi, cmid, clo)):
            bias_rows = jnp.where(slot == part, c_part[hh:hh + 1], bias_rows)
        q_ref[hh] = jnp.concatenate(
            [qt[hh * HEAD_DIM:(hh + 1) * HEAD_DIM], bias_rows], axis=0).astype(BF16)


def _attn_in(x, g, wq_t, wk, wv_t, wf_t, bf_col, pk, *, tm=ROW_TILE):
    b, s, d = x.shape
    const = lambda bi, i: (0, 0)
    return pl.pallas_call(
        functools.partial(_attn_in_kernel, tm=tm),
        grid=(b, s // tm),
        in_specs=[
            pl.BlockSpec((None, tm, d), lambda bi, i: (bi, i, 0)),
            pl.BlockSpec((1, d), const),
            pl.BlockSpec(wq_t.shape, const),
            pl.BlockSpec(wk.shape, const),
            pl.BlockSpec(wv_t.shape, const),
            pl.BlockSpec(wf_t.shape, const),
            pl.BlockSpec(bf_col.shape, const),
            pl.BlockSpec(pk.shape, const),
        ],
        out_specs=[
            pl.BlockSpec((None, N_HEADS, HEAD_PAD, tm), lambda bi, i: (bi, 0, 0, i)),
            pl.BlockSpec((None, N_HEADS, tm, HEAD_PAD), lambda bi, i: (bi, 0, i, 0)),
            pl.BlockSpec((None, N_HEADS, None, V_ROWS, tm),
                         lambda bi, i: (bi, 0, i // (V_BLOCK // tm), 0, i % (V_BLOCK // tm))),
        ],
        out_shape=[
            jax.ShapeDtypeStruct((b, N_HEADS, HEAD_PAD, s), BF16),
            jax.ShapeDtypeStruct((b, N_HEADS, s, HEAD_PAD), BF16),
            jax.ShapeDtypeStruct((b, N_HEADS, s // V_BLOCK, V_ROWS, V_BLOCK), BF16),
        ],
        scratch_shapes=[pltpu.VMEM((HEAD_PAD, 1), F32)],
        compiler_params=pltpu.CompilerParams(
            dimension_semantics=("arbitrary", "arbitrary"),
            vmem_limit_bytes=VMEM_LIMIT_BYTES),
        name="attn_in",
    )(x, g, wq_t, wk, wv_t, wf_t, bf_col, pk)


def _flash_kernel(q_ref, qnext_ref, k_ref, v_ref, o_ref, s_tiles, smax_tiles, m_ref, acc_ref,
                  *, bk):
    i = pl.program_id(2)
    bq = 2 * bk
    sub = bk // V_BLOCK
    n_tiles = bq // QUERY_TILE
    m_ref[...] = jnp.full_like(m_ref, -jnp.inf)
    acc_ref[...] = jnp.zeros_like(acc_ref)

    def scores_tile(t, pos, c, rows=bk, queries=q_ref):
        cols = slice(c * QUERY_TILE, (c + 1) * QUERY_TILE)
        start = pl.multiple_of(t * bk, bk)
        s_max = None
        for r0 in range(0, rows, SCORE_ROWS):
            n = min(SCORE_ROWS, rows - r0)
            s = jnp.dot(k_ref[pl.ds(start + r0, n), :], queries[:, cols],
                        preferred_element_type=F32)
            s_tiles[pos % SCORE_RING][r0:r0 + n, :] = s
            part = jnp.max(s, axis=0, keepdims=True)
            s_max = part if s_max is None else jnp.maximum(s_max, part)
        smax_tiles[pos % SCORE_RING][...] = s_max

    def consume_tile(t, pos, c, rows=bk, key_offset=None):
        cols = slice(c * QUERY_TILE, (c + 1) * QUERY_TILE)
        s = s_tiles[pos % SCORE_RING][0:rows, :]
        s_max = smax_tiles[pos % SCORE_RING][...]
        if key_offset is not None:
            key = lax.broadcasted_iota(jnp.int32, (rows, QUERY_TILE), 0) + key_offset
            qry = lax.broadcasted_iota(jnp.int32, (rows, QUERY_TILE), 1) + c * QUERY_TILE
            s = jnp.where(key <= qry, s, NEG_INF)
            s_max = jnp.max(s, axis=0, keepdims=True)
        m_prev = m_ref[:, cols]
        m_new = jnp.maximum(m_prev, s_max)
        alpha = jnp.exp2(m_prev - m_new)
        p = jnp.exp2(s - m_new).astype(BF16)
        pv = None
        for u in range(pl.cdiv(rows, V_BLOCK)):
            n = min(V_BLOCK, rows - u * V_BLOCK)
            part = jnp.dot(v_ref[t * sub + u, :, 0:n], p[u * V_BLOCK:u * V_BLOCK + n, :],
                           preferred_element_type=F32)
            pv = part if pv is None else pv + part
        acc_ref[:, cols] = alpha * acc_ref[:, cols] + pv
        m_ref[:, cols] = m_new

    diag = []
    for r in range(2):
        for c in range(n_tiles):
            rows = min(bk, (c + 1) * QUERY_TILE - r * bk)
            if rows > 0:
                diag.append((r, c, rows, rows < bk or c * QUERY_TILE < (r + 1) * bk))

    la = SCORE_LOOKAHEAD
    assert la < SCORE_RING and n_tiles % SCORE_RING == 0 and len(diag) % SCORE_RING == 0

    @pl.when(i == 0)
    def _():
        for pos, (r, c, rows, _) in enumerate(diag[:la]):
            scores_tile(r, pos, c, rows)

    def body(g, carry):
        t = 2 * g
        for pos in range(2 * n_tiles):
            consume_tile(t + pos // n_tiles, pos, pos % n_tiles)
            ahead = pos + la
            scores_tile(t + ahead // n_tiles, ahead, ahead % n_tiles)
        return carry

    lax.fori_loop(0, i, body, 0)
    t = 2 * i
    for pos, (r, c, rows, masked) in enumerate(diag):
        consume_tile(t + r, pos, c, rows, key_offset=r * bk if masked else None)
        if pos + la < len(diag):
            r2, c2, rows2, _ = diag[pos + la]
            scores_tile(t + r2, pos + la, c2, rows2)
        else:
            scores_tile(0, pos + la, pos + la - len(diag), queries=qnext_ref)

    acc = acc_ref[...]
    o_ref[...] = (acc[0:HEAD_DIM] / acc[V_ONES_ROW:V_ONES_ROW + 1]).astype(o_ref.dtype)


def _flash(q_t, k, v_t, *, bk=ATTN_BLOCK):
    b, nh, s, hp = k.shape
    bq = 2 * bk
    assert v_t.shape == (b, nh, s // V_BLOCK, V_ROWS, V_BLOCK)
    score_tiles = [pltpu.VMEM((bk, QUERY_TILE), F32) for _ in range(SCORE_RING)]
    smax_tiles = [pltpu.VMEM((1, QUERY_TILE), F32) for _ in range(SCORE_RING)]
    n_steps = s // bq
    return pl.pallas_call(
        functools.partial(_flash_kernel, bk=bk),
        grid=(b, nh, s // bq),
        in_specs=[
            pl.BlockSpec((None, None, hp, bq), lambda bi, h, i: (bi, h, 0, i)),
            pl.BlockSpec((None, None, hp, bq),
                         lambda bi, h, i: (bi, h, 0, jnp.minimum(i + 1, n_steps - 1))),
            pl.BlockSpec((None, None, s, hp), lambda bi, h, i: (bi, h, 0, 0)),
            pl.BlockSpec((None, None, s // V_BLOCK, V_ROWS, V_BLOCK),
                         lambda bi, h, i: (bi, h, 0, 0, 0)),
        ],
        out_specs=pl.BlockSpec((None, None, HEAD_DIM, bq), lambda bi, h, i: (bi, h, 0, i)),
        out_shape=jax.ShapeDtypeStruct((b, nh, HEAD_DIM, s), BF16),
        scratch_shapes=[score_tiles, smax_tiles,
                        pltpu.VMEM((1, bq), F32), pltpu.VMEM((V_ROWS, bq), F32)],
        compiler_params=pltpu.CompilerParams(
            dimension_semantics=("arbitrary", "arbitrary", "arbitrary"),
            vmem_limit_bytes=VMEM_LIMIT_BYTES),
        name="fox_flash",
    )(q_t, q_t, k, v_t)


HALO = 8


def _conv_kernel(x_ref, g0_ref, win_ref, cw_ref, wout_ref, g1_ref, y_ref, z_ref, *, tm):
    d = D_MODEL

    @pl.when(pl.program_id(1) == 0)
    def _():
        z_ref[0:HALO, :] = jnp.zeros((HALO, d), F32)

    @pl.when(pl.program_id(1) > 0)
    def _():
        z_ref[0:HALO, :] = z_ref[tm:tm + HALO, :]

    def in_proj(r0):
        x = x_ref[r0:r0 + ROW_TILE, :]
        h = _rms(x, g0_ref[...]).astype(BF16)
        c_gate = jnp.dot(h, win_ref[:, d:2 * d], preferred_element_type=F32)
        u = jnp.dot(h, win_ref[:, 2 * d:3 * d], preferred_element_type=F32)
        z_ref[HALO + r0:HALO + r0 + ROW_TILE, :] = c_gate * u
        return x, jnp.dot(h, win_ref[:, 0:d], preferred_element_type=F32)

    def out_proj(r0, x, b_gate):
        cw = cw_ref[...]
        zc = sum(cw[k:k + 1, :] * z_ref[pl.ds(HALO + r0 - (CONV_WIDTH - 1 - k), ROW_TILE), :]
                 for k in range(CONV_WIDTH))
        y = (b_gate * zc).astype(BF16)
        m = jnp.dot(y, wout_ref[...], preferred_element_type=F32)
        y_ref[r0:r0 + ROW_TILE, :] = x + _rms(m, g1_ref[...])

    starts = range(0, tm, ROW_TILE)
    staged = [in_proj(r0) for r0 in starts]
    for r0, (x, b_gate) in zip(starts, staged):
        out_proj(r0, x, b_gate)


def _conv_layer(x, g0, w_in, conv_w, w_out, g1, *, tm=2 * ROW_TILE):
    b, s, d = x.shape
    const = lambda bi, i: (0, 0)
    resident = lambda a: pl.BlockSpec(a.shape, const, pipeline_mode=pl.Buffered(1))
    return pl.pallas_call(
        functools.partial(_conv_kernel, tm=tm),
        grid=(b, s // tm),
        in_specs=[
            pl.BlockSpec((None, tm, d), lambda bi, i: (bi, i, 0)),
            pl.BlockSpec((1, d), const),
            resident(w_in),
            pl.BlockSpec(conv_w.shape, const),
            resident(w_out),
            pl.BlockSpec((1, d), const),
        ],
        out_specs=pl.BlockSpec((None, tm, d), lambda bi, i: (bi, i, 0)),
        out_shape=jax.ShapeDtypeStruct(x.shape, F32),
        scratch_shapes=[pltpu.VMEM((tm + HALO, d), F32)],
        compiler_params=pltpu.CompilerParams(
            dimension_semantics=("arbitrary", "arbitrary"),
            vmem_limit_bytes=VMEM_LIMIT_BYTES),
        name="conv_mixer",
    )(x, g0, w_in, conv_w, w_out, g1)


FF_CHUNK = 1024
TAIL_SUB_TILE = 256


def _tail_kernel(*refs, attn_out):
    if attn_out:
        o_ref, wo_ref, *refs = refs
    x_ref, p_ref, g_ref, wup_ref, wdn_ref, wg_ref, wp_ref, y_ref = refs
    d_ff = wup_ref.shape[1]
    tiles = [slice(r0, r0 + TAIL_SUB_TILE) for r0 in range(0, x_ref.shape[0], TAIL_SUB_TILE)]

    def mixer_out(rows):
        x = x_ref[rows, :]
        if not attn_out:
            return x
        o_t = jnp.concatenate([o_ref[hh, :, rows] for hh in range(N_HEADS)], axis=0)
        m_t = jnp.dot(wo_ref[...], o_t, preferred_element_type=F32)
        return x + _rms(m_t.T, g_ref[0:1, :])

    def mlp(x):
        h = _rms(x, g_ref[1:2, :]).astype(BF16)
        f = None
        for c in range(d_ff // FF_CHUNK):
            cols = slice(c * FF_CHUNK, (c + 1) * FF_CHUNK)
            u = jnp.dot(h, wup_ref[:, cols], preferred_element_type=F32)
            a = jnp.square(jnp.maximum(u, 0.0)).astype(BF16)
            part = jnp.dot(a, wdn_ref[cols, :], preferred_element_type=F32)
            f = part if f is None else f + part
        return x + _rms(f, g_ref[2:3, :])

    def ple(rows, x):
        h = _rms(x, g_ref[3:4, :]).astype(BF16)
        gate = jax.nn.sigmoid(jnp.dot(h, wg_ref[...], preferred_element_type=F32))
        e = jnp.dot(p_ref[rows, :].astype(BF16), wp_ref[...],
                    preferred_element_type=F32) * gate
        y_ref[rows, :] = x + _rms(e, g_ref[4:5, :])

    xs = [mixer_out(rows) for rows in tiles]
    xs = [mlp(x) for x in xs]
    for rows, x in zip(tiles, xs):
        ple(rows, x)


def _layer_tail(x, p, layer, gains, w_up, w_down, w_gate, w_proj, o_t=None, w_out_t=None,
                *, tm=2 * TAIL_SUB_TILE):
    b, s, d = x.shape
    const = lambda bi, i: (0, 0)
    resident = lambda a: pl.BlockSpec(a.shape, const, pipeline_mode=pl.Buffered(1))
    attn_out = o_t is not None
    operands, specs = [], []
    if attn_out:
        operands += [o_t, w_out_t]
        specs += [pl.BlockSpec((None, N_HEADS, HEAD_DIM, tm), lambda bi, i: (bi, 0, 0, i)),
                  resident(w_out_t)]
    operands += [x, p, gains, w_up, w_down, w_gate, w_proj]
    specs += [
        pl.BlockSpec((None, tm, d), lambda bi, i: (bi, i, 0)),
        pl.BlockSpec((None, None, tm, p.shape[-1]), lambda bi, i: (layer, bi, i, 0)),
        pl.BlockSpec(gains.shape, const),
        resident(w_up), resident(w_down), resident(w_gate), resident(w_proj),
    ]
    return pl.pallas_call(
        functools.partial(_tail_kernel, attn_out=attn_out),
        grid=(b, s // tm),
        in_specs=specs,
        out_specs=pl.BlockSpec((None, tm, d), lambda bi, i: (bi, i, 0)),
        out_shape=jax.ShapeDtypeStruct(x.shape, F32),
        compiler_params=pltpu.CompilerParams(
            dimension_semantics=("arbitrary", "arbitrary"),
            vmem_limit_bytes=VMEM_LIMIT_BYTES),
        name="layer_tail",
    )(*operands)


def _prep_attn_weights(w_in, b_f, w_out):
    d = D_MODEL
    wq_t = w_in[:, 0:d].T.astype(BF16)
    wk = w_in[:, d:2 * d].astype(BF16)
    wv_t = w_in[:, 2 * d:3 * d].T.astype(BF16)
    wf = w_in[:, 3 * d:]
    wf_t = jnp.pad(jnp.concatenate([wf, wf, wf], axis=1),
                   ((0, 0), (0, HEAD_PAD - 3 * N_HEADS))).T.astype(BF16)
    bf_col = jnp.pad(jnp.concatenate([b_f, b_f, b_f]), (0, HEAD_PAD - 3 * N_HEADS))
    bf_col = bf_col.reshape(HEAD_PAD, 1).astype(F32)
    return wq_t, wk, wv_t, wf_t, bf_col, w_out.T.astype(BF16)


def kernel(x, p, norm_g, w_attn_in, b_forget, w_attn_out, w_conv_in, conv_w, w_conv_out,
           w_mlp_up, w_mlp_down, w_ple_proj, w_ple_gate):
    depth = norm_g.shape[0]
    pk = jnp.asarray(_k_placement_matrix(), dtype=BF16)
    for i in range(depth):
        g = norm_g[i].astype(F32)
        gi = lambda n: g[n:n + 1]
        j = i // 2
        if i % 2 == 0:
            wq_t, wk, wv_t, wf_t, bf_col, w_out_t = _prep_attn_weights(
                w_attn_in[j], b_forget[j], w_attn_out[j])
            q_t, k, v_t = _attn_in(x, gi(0), wq_t, wk, wv_t, wf_t, bf_col, pk)
            mixer = dict(o_t=_flash(q_t, k, v_t), w_out_t=w_out_t)
        else:
            x = _conv_layer(x, gi(0), w_conv_in[j].astype(BF16), conv_w[j].astype(F32),
                            w_conv_out[j].astype(BF16), gi(1))
            mixer = {}
        x = _layer_tail(x, p, i, g[1:6], w_mlp_up[i].astype(BF16),
                        w_mlp_down[i].astype(BF16), w_ple_gate[i].astype(BF16),
                        w_ple_proj[i].astype(BF16), **mixer)
    return x
```

```python
import functools

import numpy as np
import jax
import jax.numpy as jnp
from jax import lax
from jax.experimental import pallas as pl
from jax.experimental.pallas import tpu as pltpu

F32 = jnp.float32
BF16 = jnp.bfloat16

D_MODEL = 1024
N_HEADS = 16
HEAD_DIM = D_MODEL // N_HEADS
HEAD_PAD = 128
QK_PAD = N_HEADS * HEAD_PAD
V_ROWS = 128
V_ONES_ROW = HEAD_DIM
CONV_WIDTH = 3
RMS_EPS = 1e-6
NEG_INF = -1e30
LOG2E = 1.4426950408889634
Q_SCALE = (HEAD_DIM ** -0.5) * LOG2E

VMEM_LIMIT_BYTES = 56 * 1024 * 1024

ROW_TILE = 512
ATTN_BLOCK = 2048
V_BLOCK = ROW_TILE
QUERY_TILE = 256
SCORE_LOOKAHEAD = 2
SCORE_ROWS = 512
SCORE_RING = 8

SLOT_CQ = (64, 65, 66)
SLOT_CK = (67, 68, 69)
ROW_ONES = 48


def _rms(xf, g):
    ms = jnp.mean(xf * xf, axis=-1, keepdims=True)
    return xf * lax.rsqrt(ms + RMS_EPS) * g


def _split3(x):
    hi = x.astype(BF16).astype(F32)
    r = x - hi
    mid = r.astype(BF16).astype(F32)
    lo = (r - mid).astype(BF16).astype(F32)
    return hi, mid, lo


def _k_placement_matrix():
    pk = np.zeros((HEAD_PAD, QK_PAD), np.float32)
    for h in range(N_HEADS):
        base = h * HEAD_PAD
        for part in range(3):
            pk[part * N_HEADS + h, base + SLOT_CK[part]] = -1.0
            pk[ROW_ONES, base + SLOT_CQ[part]] = 1.0
    return pk


def _attn_in_kernel(x_ref, g_ref, wq_ref, wk_ref, wv_ref, wf_ref, bf_ref, pk_ref,
                    q_ref, k_ref, v_ref, carry_ref, *, tm):
    @pl.when(pl.program_id(1) == 0)
    def _():
        carry_ref[...] = jnp.zeros_like(carry_ref)

    hn = _rms(x_ref[...], g_ref[...])
    h = hn.astype(BF16)
    kk = jnp.dot(h, wk_ref[...], preferred_element_type=F32)
    ht = hn.T.astype(BF16)

    f = jnp.dot(wf_ref[...], ht, preferred_element_type=F32) + bf_ref[...]

    vt = jnp.dot(wv_ref[...], ht, preferred_element_type=F32)
    pad_rows = jnp.where(
        lax.broadcasted_iota(jnp.int32, (V_ROWS - HEAD_DIM, tm), 0) == 0, 1.0, 0.0)
    for hh in range(N_HEADS):
        v_ref[hh] = jnp.concatenate(
            [vt[hh * HEAD_DIM:(hh + 1) * HEAD_DIM], pad_rows], axis=0).astype(BF16)

    qt = jnp.dot(wq_ref[...], ht, preferred_element_type=F32) * Q_SCALE

    logf = (jnp.minimum(f, 0.0) - jnp.log1p(jnp.exp(-jnp.abs(f)))) * LOG2E
    row = lax.broadcasted_iota(jnp.int32, (tm, tm), 0)
    col = lax.broadcasted_iota(jnp.int32, (tm, tm), 1)
    triu = (row <= col).astype(BF16)
    c = carry_ref[...]
    for part in _split3(logf):
        c = c + jnp.dot(part.astype(BF16), triu, preferred_element_type=F32)
    carry_ref[...] = c[:, tm - 1:tm]
    chi, cmid, clo = _split3(c)

    sub = lax.broadcasted_iota(jnp.int32, (HEAD_PAD, tm), 0)
    bias_t = jnp.where(sub < N_HEADS, chi,
                       jnp.where(sub < 2 * N_HEADS, cmid,
                                 jnp.where(sub < 3 * N_HEADS, clo,
                                           jnp.where(sub == ROW_ONES, 1.0, 0.0))))
    bias = bias_t.T.astype(BF16)
    kb = jnp.dot(bias, pk_ref[...], preferred_element_type=F32)
    low = lax.broadcasted_iota(jnp.int32, (tm, HEAD_PAD), 1) < HEAD_DIM
    for pr in range(N_HEADS // 2):
        pair = kk[:, pr * HEAD_PAD:(pr + 1) * HEAD_PAD]
        for odd in range(2):
            hh = 2 * pr + odd
            feats = pltpu.roll(pair, HEAD_DIM, axis=1) if odd else pair
            k_ref[hh] = (jnp.where(low, feats, 0.0)
                         + kb[:, hh * HEAD_PAD:(hh + 1) * HEAD_PAD]).astype(BF16)

    slot = lax.broadcasted_iota(jnp.int32, (HEAD_PAD - HEAD_DIM, tm), 0) + HEAD_DIM
    ones_slots = functools.reduce(jnp.logical_or, [slot == s for s in SLOT_CK])
    for hh in range(N_HEADS):
        bias_rows = jnp.where(ones_slots, 1.0, 0.0)
        for part, c_part in zip(SLOT_CQ, (chi, cmid, clo)):
            bias_rows = jnp.where(slot == part, c_part[hh:hh + 1], bias_rows)
        q_ref[hh] = jnp.concatenate(
            [qt[hh * HEAD_DIM:(hh + 1) * HEAD_DIM], bias_rows], axis=0).astype(BF16)


def _attn_in(x, g, wq_t, wk, wv_t, wf_t, bf_col, pk, *, tm=ROW_TILE):
    b, s, d = x.shape
    const = lambda bi, i: (0, 0)
    return pl.pallas_call(
        functools.partial(_attn_in_kernel, tm=tm),
        grid=(b, s // tm),
        in_specs=[
            pl.BlockSpec((None, tm, d), lambda bi, i: (bi, i, 0)),
            pl.BlockSpec((1, d), const),
            pl.BlockSpec(wq_t.shape, const),
            pl.BlockSpec(wk.shape, const),
            pl.BlockSpec(wv_t.shape, const),
            pl.BlockSpec(wf_t.shape, const),
            pl.BlockSpec(bf_col.shape, const),
            pl.BlockSpec(pk.shape, const),
        ],
        out_specs=[
            pl.BlockSpec((None, N_HEADS, HEAD_PAD, tm), lambda bi, i: (bi, 0, 0, i)),
            pl.BlockSpec((None, N_HEADS, tm, HEAD_PAD), lambda bi, i: (bi, 0, i, 0)),
            pl.BlockSpec((None, N_HEADS, None, V_ROWS, tm),
                         lambda bi, i: (bi, 0, i // (V_BLOCK // tm), 0, i % (V_BLOCK // tm))),
        ],
        out_shape=[
            jax.ShapeDtypeStruct((b, N_HEADS, HEAD_PAD, s), BF16),
            jax.ShapeDtypeStruct((b, N_HEADS, s, HEAD_PAD), BF16),
            jax.ShapeDtypeStruct((b, N_HEADS, s // V_BLOCK, V_ROWS, V_BLOCK), BF16),
        ],
        scratch_shapes=[pltpu.VMEM((HEAD_PAD, 1), F32)],
        compiler_params=pltpu.CompilerParams(
            dimension_semantics=("arbitrary", "arbitrary"),
            vmem_limit_bytes=VMEM_LIMIT_BYTES),
        name="attn_in",
    )(x, g, wq_t, wk, wv_t, wf_t, bf_col, pk)


def _flash_kernel(q_ref, qnext_ref, k_ref, v_ref, o_ref, s_tiles, smax_tiles, m_ref, acc_ref,
                  *, bk):
    i = pl.program_id(2)
    bq = 2 * bk
    sub = bk // V_BLOCK
    n_tiles = bq // QUERY_TILE
    m_ref[...] = jnp.full_like(m_ref, -jnp.inf)
    acc_ref[...] = jnp.zeros_like(acc_ref)

    def scores_tile(t, pos, c, rows=bk, queries=q_ref):
        cols = slice(c * QUERY_TILE, (c + 1) * QUERY_TILE)
        start = pl.multiple_of(t * bk, bk)
        s_max = None
        for r0 in range(0, rows, SCORE_ROWS):
            n = min(SCORE_ROWS, rows - r0)
            s = jnp.dot(k_ref[pl.ds(start + r0, n), :], queries[:, cols],
                        preferred_element_type=F32)
            s_tiles[pos % SCORE_RING][r0:r0 + n, :] = s
            part = jnp.max(s, axis=0, keepdims=True)
            s_max = part if s_max is None else jnp.maximum(s_max, part)
        smax_tiles[pos % SCORE_RING][...] = s_max

    def consume_tile(t, pos, c, rows=bk, key_offset=None):
        cols = slice(c * QUERY_TILE, (c + 1) * QUERY_TILE)
        s = s_tiles[pos % SCORE_RING][0:rows, :]
        s_max = smax_tiles[pos % SCORE_RING][...]
        if key_offset is not None:
            key = lax.broadcasted_iota(jnp.int32, (rows, QUERY_TILE), 0) + key_offset
            qry = lax.broadcasted_iota(jnp.int32, (rows, QUERY_TILE), 1) + c * QUERY_TILE
            s = jnp.where(key <= qry, s, NEG_INF)
            s_max = jnp.max(s, axis=0, keepdims=True)
        m_prev = m_ref[:, cols]
        m_new = jnp.maximum(m_prev, s_max)
        alpha = jnp.exp2(m_prev - m_new)
        p = jnp.exp2(s - m_new).astype(BF16)
        pv = None
        for u in range(pl.cdiv(rows, V_BLOCK)):
            n = min(V_BLOCK, rows - u * V_BLOCK)
            part = jnp.dot(v_ref[t * sub + u, :, 0:n], p[u * V_BLOCK:u * V_BLOCK + n, :],
                           preferred_element_type=F32)
            pv = part if pv is None else pv + part
        acc_ref[:, cols] = alpha * acc_ref[:, cols] + pv
        m_ref[:, cols] = m_new

    diag = []
    for r in range(2):
        for c in range(n_tiles):
            rows = min(bk, (c + 1) * QUERY_TILE - r * bk)
            if rows > 0:
                diag.append((r, c, rows, rows < bk or c * QUERY_TILE < (r + 1) * bk))

    la = SCORE_LOOKAHEAD
    assert la < SCORE_RING and n_tiles % SCORE_RING == 0 and len(diag) % SCORE_RING == 0

    @pl.when(i == 0)
    def _():
        for pos, (r, c, rows, _) in enumerate(diag[:la]):
            scores_tile(r, pos, c, rows)

    def body(g, carry):
        t = 2 * g
        for pos in range(2 * n_tiles):
            consume_tile(t + pos // n_tiles, pos, pos % n_tiles)
            ahead = pos + la
            scores_tile(t + ahead // n_tiles, ahead, ahead % n_tiles)
        return carry

    lax.fori_loop(0, i, body, 0)
    t = 2 * i
    for pos, (r, c, rows, masked) in enumerate(diag):
        consume_tile(t + r, pos, c, rows, key_offset=r * bk if masked else None)
        if pos + la < len(diag):
            r2, c2, rows2, _ = diag[pos + la]
            scores_tile(t + r2, pos + la, c2, rows2)
        else:
            scores_tile(0, pos + la, pos + la - len(diag), queries=qnext_ref)

    acc = acc_ref[...]
    o_ref[...] = (acc[0:HEAD_DIM] / acc[V_ONES_ROW:V_ONES_ROW + 1]).astype(o_ref.dtype)


def _flash(q_t, k, v_t, *, bk=ATTN_BLOCK):
    b, nh, s, hp = k.shape
    bq = 2 * bk
    assert v_t.shape == (b, nh, s // V_BLOCK, V_ROWS, V_BLOCK)
    score_tiles = [pltpu.VMEM((bk, QUERY_TILE), F32) for _ in range(SCORE_RING)]
    smax_tiles = [pltpu.VMEM((1, QUERY_TILE), F32) for _ in range(SCORE_RING)]
    n_steps = s // bq
    return pl.pallas_call(
        functools.partial(_flash_kernel, bk=bk),
        grid=(b, nh, s // bq),
        in_specs=[
            pl.BlockSpec((None, None, hp, bq), lambda bi, h, i: (bi, h, 0, i)),
            pl.BlockSpec((None, None, hp, bq),
                         lambda bi, h, i: (bi, h, 0, jnp.minimum(i + 1, n_steps - 1))),
            pl.BlockSpec((None, None, s, hp), lambda bi, h, i: (bi, h, 0, 0)),
            pl.BlockSpec((None, None, s // V_BLOCK, V_ROWS, V_BLOCK),
                         lambda bi, h, i: (bi, h, 0, 0, 0)),
        ],
        out_specs=pl.BlockSpec((None, None, HEAD_DIM, bq), lambda bi, h, i: (bi, h, 0, i)),
        out_shape=jax.ShapeDtypeStruct((b, nh, HEAD_DIM, s), BF16),
        scratch_shapes=[score_tiles, smax_tiles,
                        pltpu.VMEM((1, bq), F32), pltpu.VMEM((V_ROWS, bq), F32)],
        compiler_params=pltpu.CompilerParams(
            dimension_semantics=("arbitrary", "arbitrary", "arbitrary"),
            vmem_limit_bytes=VMEM_LIMIT_BYTES),
        name="fox_flash",
    )(q_t, q_t, k, v_t)


HALO = 8


def _conv_kernel(x_ref, g0_ref, win_ref, cw_ref, wout_ref, g1_ref, y_ref, z_ref, *, tm):
    d = D_MODEL

    @pl.when(pl.program_id(1) == 0)
    def _():
        z_ref[0:HALO, :] = jnp.zeros((HALO, d), F32)

    @pl.when(pl.program_id(1) > 0)
    def _():
        z_ref[0:HALO, :] = z_ref[tm:tm + HALO, :]

    def in_proj(r0):
        x = x_ref[r0:r0 + ROW_TILE, :]
        h = _rms(x, g0_ref[...]).astype(BF16)
        c_gate = jnp.dot(h, win_ref[:, d:2 * d], preferred_element_type=F32)
        u = jnp.dot(h, win_ref[:, 2 * d:3 * d], preferred_element_type=F32)
        z_ref[HALO + r0:HALO + r0 + ROW_TILE, :] = c_gate * u
        return x, jnp.dot(h, win_ref[:, 0:d], preferred_element_type=F32)

    def out_proj(r0, x, b_gate):
        cw = cw_ref[...]
        zc = sum(cw[k:k + 1, :] * z_ref[pl.ds(HALO + r0 - (CONV_WIDTH - 1 - k), ROW_TILE), :]
                 for k in range(CONV_WIDTH))
        y = (b_gate * zc).astype(BF16)
        m = jnp.dot(y, wout_ref[...], preferred_element_type=F32)
        y_ref[r0:r0 + ROW_TILE, :] = x + _rms(m, g1_ref[...])

    starts = range(0, tm, ROW_TILE)
    staged = [in_proj(r0) for r0 in starts]
    for r0, (x, b_gate) in zip(starts, staged):
        out_proj(r0, x, b_gate)


def _conv_layer(x, g0, w_in, conv_w, w_out, g1, *, tm=2 * ROW_TILE):
    b, s, d = x.shape
    const = lambda bi, i: (0, 0)
    resident = lambda a: pl.BlockSpec(a.shape, const, pipeline_mode=pl.Buffered(1))
    return pl.pallas_call(
        functools.partial(_conv_kernel, tm=tm),
        grid=(b, s // tm),
        in_specs=[
            pl.BlockSpec((None, tm, d), lambda bi, i: (bi, i, 0)),
            pl.BlockSpec((1, d), const),
            resident(w_in),
            pl.BlockSpec(conv_w.shape, const),
            resident(w_out),
            pl.BlockSpec((1, d), const),
        ],
        out_specs=pl.BlockSpec((None, tm, d), lambda bi, i: (bi, i, 0)),
        out_shape=jax.ShapeDtypeStruct(x.shape, F32),
        scratch_shapes=[pltpu.VMEM((tm + HALO, d), F32)],
        compiler_params=pltpu.CompilerParams(
            dimension_semantics=("arbitrary", "arbitrary"),
            vmem_limit_bytes=VMEM_LIMIT_BYTES),
        name="conv_mixer",
    )(x, g0, w_in, conv_w, w_out, g1)


FF_CHUNK = 1024
TAIL_SUB_TILE = 256


def _tail_kernel(*refs, attn_out):
    if attn_out:
        o_ref, wo_ref, *refs = refs
    x_ref, p_ref, g_ref, wup_ref, wdn_ref, wg_ref, wp_ref, y_ref = refs
    d_ff = wup_ref.shape[1]
    tiles = [slice(r0, r0 + TAIL_SUB_TILE) for r0 in range(0, x_ref.shape[0], TAIL_SUB_TILE)]

    def mixer_out(rows):
        x = x_ref[rows, :]
        if not attn_out:
            return x
        o_t = jnp.concatenate([o_ref[hh, :, rows] for hh in range(N_HEADS)], axis=0)
        m_t = jnp.dot(wo_ref[...], o_t, preferred_element_type=F32)
        return x + _rms(m_t.T, g_ref[0:1, :])

    def mlp(x):
        h = _rms(x, g_ref[1:2, :]).astype(BF16)
        f = None
        for c in range(d_ff // FF_CHUNK):
            cols = slice(c * FF_CHUNK, (c + 1) * FF_CHUNK)
            u = jnp.dot(h, wup_ref[:, cols], preferred_element_type=F32)
            a = jnp.square(jnp.maximum(u, 0.0)).astype(BF16)
            part = jnp.dot(a, wdn_ref[cols, :], preferred_element_type=F32)
            f = part if f is None else f + part
        return x + _rms(f, g_ref[2:3, :])

    def ple(rows, x):
        h = _rms(x, g_ref[3:4, :]).astype(BF16)
        gate = jax.nn.sigmoid(jnp.dot(h, wg_ref[...], preferred_element_type=F32))
        e = jnp.dot(p_ref[rows, :].astype(BF16), wp_ref[...],
                    preferred_element_type=F32) * gate
        y_ref[rows, :] = x + _rms(e, g_ref[4:5, :])

    xs = [mixer_out(rows) for rows in tiles]
    xs = [mlp(x) for x in xs]
    for rows, x in zip(tiles, xs):
        ple(rows, x)


def _layer_tail(x, p, layer, gains, w_up, w_down, w_gate, w_proj, o_t=None, w_out_t=None,
                *, tm=2 * TAIL_SUB_TILE):
    b, s, d = x.shape
    const = lambda bi, i: (0, 0)
    resident = lambda a: pl.BlockSpec(a.shape, const, pipeline_mode=pl.Buffered(1))
    attn_out = o_t is not None
    operands, specs = [], []
    if attn_out:
        operands += [o_t, w_out_t]
        specs += [pl.BlockSpec((None, N_HEADS, HEAD_DIM, tm), lambda bi, i: (bi, 0, 0, i)),
                  resident(w_out_t)]
    operands += [x, p, gains, w_up, w_down, w_gate, w_proj]
    specs += [
        pl.BlockSpec((None, tm, d), lambda bi, i: (bi, i, 0)),
        pl.BlockSpec((None, None, tm, p.shape[-1]), lambda bi, i: (layer, bi, i, 0)),
        pl.BlockSpec(gains.shape, const),
        resident(w_up), resident(w_down), resident(w_gate), resident(w_proj),
    ]
    return pl.pallas_call(
        functools.partial(_tail_kernel, attn_out=attn_out),
        grid=(b, s // tm),
        in_specs=specs,
        out_specs=pl.BlockSpec((None, tm, d), lambda bi, i: (bi, i, 0)),
        out_shape=jax.ShapeDtypeStruct(x.shape, F32),
        compiler_params=pltpu.CompilerParams(
            dimension_semantics=("arbitrary", "arbitrary"),
            vmem_limit_bytes=VMEM_LIMIT_BYTES),
        name="layer_tail",
    )(*operands)


def _prep_attn_weights(w_in, b_f, w_out):
    d = D_MODEL
    wq_t = w_in[:, 0:d].T.astype(BF16)
    wk = w_in[:, d:2 * d].astype(BF16)
    wv_t = w_in[:, 2 * d:3 * d].T.astype(BF16)
    wf = w_in[:, 3 * d:]
    wf_t = jnp.pad(jnp.concatenate([wf, wf, wf], axis=1),
                   ((0, 0), (0, HEAD_PAD - 3 * N_HEADS))).T.astype(BF16)
    bf_col = jnp.pad(jnp.concatenate([b_f, b_f, b_f]), (0, HEAD_PAD - 3 * N_HEADS))
    bf_col = bf_col.reshape(HEAD_PAD, 1).astype(F32)
    return wq_t, wk, wv_t, wf_t, bf_col, w_out.T.astype(BF16)


def kernel(x, p, norm_g, w_attn_in, b_forget, w_attn_out, w_conv_in, conv_w, w_conv_out,
           w_mlp_up, w_mlp_down, w_ple_proj, w_ple_gate):
    depth = norm_g.shape[0]
    pk = jnp.asarray(_k_placement_matrix(), dtype=BF16)
    for i in range(depth):
        g = norm_g[i].astype(F32)
        gi = lambda n: g[n:n + 1]
        j = i // 2
        if i % 2 == 0:
            wq_t, wk, wv_t, wf_t, bf_col, w_out_t = _prep_attn_weights(
                w_attn_in[j], b_forget[j], w_attn_out[j])
            q_t, k, v_t = _attn_in(x, gi(0), wq_t, wk, wv_t, wf_t, bf_col, pk)
            mixer = dict(o_t=_flash(q_t, k, v_t), w_out_t=w_out_t)
        else:
            x = _conv_layer(x, gi(0), w_conv_in[j].astype(BF16), conv_w[j].astype(F32),
                            w_conv_out[j].astype(BF16), gi(1))
            mixer = {}
        x = _layer_tail(x, p, i, g[1:6], w_mlp_up[i].astype(BF16),
                        w_mlp_down[i].astype(BF16), w_ple_gate[i].astype(BF16),
                        w_ple_proj[i].astype(BF16), **mixer)
    return x
```

```python
import functools

import numpy as np
import jax
import jax.numpy as jnp
from jax import lax
from jax.experimental import pallas as pl
from jax.experimental.pallas import tpu as pltpu

F32 = jnp.float32
BF16 = jnp.bfloat16

D_MODEL = 1024
N_HEADS = 16
HEAD_DIM = D_MODEL // N_HEADS
HEAD_PAD = 128
QK_PAD = N_HEADS * HEAD_PAD
V_ROWS = 128
V_ONES_ROW = HEAD_DIM
CONV_WIDTH = 3
RMS_EPS = 1e-6
NEG_INF = -1e30
LOG2E = 1.4426950408889634
Q_SCALE = (HEAD_DIM ** -0.5) * LOG2E

VMEM_LIMIT_BYTES = 56 * 1024 * 1024

ROW_TILE = 512
ATTN_BLOCK = 2048
V_BLOCK = ROW_TILE
QUERY_TILE = 256
SCORE_LOOKAHEAD = 2
SCORE_ROWS = 512
SCORE_RING = 8

SLOT_CQ = (64, 65, 66)
SLOT_CK = (67, 68, 69)
ROW_ONES = 48


def _rms(xf, g):
    ms = jnp.mean(xf * xf, axis=-1, keepdims=True)
    return xf * lax.rsqrt(ms + RMS_EPS) * g


def _split3(x):
    hi = x.astype(BF16).astype(F32)
    r = x - hi
    mid = r.astype(BF16).astype(F32)
    lo = (r - mid).astype(BF16).astype(F32)
    return hi, mid, lo


def _k_placement_matrix():
    pk = np.zeros((HEAD_PAD, QK_PAD), np.float32)
    for h in range(N_HEADS):
        base = h * HEAD_PAD
        for part in range(3):
            pk[part * N_HEADS + h, base + SLOT_CK[part]] = -1.0
            pk[ROW_ONES, base + SLOT_CQ[part]] = 1.0
    return pk


def _attn_in_kernel(x_ref, g_ref, wq_ref, wk_ref, wv_ref, wf_ref, bf_ref, pk_ref,
                    q_ref, k_ref, v_ref, carry_ref, *, tm):
    @pl.when(pl.program_id(1) == 0)
    def _():
        carry_ref[...] = jnp.zeros_like(carry_ref)

    hn = _rms(x_ref[...], g_ref[...])
    h = hn.astype(BF16)
    kk = jnp.dot(h, wk_ref[...], preferred_element_type=F32)
    ht = hn.T.astype(BF16)

    f = jnp.dot(wf_ref[...], ht, preferred_element_type=F32) + bf_ref[...]

    vt = jnp.dot(wv_ref[...], ht, preferred_element_type=F32)
    pad_rows = jnp.where(
        lax.broadcasted_iota(jnp.int32, (V_ROWS - HEAD_DIM, tm), 0) == 0, 1.0, 0.0)
    for hh in range(N_HEADS):
        v_ref[hh] = jnp.concatenate(
            [vt[hh * HEAD_DIM:(hh + 1) * HEAD_DIM], pad_rows], axis=0).astype(BF16)

    qt = jnp.dot(wq_ref[...], ht, preferred_element_type=F32) * Q_SCALE

    logf = (jnp.minimum(f, 0.0) - jnp.log1p(jnp.exp(-jnp.abs(f)))) * LOG2E
    row = lax.broadcasted_iota(jnp.int32, (tm, tm), 0)
    col = lax.broadcasted_iota(jnp.int32, (tm, tm), 1)
    triu = (row <= col).astype(BF16)
    c = carry_ref[...]
    for part in _split3(logf):
        c = c + jnp.dot(part.astype(BF16), triu, preferred_element_type=F32)
    carry_ref[...] = c[:, tm - 1:tm]
    chi, cmid, clo = _split3(c)

    sub = lax.broadcasted_iota(jnp.int32, (HEAD_PAD, tm), 0)
    bias_t = jnp.where(sub < N_HEADS, chi,
                       jnp.where(sub < 2 * N_HEADS, cmid,
                                 jnp.where(sub < 3 * N_HEADS, clo,
                                           jnp.where(sub == ROW_ONES, 1.0, 0.0))))
    bias = bias_t.T.astype(BF16)
    kb = jnp.dot(bias, pk_ref[...], preferred_element_type=F32)
    low = lax.broadcasted_iota(jnp.int32, (tm, HEAD_PAD), 1) < HEAD_DIM
    for pr in range(N_HEADS // 2):
        pair = kk[:, pr * HEAD_PAD:(pr + 1) * HEAD_PAD]
        for odd in range(2):
            hh = 2 * pr + odd
            feats = pltpu.roll(pair, HEAD_DIM, axis=1) if odd else pair
            k_ref[hh] = (jnp.where(low, feats, 0.0)
                         + kb[:, hh * HEAD_PAD:(hh + 1) * HEAD_PAD]).astype(BF16)

    slot = lax.broadcasted_iota(jnp.int32, (HEAD_PAD - HEAD_DIM, tm), 0) + HEAD_DIM
    ones_slots = functools.reduce(jnp.logical_or, [slot == s for s in SLOT_CK])
    for hh in range(N_HEADS):
        bias_rows = jnp.where(ones_slots, 1.0, 0.0)
        for part, c_part in zip(SLOT_CQ, (chi, cmid, clo)):
            bias_rows = jnp.where(slot == part, c_part[hh:hh + 1], bias_rows)
        q_ref[hh] = jnp.concatenate(
            [qt[hh * HEAD_DIM:(hh + 1) * HEAD_DIM], bias_rows], axis=0).astype(BF16)


def _attn_in(x, g, wq_t, wk, wv_t, wf_t, bf_col, pk, *, tm=ROW_TILE):
    b, s, d = x.shape
    const = lambda bi, i: (0, 0)
    return pl.pallas_call(
        functools.partial(_attn_in_kernel, tm=tm),
        grid=(b, s // tm),
        in_specs=[
            pl.BlockSpec((None, tm, d), lambda bi, i: (bi, i, 0)),
            pl.BlockSpec((1, d), const),
            pl.BlockSpec(wq_t.shape, const),
            pl.BlockSpec(wk.shape, const),
            pl.BlockSpec(wv_t.shape, const),
            pl.BlockSpec(wf_t.shape, const),
            pl.BlockSpec(bf_col.shape, const),
            pl.BlockSpec(pk.shape, const),
        ],
        out_specs=[
            pl.BlockSpec((None, N_HEADS, HEAD_PAD, tm), lambda bi, i: (bi, 0, 0, i)),
            pl.BlockSpec((None, N_HEADS, tm, HEAD_PAD), lambda bi, i: (bi, 0, i, 0)),
            pl.BlockSpec((None, N_HEADS, None, V_ROWS, tm),
                         lambda bi, i: (bi, 0, i // (V_BLOCK // tm), 0, i % (V_BLOCK // tm))),
        ],
        out_shape=[
            jax.ShapeDtypeStruct((b, N_HEADS, HEAD_PAD, s), BF16),
            jax.ShapeDtypeStruct((b, N_HEADS, s, HEAD_PAD), BF16),
            jax.ShapeDtypeStruct((b, N_HEADS, s // V_BLOCK, V_ROWS, V_BLOCK), BF16),
        ],
        scratch_shapes=[pltpu.VMEM((HEAD_PAD, 1), F32)],
        compiler_params=pltpu.CompilerParams(
            dimension_semantics=("arbitrary", "arbitrary"),
            vmem_limit_bytes=VMEM_LIMIT_BYTES),
        name="attn_in",
    )(x, g, wq_t, wk, wv_t, wf_t, bf_col, pk)


def _flash_kernel(q_ref, qnext_ref, k_ref, v_ref, o_ref, s_tiles, smax_tiles, m_ref, acc_ref,
                  *, bk):
    i = pl.program_id(2)
    bq = 2 * bk
    sub = bk // V_BLOCK
    n_tiles = bq // QUERY_TILE
    m_ref[...] = jnp.full_like(m_ref, -jnp.inf)
    acc_ref[...] = jnp.zeros_like(acc_ref)

    def scores_tile(t, pos, c, rows=bk, queries=q_ref):
        cols = slice(c * QUERY_TILE, (c + 1) * QUERY_TILE)
        start = pl.multiple_of(t * bk, bk)
        s_max = None
        for r0 in range(0, rows, SCORE_ROWS):
            n = min(SCORE_ROWS, rows - r0)
            s = jnp.dot(k_ref[pl.ds(start + r0, n), :], queries[:, cols],
                        preferred_element_type=F32)
            s_tiles[pos % SCORE_RING][r0:r0 + n, :] = s
            part = jnp.max(s, axis=0, keepdims=True)
            s_max = part if s_max is None else jnp.maximum(s_max, part)
        smax_tiles[pos % SCORE_RING][...] = s_max

    def consume_tile(t, pos, c, rows=bk, key_offset=None):
        cols = slice(c * QUERY_TILE, (c + 1) * QUERY_TILE)
        s = s_tiles[pos % SCORE_RING][0:rows, :]
        s_max = smax_tiles[pos % SCORE_RING][...]
        if key_offset is not None:
            key = lax.broadcasted_iota(jnp.int32, (rows, QUERY_TILE), 0) + key_offset
            qry = lax.broadcasted_iota(jnp.int32, (rows, QUERY_TILE), 1) + c * QUERY_TILE
            s = jnp.where(key <= qry, s, NEG_INF)
            s_max = jnp.max(s, axis=0, keepdims=True)
        m_prev = m_ref[:, cols]
        m_new = jnp.maximum(m_prev, s_max)
        alpha = jnp.exp2(m_prev - m_new)
        p = jnp.exp2(s - m_new).astype(BF16)
        pv = None
        for u in range(pl.cdiv(rows, V_BLOCK)):
            n = min(V_BLOCK, rows - u * V_BLOCK)
            part = jnp.dot(v_ref[t * sub + u, :, 0:n], p[u * V_BLOCK:u * V_BLOCK + n, :],
                           preferred_element_type=F32)
            pv = part if pv is None else pv + part
        acc_ref[:, cols] = alpha * acc_ref[:, cols] + pv
        m_ref[:, cols] = m_new

    diag = []
    for r in range(2):
        for c in range(n_tiles):
            rows = min(bk, (c + 1) * QUERY_TILE - r * bk)
            if rows > 0:
                diag.append((r, c, rows, rows < bk or c * QUERY_TILE < (r + 1) * bk))

    la = SCORE_LOOKAHEAD
    rest = diag[la:]
    diag = diag[:la] + [d for d in rest if not d[3]] + [d for d in rest if d[3]]
    assert la < SCORE_RING and n_tiles % SCORE_RING == 0 and len(diag) % SCORE_RING == 0

    @pl.when(i == 0)
    def _():
        for pos, (r, c, rows, _) in enumerate(diag[:la]):
            scores_tile(r, pos, c, rows)

    def body(g, carry):
        t = 2 * g
        for pos in range(2 * n_tiles):
            consume_tile(t + pos // n_tiles, pos, pos % n_tiles)
            ahead = pos + la
            scores_tile(t + ahead // n_tiles, ahead, ahead % n_tiles)
        return carry

    lax.fori_loop(0, i, body, 0)
    t = 2 * i
    for pos, (r, c, rows, masked) in enumerate(diag):
        consume_tile(t + r, pos, c, rows, key_offset=r * bk if masked else None)
        if pos + la < len(diag):
            r2, c2, rows2, _ = diag[pos + la]
            scores_tile(t + r2, pos + la, c2, rows2)
        else:
            scores_tile(0, pos + la, pos + la - len(diag), queries=qnext_ref)

    acc = acc_ref[...]
    o_ref[...] = (acc[0:HEAD_DIM] / acc[V_ONES_ROW:V_ONES_ROW + 1]).astype(o_ref.dtype)


def _flash(q_t, k, v_t, *, bk=ATTN_BLOCK):
    b, nh, s, hp = k.shape
    bq = 2 * bk
    assert v_t.shape == (b, nh, s // V_BLOCK, V_ROWS, V_BLOCK)
    score_tiles = [pltpu.VMEM((bk, QUERY_TILE), F32) for _ in range(SCORE_RING)]
    smax_tiles = [pltpu.VMEM((1, QUERY_TILE), F32) for _ in range(SCORE_RING)]
    n_steps = s // bq
    return pl.pallas_call(
        functools.partial(_flash_kernel, bk=bk),
        grid=(b, nh, s // bq),
        in_specs=[
            pl.BlockSpec((None, None, hp, bq), lambda bi, h, i: (bi, h, 0, i)),
            pl.BlockSpec((None, None, hp, bq),
                         lambda bi, h, i: (bi, h, 0, jnp.minimum(i + 1, n_steps - 1))),
            pl.BlockSpec((None, None, s, hp), lambda bi, h, i: (bi, h, 0, 0)),
            pl.BlockSpec((None, None, s // V_BLOCK, V_ROWS, V_BLOCK),
                         lambda bi, h, i: (bi, h, 0, 0, 0)),
        ],
        out_specs=pl.BlockSpec((None, None, HEAD_DIM, bq), lambda bi, h, i: (bi, h, 0, i)),
        out_shape=jax.ShapeDtypeStruct((b, nh, HEAD_DIM, s), BF16),
        scratch_shapes=[score_tiles, smax_tiles,
                        pltpu.VMEM((1, bq), F32), pltpu.VMEM((V_ROWS, bq), F32)],
        compiler_params=pltpu.CompilerParams(
            dimension_semantics=("arbitrary", "arbitrary", "arbitrary"),
            vmem_limit_bytes=VMEM_LIMIT_BYTES),
        name="fox_flash",
    )(q_t, q_t, k, v_t)


HALO = 8


def _conv_kernel(x_ref, g0_ref, win_ref, cw_ref, wout_ref, g1_ref, y_ref, z_ref, *, tm):
    d = D_MODEL

    @pl.when(pl.program_id(1) == 0)
    def _():
        z_ref[0:HALO, :] = jnp.zeros((HALO, d), F32)

    @pl.when(pl.program_id(1) > 0)
    def _():
        z_ref[0:HALO, :] = z_ref[tm:tm + HALO, :]

    def in_proj(r0):
        x = x_ref[r0:r0 + ROW_TILE, :]
        h = _rms(x, g0_ref[...]).astype(BF16)
        c_gate = jnp.dot(h, win_ref[:, d:2 * d], preferred_element_type=F32)
        u = jnp.dot(h, win_ref[:, 2 * d:3 * d], preferred_element_type=F32)
        z_ref[HALO + r0:HALO + r0 + ROW_TILE, :] = c_gate * u
        return x, jnp.dot(h, win_ref[:, 0:d], preferred_element_type=F32)

    def out_proj(r0, x, b_gate):
        cw = cw_ref[...]
        zc = sum(cw[k:k + 1, :] * z_ref[pl.ds(HALO + r0 - (CONV_WIDTH - 1 - k), ROW_TILE), :]
                 for k in range(CONV_WIDTH))
        y = (b_gate * zc).astype(BF16)
        m = jnp.dot(y, wout_ref[...], preferred_element_type=F32)
        y_ref[r0:r0 + ROW_TILE, :] = x + _rms(m, g1_ref[...])

    starts = range(0, tm, ROW_TILE)
    staged = [in_proj(r0) for r0 in starts]
    for r0, (x, b_gate) in zip(starts, staged):
        out_proj(r0, x, b_gate)


def _conv_layer(x, g0, w_in, conv_w, w_out, g1, *, tm=2 * ROW_TILE):
    b, s, d = x.shape
    const = lambda bi, i: (0, 0)
    resident = lambda a: pl.BlockSpec(a.shape, const, pipeline_mode=pl.Buffered(1))
    return pl.pallas_call(
        functools.partial(_conv_kernel, tm=tm),
        grid=(b, s // tm),
        in_specs=[
            pl.BlockSpec((None, tm, d), lambda bi, i: (bi, i, 0)),
            pl.BlockSpec((1, d), const),
            resident(w_in),
            pl.BlockSpec(conv_w.shape, const),
            resident(w_out),
            pl.BlockSpec((1, d), const),
        ],
        out_specs=pl.BlockSpec((None, tm, d), lambda bi, i: (bi, i, 0)),
        out_shape=jax.ShapeDtypeStruct(x.shape, F32),
        scratch_shapes=[pltpu.VMEM((tm + HALO, d), F32)],
        compiler_params=pltpu.CompilerParams(
            dimension_semantics=("arbitrary", "arbitrary"),
            vmem_limit_bytes=VMEM_LIMIT_BYTES),
        name="conv_mixer",
    )(x, g0, w_in, conv_w, w_out, g1)


FF_CHUNK = 1024
TAIL_SUB_TILE = 256


def _tail_kernel(*refs, attn_out):
    if attn_out:
        o_ref, wo_ref, *refs = refs
    x_ref, p_ref, g_ref, wup_ref, wdn_ref, wg_ref, wp_ref, y_ref = refs
    d_ff = wup_ref.shape[1]
    tiles = [slice(r0, r0 + TAIL_SUB_TILE) for r0 in range(0, x_ref.shape[0], TAIL_SUB_TILE)]

    def mixer_out(rows):
        x = x_ref[rows, :]
        if not attn_out:
            return x
        o_t = jnp.concatenate([o_ref[hh, :, rows] for hh in range(N_HEADS)], axis=0)
        m_t = jnp.dot(wo_ref[...], o_t, preferred_element_type=F32)
        return x + _rms(m_t.T, g_ref[0:1, :])

    def mlp(x):
        h = _rms(x, g_ref[1:2, :]).astype(BF16)
        f = None
        for c in range(d_ff // FF_CHUNK):
            cols = slice(c * FF_CHUNK, (c + 1) * FF_CHUNK)
            u = jnp.dot(h, wup_ref[:, cols], preferred_element_type=F32)
            a = jnp.square(jnp.maximum(u, 0.0)).astype(BF16)
            part = jnp.dot(a, wdn_ref[cols, :], preferred_element_type=F32)
            f = part if f is None else f + part
        return x + _rms(f, g_ref[2:3, :])

    def ple(rows, x):
        h = _rms(x, g_ref[3:4, :]).astype(BF16)
        gate = jax.nn.sigmoid(jnp.dot(h, wg_ref[...], preferred_element_type=F32))
        e = jnp.dot(p_ref[rows, :].astype(BF16), wp_ref[...],
                    preferred_element_type=F32) * gate
        y_ref[rows, :] = x + _rms(e, g_ref[4:5, :])

    xs = [mixer_out(rows) for rows in tiles]
    xs = [mlp(x) for x in xs]
    for rows, x in zip(tiles, xs):
        ple(rows, x)


def _layer_tail(x, p, layer, gains, w_up, w_down, w_gate, w_proj, o_t=None, w_out_t=None,
                *, tm=2 * TAIL_SUB_TILE):
    b, s, d = x.shape
    const = lambda bi, i: (0, 0)
    resident = lambda a: pl.BlockSpec(a.shape, const, pipeline_mode=pl.Buffered(1))
    attn_out = o_t is not None
    operands, specs = [], []
    if attn_out:
        operands += [o_t, w_out_t]
        specs += [pl.BlockSpec((None, N_HEADS, HEAD_DIM, tm), lambda bi, i: (bi, 0, 0, i)),
                  resident(w_out_t)]
    operands += [x, p, gains, w_up, w_down, w_gate, w_proj]
    specs += [
        pl.BlockSpec((None, tm, d), lambda bi, i: (bi, i, 0)),
        pl.BlockSpec((None, None, tm, p.shape[-1]), lambda bi, i: (layer, bi, i, 0)),
        pl.BlockSpec(gains.shape, const),
        resident(w_up), resident(w_down), resident(w_gate), resident(w_proj),
    ]
    return pl.pallas_call(
        functools.partial(_tail_kernel, attn_out=attn_out),
        grid=(b, s // tm),
        in_specs=specs,
        out_specs=pl.BlockSpec((None, tm, d), lambda bi, i: (bi, i, 0)),
        out_shape=jax.ShapeDtypeStruct(x.shape, F32),
        compiler_params=pltpu.CompilerParams(
            dimension_semantics=("arbitrary", "arbitrary"),
            vmem_limit_bytes=VMEM_LIMIT_BYTES),
        name="layer_tail",
    )(*operands)


def _prep_attn_weights(w_in, b_f, w_out):
    d = D_MODEL
    wq_t = w_in[:, 0:d].T.astype(BF16)
    wk = w_in[:, d:2 * d].astype(BF16)
    wv_t = w_in[:, 2 * d:3 * d].T.astype(BF16)
    wf = w_in[:, 3 * d:]
    wf_t = jnp.pad(jnp.concatenate([wf, wf, wf], axis=1),
                   ((0, 0), (0, HEAD_PAD - 3 * N_HEADS))).T.astype(BF16)
    bf_col = jnp.pad(jnp.concatenate([b_f, b_f, b_f]), (0, HEAD_PAD - 3 * N_HEADS))
    bf_col = bf_col.reshape(HEAD_PAD, 1).astype(F32)
    return wq_t, wk, wv_t, wf_t, bf_col, w_out.T.astype(BF16)


def kernel(x, p, norm_g, w_attn_in, b_forget, w_attn_out, w_conv_in, conv_w, w_conv_out,
           w_mlp_up, w_mlp_down, w_ple_proj, w_ple_gate):
    depth = norm_g.shape[0]
    pk = jnp.asarray(_k_placement_matrix(), dtype=BF16)
    for i in range(depth):
        g = norm_g[i].astype(F32)
        gi = lambda n: g[n:n + 1]
        j = i // 2
        if i % 2 == 0:
            wq_t, wk, wv_t, wf_t, bf_col, w_out_t = _prep_attn_weights(
                w_attn_in[j], b_forget[j], w_attn_out[j])
            q_t, k, v_t = _attn_in(x, gi(0), wq_t, wk, wv_t, wf_t, bf_col, pk)
            mixer = dict(o_t=_flash(q_t, k, v_t), w_out_t=w_out_t)
        else:
            x = _conv_layer(x, gi(0), w_conv_in[j].astype(BF16), conv_w[j].astype(F32),
                            w_conv_out[j].astype(BF16), gi(1))
            mixer = {}
        x = _layer_tail(x, p, i, g[1:6], w_mlp_up[i].astype(BF16),
                        w_mlp_down[i].astype(BF16), w_ple_gate[i].astype(BF16),
                        w_ple_proj[i].astype(BF16), **mixer)
    return x
```
